```python
import jax
import jax.numpy as jnp
from jax import lax
import numpy as np

D_MODEL = 1024
BATCH = 8
SEQ = 4096
DEPTH = 2
DEC_BATCH = 32
DEC_SEQ = 4
PAST_LEN = 16384
PAGE_SIZE = 128

D_HEAD = 64
H_SB = 8
H_DSA = 8
KV_DSA = 2
H_IDX = 8
D_IDX = 64
DSA_TOPK = 256
Q_BLOCK = 128
ROPE_THETA = 10000.0
RK_N = 64
RK_H = D_MODEL // RK_N
RK_DECAY_LORA = 64
RK_AAA_LORA = 64
RK_GATE_LORA = 160
RK_GN_EPS = 64e-5
D_FF = 2816
CONV_W = 3
NORM_EPS = 1e-6
N_ATT_LAYERS = (DEPTH + 1) // 2
N_RWKV_LAYERS = DEPTH // 2
ATT_WIDTH = (H_SB + H_DSA) * D_HEAD
SPLIT_SIZES = (H_SB * D_HEAD, H_SB * D_HEAD, H_SB * D_HEAD,
               H_DSA * D_HEAD, KV_DSA * D_HEAD, KV_DSA * D_HEAD,
               H_IDX * D_IDX, D_IDX, H_IDX)
SPLIT_POINTS = [int(s) for s in np.cumsum(SPLIT_SIZES)[:-1]]
ATT_IN = int(sum(SPLIT_SIZES))

kernel_name = 'sb_dsa_rwkv7_convffn_step'


def rms_norm(x, g):
    xf = x.astype(jnp.float32)
    y = xf * lax.rsqrt(jnp.mean(xf * xf, axis=-1, keepdims=True) + NORM_EPS)
    return (y * g.astype(jnp.float32)).astype(x.dtype)


def rope(x, pos):
    half = x.shape[-1] // 2
    inv = ROPE_THETA ** (-2.0 * jnp.arange(half, dtype=jnp.float32) / x.shape[-1])
    ang = pos.astype(jnp.float32)[:, None] * inv[None, :]
    cos = jnp.cos(ang)[None, :, None, :]
    sin = jnp.sin(ang)[None, :, None, :]
    xf = x.astype(jnp.float32)
    x1, x2 = xf[..., :half], xf[..., half:]
    return jnp.concatenate([x1 * cos - x2 * sin, x2 * cos + x1 * sin], axis=-1).astype(x.dtype)


def gather_pages(cache, layer, page_table):
    g = cache[layer, page_table]
    return g.reshape((g.shape[0], g.shape[1] * g.shape[2]) + g.shape[3:])


def stick_breaking(q, k, v, q_pos, k_pos):
    z = jnp.einsum('bqhd,blhd->bhql', q, k).astype(jnp.float32) * (D_HEAD ** -0.5)
    mask = (k_pos[None, :] < q_pos[:, None])[None, None]
    log_beta = jax.nn.log_sigmoid(z)
    log_keep = jnp.where(mask, log_beta - z, 0.0)
    after = lax.cumsum(log_keep, axis=3, reverse=True) - log_keep
    w = jnp.where(mask, jnp.exp(log_beta + after), 0.0)
    return jnp.einsum('bhql,blhd->bqhd', w.astype(v.dtype), v)


def dsa_attention(q, k, v, q_idx, k_idx, w_idx, q_pos, k_pos, n_sel):
    B, Q = q.shape[0], q.shape[1]
    s = jnp.einsum('bqhd,bld->bqhl', q_idx, k_idx).astype(jnp.float32)
    score = jnp.einsum('bqh,bqhl->bql', w_idx.astype(jnp.float32), jax.nn.relu(s)) * (D_IDX ** -0.5 * H_IDX ** -0.5)
    causal = (k_pos[None, :] <= q_pos[:, None])[None]
    score = jnp.where(causal, score, -jnp.inf)
    _, sel = lax.top_k(score, n_sel)
    valid = sel <= q_pos[None, :, None]
    take = jax.vmap(lambda a, i: a[i])
    kg = take(k, sel)
    vg = take(v, sel)
    qg = q.reshape(B, Q, KV_DSA, H_DSA // KV_DSA, D_HEAD)
    logits = jnp.einsum('bqgrd,bqkgd->bqgrk', qg, kg).astype(jnp.float32) * (D_HEAD ** -0.5)
    logits = jnp.where(valid[:, :, None, None, :], logits, -jnp.inf)
    p = jax.nn.softmax(logits, axis=-1)
    o = jnp.einsum('bqgrk,bqkgd->bqgrd', p.astype(v.dtype), vg)
    return o.reshape(B, Q, H_DSA, D_HEAD)


def attn_mixer(h, w_in, w_out, past):
    Bt, T, _ = h.shape
    P = 0 if past is None else past[0].shape[1]
    L = P + T
    pos = P + jnp.arange(T, dtype=jnp.int32)
    proj = h @ w_in
    qa, ka, va, qb, kb, vb, qi, ki, wi = jnp.split(proj, SPLIT_POINTS, axis=-1)
    qa = qa.reshape(Bt, T, H_SB, D_HEAD)
    ka = ka.reshape(Bt, T, H_SB, D_HEAD)
    va = va.reshape(Bt, T, H_SB, D_HEAD)
    qb = rope(qb.reshape(Bt, T, H_DSA, D_HEAD), pos)
    kb = rope(kb.reshape(Bt, T, KV_DSA, D_HEAD), pos)
    vb = vb.reshape(Bt, T, KV_DSA, D_HEAD)
    qi = rope(qi.reshape(Bt, T, H_IDX, D_IDX), pos)
    ki = rope(ki[:, :, None, :], pos)[:, :, 0, :]
    new_rows = (ka, va, kb, vb, ki)
    if past is None:
        KA, VA, KB, VB, KI = new_rows
    else:
        KA, VA, KB, VB, KI = (jnp.concatenate([p_.astype(n_.dtype), n_], axis=1) for p_, n_ in zip(past, new_rows))
    k_pos = jnp.arange(L, dtype=jnp.int32)
    qb_len = min(Q_BLOCK, T)
    n_blocks = T // qb_len
    n_sel = min(DSA_TOPK, L // 4)

    def block(i):
        start = i * qb_len
        sl = lambda a: lax.dynamic_slice_in_dim(a, start, qb_len, axis=1)
        qp = lax.dynamic_slice_in_dim(pos, start, qb_len)
        oa = stick_breaking(sl(qa), KA, VA, qp, k_pos)
        ob = dsa_attention(sl(qb), KB, VB, sl(qi), KI, sl(wi), qp, k_pos, n_sel)
        return jnp.concatenate([oa.reshape(Bt, qb_len, H_SB * D_HEAD), ob.reshape(Bt, qb_len, H_DSA * D_HEAD)], axis=-1)

    o = lax.map(block, jnp.arange(n_blocks))
    o = jnp.transpose(o, (1, 0, 2, 3)).reshape(Bt, T, ATT_WIDTH)
    return o @ w_out, new_rows


def rwkv7_mixer(h, shift_prev, S0, mix, w_r, w_k, w_v, w_o, w0, w1, w2, a0, a1, a2, g1, g2, k_k, k_a, r_k, ln_w, ln_b):
    Bt, T, D = h.shape
    x_prev = jnp.concatenate([shift_prev[:, None, :].astype(h.dtype), h[:, :-1]], axis=1)
    xx = x_prev - h
    xr, xw, xk, xv, xa, xg = (h + xx * mix[j] for j in range(6))
    r = xr @ w_r
    k = xk @ w_k
    v = xv @ w_v
    w_log = -jax.nn.softplus(-(w0 + jnp.tanh(xw @ w1) @ w2).astype(jnp.float32)) - 0.5
    a = jax.nn.sigmoid((a0 + (xa @ a1) @ a2).astype(jnp.float32))
    g = jax.nn.sigmoid(xg @ g1) @ g2
    hs = lambda t: t.reshape(Bt, T, RK_H, RK_N)
    kk = hs((k * k_k).astype(jnp.float32))
    kk = kk / jnp.maximum(jnp.sqrt(jnp.sum(kk * kk, axis=-1, keepdims=True)), 1e-12)
    kf = k.astype(jnp.float32) * (1.0 + (a - 1.0) * k_a.astype(jnp.float32))
    decay = jnp.exp(-jnp.exp(w_log))
    rf = hs(r.astype(jnp.float32))
    kh = hs(kf)
    vf = hs(v.astype(jnp.float32))
    seqs = (rf, hs(decay), kh, vf, -kk, kk * hs(a))
    seqs = tuple(jnp.moveaxis(t, 1, 0) for t in seqs)

    def step(S, inp):
        r_t, d_t, k_t, v_t, av_t, bv_t = inp
        sa = jnp.einsum('bhij,bhj->bhi', S, av_t)
        S = S * d_t[:, :, None, :] + sa[..., None] * bv_t[:, :, None, :] + v_t[..., None] * k_t[:, :, None, :]
        return S, jnp.einsum('bhij,bhj->bhi', S, r_t)

    S_fin, y = lax.scan(step, S0.astype(jnp.float32), seqs)
    y = jnp.moveaxis(y, 0, 1)
    mu = jnp.mean(y, axis=-1, keepdims=True)
    var = jnp.mean(jnp.square(y - mu), axis=-1, keepdims=True)
    yn = ((y - mu) * lax.rsqrt(var + RK_GN_EPS)).reshape(Bt, T, D) * ln_w.astype(jnp.float32) + ln_b.astype(jnp.float32)
    bonus = jnp.sum(rf * kh * r_k.astype(jnp.float32), axis=-1, keepdims=True) * vf
    out = ((yn + bonus.reshape(Bt, T, D)) * g.astype(jnp.float32)).astype(h.dtype) @ w_o
    return out, h[:, -1], S_fin.astype(h.dtype)


def conv_ffn(h, w_up, conv_w, conv_b, w_down, prev):
    T = h.shape[1]
    u = h @ w_up
    full = jnp.concatenate([prev.astype(u.dtype), u], axis=1)
    c = conv_b + sum(conv_w[j] * full[:, j:j + T] for j in range(CONV_W))
    gate, val = jnp.split(c, 2, axis=-1)
    y = (jax.nn.gelu(gate, approximate=True) * val) @ w_down
    return y, full[:, full.shape[1] - (CONV_W - 1):]


def setup_inputs(seed: int = 0) -> dict:
    key = jax.random.key(seed)
    keys = jax.random.split(key, 64)
    counter = [0]

    def nk():
        kk_ = keys[counter[0]]
        counter[0] += 1
        return kk_

    def nrm(shape, scale):
        return jax.random.normal(nk(), shape, jnp.float32) * scale

    n_pages = PAST_LEN // PAGE_SIZE
    n_used = DEC_BATCH * n_pages
    n_pool = n_used + n_used // 4
    NA, NR, D = N_ATT_LAYERS, N_RWKV_LAYERS, D_MODEL
    inp = {}
    inp['x_prompt'] = nrm((BATCH, SEQ, D), 1.0)
    inp['x_sample'] = nrm((DEC_BATCH, DEC_SEQ, D), 1.0)
    inp['cache_sb_k'] = nrm((NA, n_pool, PAGE_SIZE, H_SB, D_HEAD), 1.0)
    inp['cache_sb_v'] = nrm((NA, n_pool, PAGE_SIZE, H_SB, D_HEAD), 1.0)
    inp['cache_dsa_k'] = nrm((NA, n_pool, PAGE_SIZE, KV_DSA, D_HEAD), 1.0)
    inp['cache_dsa_v'] = nrm((NA, n_pool, PAGE_SIZE, KV_DSA, D_HEAD), 1.0)
    inp['cache_idx_k'] = nrm((NA, n_pool, PAGE_SIZE, D_IDX), 1.0)
    inp['page_table'] = jax.random.permutation(nk(), n_pool)[:n_used].reshape(DEC_BATCH, n_pages).astype(jnp.int32)
    inp['state_wkv'] = nrm((NR, DEC_BATCH, RK_H, RK_N, RK_N), 0.3)
    inp['state_shift'] = nrm((NR, DEC_BATCH, D), 1.0)
    inp['state_ffn_conv'] = nrm((DEPTH, DEC_BATCH, CONV_W - 1, 2 * D_FF), 1.0)
    inp['norm_mix_pre'] = 1.0 + nrm((DEPTH, D), 0.05)
    inp['norm_mix_post'] = 1.0 + nrm((DEPTH, D), 0.05)
    inp['norm_ffn_pre'] = 1.0 + nrm((DEPTH, D), 0.05)
    inp['norm_ffn_post'] = 1.0 + nrm((DEPTH, D), 0.05)
    inp['att_w_in'] = nrm((NA, D, ATT_IN), D ** -0.5)
    inp['att_w_out'] = nrm((NA, ATT_WIDTH, D), ATT_WIDTH ** -0.5)
    inp['rk_mix'] = jax.random.uniform(nk(), (NR, 6, D), jnp.float32)
    inp['rk_w_r'] = nrm((NR, D, D), D ** -0.5)
    inp['rk_w_k'] = nrm((NR, D, D), D ** -0.5)
    inp['rk_w_v'] = nrm((NR, D, D), D ** -0.5)
    inp['rk_w_o'] = nrm((NR, D, D), D ** -0.5)
    inp['rk_w0'] = jax.random.uniform(nk(), (NR, D), jnp.float32, minval=-6.0, maxval=1.0)
    inp['rk_w1'] = nrm((NR, D, RK_DECAY_LORA), D ** -0.5)
    inp['rk_w2'] = nrm((NR, RK_DECAY_LORA, D), 0.5 * RK_DECAY_LORA ** -0.5)
    inp['rk_a0'] = nrm((NR, D), 0.1)
    inp['rk_a1'] = nrm((NR, D, RK_AAA_LORA), D ** -0.5)
    inp['rk_a2'] = nrm((NR, RK_AAA_LORA, D), 0.5 * RK_AAA_LORA ** -0.5)
    inp['rk_g1'] = nrm((NR, D, RK_GATE_LORA), D ** -0.5)
    inp['rk_g2'] = nrm((NR, RK_GATE_LORA, D), RK_GATE_LORA ** -0.5)
    inp['rk_k_k'] = 0.85 + nrm((NR, D), 0.05)
    inp['rk_k_a'] = 1.0 + nrm((NR, D), 0.05)
    inp['rk_r_k'] = nrm((NR, RK_H, RK_N), 0.1)
    inp['rk_ln_w'] = 1.0 + nrm((NR, D), 0.05)
    inp['rk_ln_b'] = nrm((NR, D), 0.01)
    inp['ffn_w_up'] = nrm((DEPTH, D, 2 * D_FF), D ** -0.5)
    inp['ffn_conv_w'] = nrm((DEPTH, CONV_W, 2 * D_FF), CONV_W ** -0.5)
    inp['ffn_conv_b'] = nrm((DEPTH, 2 * D_FF), 0.01)
    inp['ffn_w_down'] = nrm((DEPTH, D_FF, D), D_FF ** -0.5)
    return inp


def reference(x_prompt, x_sample, cache_sb_k, cache_sb_v, cache_dsa_k, cache_dsa_v, cache_idx_k, page_table,
              state_wkv, state_shift, state_ffn_conv,
              norm_mix_pre, norm_mix_post, norm_ffn_pre, norm_ffn_post,
              att_w_in, att_w_out,
              rk_mix, rk_w_r, rk_w_k, rk_w_v, rk_w_o, rk_w0, rk_w1, rk_w2, rk_a0, rk_a1, rk_a2,
              rk_g1, rk_g2, rk_k_k, rk_k_a, rk_r_k, rk_ln_w, rk_ln_b,
              ffn_w_up, ffn_conv_w, ffn_conv_b, ffn_w_down):
    xp, xs = x_prompt, x_sample
    Bp = xp.shape[0]
    att_p, att_s, wkv_p, wkv_s, sh_p, sh_s, cv_p, cv_s = [], [], [], [], [], [], [], []
    for i in range(DEPTH):
        li = i // 2
        hp = rms_norm(xp, norm_mix_pre[i])
        hs = rms_norm(xs, norm_mix_pre[i])
        if i % 2 == 0:
            past = tuple(gather_pages(c, li, page_table) for c in (cache_sb_k, cache_sb_v, cache_dsa_k, cache_dsa_v, cache_idx_k))
            op, rows_p = attn_mixer(hp, att_w_in[li], att_w_out[li], None)
            os_, rows_s = attn_mixer(hs, att_w_in[li], att_w_out[li], past)
            att_p.append(rows_p)
            att_s.append(rows_s)
        else:
            prm = (rk_mix[li], rk_w_r[li], rk_w_k[li], rk_w_v[li], rk_w_o[li], rk_w0[li], rk_w1[li], rk_w2[li],
                   rk_a0[li], rk_a1[li], rk_a2[li], rk_g1[li], rk_g2[li], rk_k_k[li], rk_k_a[li], rk_r_k[li],
                   rk_ln_w[li], rk_ln_b[li])
            zero_shift = jnp.zeros((Bp, D_MODEL), xp.dtype)
            zero_wkv = jnp.zeros((Bp, RK_H, RK_N, RK_N), jnp.float32)
            op, shp, Sp = rwkv7_mixer(hp, zero_shift, zero_wkv, *prm)
            os_, shs, Ss = rwkv7_mixer(hs, state_shift[li], state_wkv[li], *prm)
            wkv_p.append(Sp)
            wkv_s.append(Ss)
            sh_p.append(shp)
            sh_s.append(shs)
        xp = xp + rms_norm(op, norm_mix_post[i])
        xs = xs + rms_norm(os_, norm_mix_post[i])
        hp = rms_norm(xp, norm_ffn_pre[i])
        hs = rms_norm(xs, norm_ffn_pre[i])
        fp, cp = conv_ffn(hp, ffn_w_up[i], ffn_conv_w[i], ffn_conv_b[i], ffn_w_down[i],
                          jnp.zeros((Bp, CONV_W - 1, 2 * D_FF), xp.dtype))
        fs, cs = conv_ffn(hs, ffn_w_up[i], ffn_conv_w[i], ffn_conv_b[i], ffn_w_down[i], state_ffn_conv[i])
        cv_p.append(cp)
        cv_s.append(cs)
        xp = xp + rms_norm(fp, norm_ffn_post[i])
        xs = xs + rms_norm(fs, norm_ffn_post[i])
    st = lambda rows, j: jnp.stack([r[j] for r in rows])
    return (xp, xs,
            st(att_p, 0), st(att_s, 0), st(att_p, 1), st(att_s, 1),
            st(att_p, 2), st(att_s, 2), st(att_p, 3), st(att_s, 3),
            st(att_p, 4), st(att_s, 4),
            jnp.stack(wkv_p), jnp.stack(wkv_s), jnp.stack(sh_p), jnp.stack(sh_s),
            jnp.stack(cv_p), jnp.stack(cv_s))
```

```python
import functools
import math

import jax
import jax.numpy as jnp
from jax import lax
from jax.experimental import pallas as pl
from jax.experimental.pallas import tpu as pltpu

F32 = jnp.float32
BF16 = jnp.bfloat16
I32 = jnp.int32

D_MODEL = 1024
D_HEAD = 64
H_SB = 8
H_DSA = 8
KV_DSA = 2
H_IDX = 8
D_IDX = 64
DSA_TOPK = 256
PAGE = 128
ROPE_THETA = 10000.0
RK_N = 64
RK_H = D_MODEL // RK_N
RK_GN_EPS = 64e-5
D_FF = 2816
NORM_EPS = 1e-6

LANES = 128
VMEM_LIMIT = 56 * 1024 * 1024
INT_MIN = -2147483648
INT_MAX = 2147483647
NEG_BIG = -1e30

_C_QA, _C_KA, _C_VA, _C_QB, _C_KB, _C_VB, _C_QI, _C_KI, _C_WI, _C_END = (
    0, 512, 1024, 1536, 2048, 2176, 2304, 2816, 2944, 3072)


def _cparams(*sem):
    return pltpu.CompilerParams(dimension_semantics=sem, vmem_limit_bytes=VMEM_LIMIT)


def _dot(a, b):
    return jnp.dot(a, b, preferred_element_type=F32)


def _dot_nt(a, b):
    return lax.dot_general(a, b, (((1,), (1,)), ((), ())), preferred_element_type=F32)


def _dot_tn(a, b):
    return lax.dot_general(a, b, (((0,), (0,)), ((), ())), preferred_element_type=F32)


def _split(x):
    hi = x.astype(BF16)
    lo = (x - hi.astype(F32)).astype(BF16)
    return hi, lo


def _dot_x2(x, m_bf16):
    hi, lo = _split(x)
    return _dot(hi, m_bf16) + _dot(lo, m_bf16)


def _rms(x, g):
    return x * lax.rsqrt(jnp.mean(x * x, axis=-1, keepdims=True) + NORM_EPS) * g


def _seg_ones():
    r = lax.broadcasted_iota(I32, (LANES, LANES), 0)
    c = lax.broadcasted_iota(I32, (LANES, LANES), 1)
    return ((r < 64) == (c < 64)).astype(BF16)


def _seg64_sum(x, bd):
    n = x.shape[1] // LANES
    return jnp.concatenate([_dot_x2(x[:, LANES * m:LANES * (m + 1)], bd) for m in range(n)], axis=1)


def _attn_proj_kernel(x_ref, g_ref, w_ref, cos_ref, sin_ref,
                      qa_ref, ka_ref, va_ref, qb_ref, kb_ref, vb_ref, qi_ref, ki_ref, wi_ref):
    h = _rms(x_ref[...], g_ref[...]).astype(BF16)
    cos = cos_ref[...]
    sin = sin_ref[...]
    lane = lax.broadcasted_iota(I32, (1, LANES), 1)
    first = (lane & 32) == 0

    def proj(c0, c1):
        return _dot(h, w_ref[:, c0:c1])

    def rope(blk):
        rot = jnp.where(first, pltpu.roll(blk, LANES - 32, 1), pltpu.roll(blk, 32, 1))
        return blk * cos + rot * sin

    qa_ref[...] = proj(_C_QA, _C_KA)
    ka_ref[...] = proj(_C_KA, _C_VA)
    va_ref[...] = proj(_C_VA, _C_QB)
    for m in range(4):
        qb_ref[:, LANES * m:LANES * (m + 1)] = rope(proj(_C_QB + LANES * m, _C_QB + LANES * (m + 1)))
        qi_ref[:, LANES * m:LANES * (m + 1)] = rope(proj(_C_QI + LANES * m, _C_QI + LANES * (m + 1)))
    kb_ref[...] = rope(proj(_C_KB, _C_VB))
    vb_ref[...] = proj(_C_VB, _C_QI)
    ki_ref[...] = rope(proj(_C_KI, _C_WI))
    wi_ref[...] = proj(_C_WI, _C_END)


def _attn_proj(x, g, w, cos, sin, tm):
    M = x.shape[0]
    nt = cos.shape[0] // tm
    widths = (512, 512, 512, 512, 128, 128, 512, 128, 128)
    row = lambda i: (i, 0)
    return pl.pallas_call(
        _attn_proj_kernel,
        grid=(M // tm,),
        in_specs=[pl.BlockSpec((tm, D_MODEL), row),
                  pl.BlockSpec((1, D_MODEL), lambda i: (0, 0)),
                  pl.BlockSpec((D_MODEL, _C_END), lambda i: (0, 0)),
                  pl.BlockSpec((tm, LANES), lambda i: (i % nt, 0)),
                  pl.BlockSpec((tm, LANES), lambda i: (i % nt, 0))],
        out_specs=[pl.BlockSpec((tm, wd), row) for wd in widths],
        out_shape=[jax.ShapeDtypeStruct((M, wd), F32) for wd in widths],
        compiler_params=_cparams("parallel"),
        name="attn_proj",
    )(x, g, w, cos, sin)


def _rope_tables(pos):
    half = D_HEAD // 2
    inv = ROPE_THETA ** (-2.0 * jnp.arange(half, dtype=F32) / D_HEAD)
    ang = pos.astype(F32)[:, None] * inv[None, :]
    cos = jnp.cos(ang)
    sin = jnp.sin(ang)
    return jnp.tile(cos, (1, 4)), jnp.tile(jnp.concatenate([-sin, sin], axis=1), (1, 2))


def _pack_w_in(w_in):
    ki = w_in[:, 2816:2880]
    wi = w_in[:, 2880:2888]
    pad = jnp.zeros((w_in.shape[0], LANES - H_IDX), w_in.dtype)
    return jnp.concatenate([w_in[:, :2816], ki, ki, wi, pad], axis=1).astype(BF16)


def _sb_block(z, c, tri, vb, causal):
    t = jnp.log(1.0 + jnp.exp(-jnp.abs(z)))
    lb = jnp.minimum(z, 0.0) - t
    lk = jnp.minimum(-z, 0.0) - t
    if causal is not None:
        lk = jnp.where(causal, lk, 0.0)
    aft = _dot_x2(lk, tri) + c
    w = jnp.exp(lb + aft)
    if causal is not None:
        w = jnp.where(causal, w, 0.0)
    pv = _dot(w.astype(BF16), vb)
    return c + jnp.sum(lk, axis=1, keepdims=True), pv


def _sb_prompt_kernel(q_ref, k_ref, v_ref, o_ref, *, tq):
    i = pl.program_id(2)
    lane = lax.broadcasted_iota(I32, (1, LANES), 1)
    row = lax.broadcasted_iota(I32, (tq, tq), 0)
    col = lax.broadcasted_iota(I32, (tq, tq), 1)
    tri = (row > col).astype(BF16)
    causal = col < row
    q = q_ref[...] * (D_HEAD ** -0.5)
    outs = []
    for h2 in range(2):
        hm = (lane < 64) if h2 == 0 else (lane >= 64)
        qh = jnp.where(hm, q, 0.0).astype(BF16)

        def step(j, carry, mask):
            c, acc = carry
            off = pl.multiple_of(j * tq, tq)
            kb = k_ref[pl.ds(off, tq), :].astype(BF16)
            vb = v_ref[pl.ds(off, tq), :].astype(BF16)
            c, pv = _sb_block(_dot_nt(qh, kb), c, tri, vb, mask)
            return c, acc + pv

        carry = step(i, (jnp.zeros((tq, 1), F32), jnp.zeros((tq, LANES), F32)), causal)
        carry = lax.fori_loop(0, i, lambda jj, cr: step(i - 1 - jj, cr, None), carry)
        outs.append(carry[1])
    o_ref[...] = jnp.where(lane < 64, outs[0], outs[1])


def _sb_prompt(q, k, v, B, T, tq):
    nq = T // tq
    return pl.pallas_call(
        functools.partial(_sb_prompt_kernel, tq=tq),
        grid=(B, 4, nq),
        in_specs=[pl.BlockSpec((tq, LANES), lambda b, p, i: (b * nq + i, p)),
                  pl.BlockSpec((T, LANES), lambda b, p, i: (b, p)),
                  pl.BlockSpec((T, LANES), lambda b, p, i: (b, p))],
        out_specs=pl.BlockSpec((tq, LANES), lambda b, p, i: (b * nq + i, p)),
        out_shape=jax.ShapeDtypeStruct(q.shape, F32),
        compiler_params=_cparams("parallel", "parallel", "arbitrary"),
        name="sb_prompt",
    )(q, k, v)


SROWS = 16


def _sb_sample_kernel(pt_ref, q_ref, kn_ref, vn_ref, kc_ref, vc_ref, o_ref, c_scr, acc_scr):
    j = pl.program_id(1)
    row = lax.broadcasted_iota(I32, (PAGE, PAGE), 0)
    col = lax.broadcasted_iota(I32, (PAGE, PAGE), 1)
    tri = (row > col).astype(BF16)
    qrow = lax.broadcasted_iota(I32, (SROWS, PAGE), 0)
    kcol = lax.broadcasted_iota(I32, (SROWS, PAGE), 1)
    causal = kcol < qrow
    scale = D_HEAD ** -0.5

    def qhead(h):
        return (q_ref[0, :, D_HEAD * h:D_HEAD * (h + 1)] * scale).astype(BF16)

    @pl.when(j == 0)
    def _():
        for h in range(H_SB):
            kb = kn_ref[0, :, D_HEAD * h:D_HEAD * (h + 1)].astype(BF16)
            vb = vn_ref[0, :, D_HEAD * h:D_HEAD * (h + 1)].astype(BF16)
            c, pv = _sb_block(_dot_nt(qhead(h), kb), jnp.zeros((SROWS, 1), F32), tri, vb, causal)
            c_scr[h] = jnp.broadcast_to(c, (SROWS, LANES))
            acc_scr[h] = pv

    for h in range(H_SB):
        kb = kc_ref[pl.ds(h, PAGE, stride=H_SB), :].astype(BF16)
        vb = vc_ref[pl.ds(h, PAGE, stride=H_SB), :].astype(BF16)
        c, pv = _sb_block(_dot_nt(qhead(h), kb), c_scr[h][:, 0:1], tri, vb, None)
        c_scr[h] = jnp.broadcast_to(c, (SROWS, LANES))
        acc_scr[h] = acc_scr[h] + pv

    @pl.when(j == pl.num_programs(1) - 1)
    def _():
        for h in range(H_SB):
            o_ref[0, :, D_HEAD * h:D_HEAD * (h + 1)] = acc_scr[h]


def _sb_sample(pt, q, kn, vn, kc, vc):
    B, NP = pt.shape
    grid_spec = pltpu.PrefetchScalarGridSpec(
        num_scalar_prefetch=1,
        grid=(B, NP),
        in_specs=[pl.BlockSpec((1, SROWS, 512), lambda b, j, pt: (b, 0, 0)),
                  pl.BlockSpec((1, PAGE, 512), lambda b, j, pt: (b, 0, 0)),
                  pl.BlockSpec((1, PAGE, 512), lambda b, j, pt: (b, 0, 0)),
                  pl.BlockSpec((None, PAGE * H_SB, D_HEAD), lambda b, j, pt: (pt[b, NP - 1 - j], 0, 0)),
                  pl.BlockSpec((None, PAGE * H_SB, D_HEAD), lambda b, j, pt: (pt[b, NP - 1 - j], 0, 0))],
        out_specs=pl.BlockSpec((1, SROWS, 512), lambda b, j, pt: (b, 0, 0)),
        scratch_shapes=[pltpu.VMEM((H_SB, SROWS, LANES), F32), pltpu.VMEM((H_SB, SROWS, D_HEAD), F32)])
    return pl.pallas_call(
        _sb_sample_kernel,
        grid_spec=grid_spec,
        out_shape=jax.ShapeDtypeStruct((B, SROWS, 512), F32),
        compiler_params=_cparams("parallel", "arbitrary"),
        name="sb_sample",
    )(pt, q, kn, vn, kc, vc)


_IDX_SCALE = D_IDX ** -0.5 * H_IDX ** -0.5


def _score_key(score):
    score = jnp.where(score == 0.0, 0.0, score)
    bits = pltpu.bitcast(score, I32)
    return bits ^ ((bits >> 31) & INT_MAX)


def _topk_threshold(count_ge, count_eq_lt, rows, n_sel, idx_bits, jl_ref):
    def bit_body(it, tu):
        cand = tu | jnp.left_shift(jnp.int32(1), 31 - it)
        return jnp.where(count_ge(cand ^ INT_MIN) >= n_sel, cand, tu)

    tu = lax.fori_loop(0, 32, bit_body, jnp.zeros((rows, 1), I32))
    thr = tu ^ INT_MIN
    n_ge = count_ge(thr)
    n_gt = jnp.where(thr == INT_MAX, 0, count_ge(thr + 1))
    need = n_sel - n_gt
    tied = n_ge > n_sel
    jl_ref[...] = jnp.full((rows, 1), INT_MAX, I32)

    @pl.when(jnp.max(tied.astype(I32)) > 0)
    def _():
        def idx_body(it, lo):
            cand = lo | jnp.left_shift(jnp.int32(1), idx_bits - 1 - it)
            return jnp.where(count_eq_lt(thr, cand) < need, cand, lo)
        lo = lax.fori_loop(0, idx_bits, idx_body, jnp.zeros((rows, 1), I32))
        jl_ref[...] = jnp.where(tied, lo, INT_MAX)

    return thr


def _dsa_prompt_kernel(qi_ref, wi_ref, ki_ref, qb_ref, kb_ref, vb_ref, o_ref, key_scr, jl_scr,
                       *, tq, n_sel, idx_bits):
    i = pl.program_id(1)
    tk = tq
    lane = lax.broadcasted_iota(I32, (1, LANES), 1)
    lo_half = lane < 64
    row = lax.broadcasted_iota(I32, (tq, tk), 0)
    col = lax.broadcasted_iota(I32, (tq, tk), 1)
    diag_ok = col <= row

    qs = []
    for h in range(H_IDX):
        blk = qi_ref[:, LANES * (h // 2):LANES * (h // 2 + 1)]
        qs.append(jnp.where(lo_half if h % 2 == 0 else ~lo_half, blk, 0.0))
    qstack = jnp.concatenate(qs, axis=0).astype(BF16)
    wb = [jnp.broadcast_to(wi_ref[:, h:h + 1], (tq, tk)) for h in range(H_IDX)]

    def score_chunk(c, mask):
        off = pl.multiple_of(c * tk, tk)
        kc = ki_ref[pl.ds(off, tk), :].astype(BF16)
        s = _dot_nt(qstack, kc)
        score = wb[0] * jnp.maximum(s[0:tq], 0.0)
        for h in range(1, H_IDX):
            score = score + wb[h] * jnp.maximum(s[h * tq:(h + 1) * tq], 0.0)
        key = _score_key(score * _IDX_SCALE)
        if mask is not None:
            key = jnp.where(mask, key, INT_MIN)
        key_scr[c] = key

    def score_body(c, carry):
        score_chunk(c, None)
        return carry

    lax.fori_loop(0, i, score_body, 0)
    score_chunk(i, diag_ok)

    def count_ge(cand):
        def body(c, acc):
            return acc + (key_scr[c] >= cand).astype(I32)
        acc = lax.fori_loop(0, i + 1, body, jnp.zeros((tq, tk), I32))
        return jnp.sum(acc, axis=1, keepdims=True)

    def count_eq_lt(thr, jcand):
        def body(c, acc):
            return acc + ((key_scr[c] == thr) & (col + c * tk < jcand)).astype(I32)
        acc = lax.fori_loop(0, i + 1, body, jnp.zeros((tq, tk), I32))
        return jnp.sum(acc, axis=1, keepdims=True)

    thr = _topk_threshold(count_ge, count_eq_lt, tq, n_sel, idx_bits, jl_scr)
    jl = jl_scr[...]

    qs = []
    for h in range(H_DSA):
        blk = qb_ref[:, LANES * (h // 2):LANES * (h // 2 + 1)] * (D_HEAD ** -0.5)
        g = h // (H_DSA // KV_DSA)
        if (h % 2) != g:
            blk = pltpu.roll(blk, 64, 1)
        qs.append(jnp.where(lo_half if g == 0 else ~lo_half, blk, 0.0))
    qstack2 = jnp.concatenate(qs, axis=0).astype(BF16)

    def att_chunk(c, carry, causal):
        m, l, acc = carry
        off = pl.multiple_of(c * tk, tk)
        kblk = key_scr[c]
        sel = (kblk > thr) | ((kblk == thr) & (col + off <= jl))
        if causal is not None:
            sel = sel & causal
        z = _dot_nt(qstack2, kb_ref[pl.ds(off, tk), :].astype(BF16))
        z = jnp.where(sel[None], z.reshape(H_DSA, tq, tk), NEG_BIG).reshape(H_DSA * tq, tk)
        m_new = jnp.maximum(m, jnp.max(z, axis=1, keepdims=True))
        alpha = jnp.exp(m - m_new)
        p = jnp.exp(z - m_new)
        l = alpha * l + jnp.sum(p, axis=1, keepdims=True)
        acc = alpha * acc + _dot(p.astype(BF16), vb_ref[pl.ds(off, tk), :].astype(BF16))
        return m_new, l, acc

    carry = (jnp.full((H_DSA * tq, 1), NEG_BIG, F32), jnp.zeros((H_DSA * tq, 1), F32),
             jnp.zeros((H_DSA * tq, LANES), F32))
    carry = lax.fori_loop(0, i, lambda c, cr: att_chunk(c, cr, None), carry)
    m, l, acc = att_chunk(i, carry, diag_ok)
    out = acc / l
    for mblk in range(4):
        parts = []
        for h in (2 * mblk, 2 * mblk + 1):
            o_h = out[h * tq:(h + 1) * tq]
            if (h % 2) != h // (H_DSA // KV_DSA):
                o_h = pltpu.roll(o_h, 64, 1)
            parts.append(o_h)
        o_ref[:, LANES * mblk:LANES * (mblk + 1)] = jnp.where(lo_half, parts[0], parts[1])


def _dsa_prompt(qi, wi, ki2, qb, kb, vb, B, T, tq, n_sel):
    nq = T // tq
    qrow = lambda b, i: (b * nq + i, 0)
    full = lambda b, i: (b, 0)
    return pl.pallas_call(
        functools.partial(_dsa_prompt_kernel, tq=tq, n_sel=n_sel, idx_bits=max(1, (T - 1).bit_length())),
        grid=(B, nq),
        in_specs=[pl.BlockSpec((tq, 512), qrow), pl.BlockSpec((tq, LANES), qrow),
                  pl.BlockSpec((T, LANES), full), pl.BlockSpec((tq, 512), qrow),
                  pl.BlockSpec((T, LANES), full), pl.BlockSpec((T, LANES), full)],
        out_specs=pl.BlockSpec((tq, 512), qrow),
        out_shape=jax.ShapeDtypeStruct(qb.shape, F32),
        scratch_shapes=[pltpu.VMEM((T // tq, tq, tq), I32), pltpu.VMEM((tq, 1), I32)],
        compiler_params=_cparams("parallel", "arbitrary"),
        name="dsa_prompt",
    )(qi, wi, ki2, qb, kb, vb)


def _dsa_score_sample_kernel(pt_ref, q_ref, w_ref, kn_ref, kc_ref, key_ref, thr_ref, jl_ref,
                             *, n_pages, n_sel, idx_bits):
    j = pl.program_id(1)
    qrow = lax.broadcasted_iota(I32, (SROWS, PAGE), 0)
    kcol = lax.broadcasted_iota(I32, (SROWS, PAGE), 1)
    q = q_ref[0].astype(BF16)
    w = w_ref[0]

    def score(kpage, mask, c):
        s = jnp.maximum(_dot_nt(q, kpage.astype(BF16)), 0.0) * w
        sc = s[0:SROWS]
        for h in range(1, H_IDX):
            sc = sc + s[h * SROWS:(h + 1) * SROWS]
        key = _score_key(sc * _IDX_SCALE)
        if mask is not None:
            key = jnp.where(mask, key, INT_MIN)
        key_ref[0, c] = key

    @pl.when(j < n_pages)
    def _():
        score(kc_ref[...], None, j)

    @pl.when(j == n_pages)
    def _():
        score(kn_ref[0], kcol <= qrow, n_pages)
        nchunk = n_pages + 1

        def count_ge(cand):
            def body(c, acc):
                return acc + (key_ref[0, c] >= cand).astype(I32)
            acc = lax.fori_loop(0, nchunk, body, jnp.zeros((SROWS, PAGE), I32))
            return jnp.sum(acc, axis=1, keepdims=True)

        def count_eq_lt(thr, jcand):
            def body(c, acc):
                return acc + ((key_ref[0, c] == thr) & (kcol + c * PAGE < jcand)).astype(I32)
            acc = lax.fori_loop(0, nchunk, body, jnp.zeros((SROWS, PAGE), I32))
            return jnp.sum(acc, axis=1, keepdims=True)

        thr_ref[0] = _topk_threshold(count_ge, count_eq_lt, SROWS, n_sel, idx_bits, jl_ref.at[0])


def _dsa_score_sample(pt, q, w, kn, kc, n_sel):
    B, NP = pt.shape
    L = (NP + 1) * PAGE
    grid_spec = pltpu.PrefetchScalarGridSpec(
        num_scalar_prefetch=1,
        grid=(B, NP + 1),
        in_specs=[pl.BlockSpec((1, H_IDX * SROWS, D_IDX), lambda b, j, pt: (b, 0, 0)),
                  pl.BlockSpec((1, H_IDX * SROWS, 1), lambda b, j, pt: (b, 0, 0)),
                  pl.BlockSpec((1, PAGE, D_IDX), lambda b, j, pt: (b, 0, 0)),
                  pl.BlockSpec((None, PAGE, D_IDX), lambda b, j, pt: (pt[b, jnp.minimum(j, NP - 1)], 0, 0))],
        out_specs=[pl.BlockSpec((1, NP + 1, SROWS, PAGE), lambda b, j, pt: (b, 0, 0, 0)),
                   pl.BlockSpec((1, SROWS, 1), lambda b, j, pt: (b, 0, 0)),
                   pl.BlockSpec((1, SROWS, 1), lambda b, j, pt: (b, 0, 0))])
    return pl.pallas_call(
        functools.partial(_dsa_score_sample_kernel, n_pages=NP, n_sel=n_sel,
                          idx_bits=max(1, (L - 1).bit_length())),
        grid_spec=grid_spec,
        out_shape=[jax.ShapeDtypeStruct((B, NP + 1, SROWS, PAGE), I32),
                   jax.ShapeDtypeStruct((B, SROWS, 1), I32),
                   jax.ShapeDtypeStruct((B, SROWS, 1), I32)],
        compiler_params=_cparams("parallel", "arbitrary"),
        name="dsa_score_sample",
    )(pt, q, w, kn, kc)


_GR = (H_DSA // KV_DSA) * SROWS


def _dsa_att_sample_kernel(pt_ref, q_ref, key_ref, thr_ref, jl_ref, kn_ref, vn_ref, kc_ref, vc_ref, o_ref,
                           m_scr, l_scr, acc_scr, *, n_pages):
    j = pl.program_id(1)
    qrow = lax.broadcasted_iota(I32, (SROWS, PAGE), 0)
    kcol = lax.broadcasted_iota(I32, (SROWS, PAGE), 1)
    nrep = H_DSA // KV_DSA

    @pl.when(j == 0)
    def _():
        m_scr[...] = jnp.full(m_scr.shape, NEG_BIG, F32)
        l_scr[...] = jnp.zeros(l_scr.shape, F32)
        acc_scr[...] = jnp.zeros(acc_scr.shape, F32)

    def attend(kg, vg, g, sel):
        q = (q_ref[0, g] * (D_HEAD ** -0.5)).astype(BF16)
        z = _dot_nt(q, kg.astype(BF16))
        z = jnp.where(sel[None], z.reshape(nrep, SROWS, PAGE), NEG_BIG).reshape(_GR, PAGE)
        m = m_scr[g][:, 0:1]
        l = l_scr[g][:, 0:1]
        m_new = jnp.maximum(m, jnp.max(z, axis=1, keepdims=True))
        alpha = jnp.exp(m - m_new)
        p = jnp.exp(z - m_new)
        l = alpha * l + jnp.sum(p, axis=1, keepdims=True)
        acc_scr[g] = alpha * acc_scr[g] + _dot(p.astype(BF16), vg.astype(BF16))
        m_scr[g] = jnp.broadcast_to(m_new, (_GR, LANES))
        l_scr[g] = jnp.broadcast_to(l, (_GR, LANES))

    thr = thr_ref[0]
    jl = jl_ref[0]

    @pl.when(j < n_pages)
    def _():
        kblk = key_ref[0, j]
        sel = (kblk > thr) | ((kblk == thr) & (kcol + j * PAGE <= jl))
        for g in range(KV_DSA):
            attend(kc_ref[pl.ds(g, PAGE, stride=KV_DSA), :], vc_ref[pl.ds(g, PAGE, stride=KV_DSA), :], g, sel)

    @pl.when(j == n_pages)
    def _():
        off = n_pages * PAGE
        kblk = key_ref[0, n_pages]
        sel = ((kblk > thr) | ((kblk == thr) & (kcol + off <= jl))) & (kcol <= qrow)
        for g in range(KV_DSA):
            attend(kn_ref[0, :, D_HEAD * g:D_HEAD * (g + 1)], vn_ref[0, :, D_HEAD * g:D_HEAD * (g + 1)], g, sel)
        for g in range(KV_DSA):
            o_ref[0, g] = acc_scr[g] / l_scr[g][:, 0:1]


def _dsa_att_sample(pt, q, keys, thr, jl, kn, vn, kc, vc):
    B, NP = pt.shape
    page = lambda b, j, pt: (pt[b, jnp.minimum(j, NP - 1)], 0, 0)
    per_b3 = lambda b, j, pt: (b, 0, 0)
    grid_spec = pltpu.PrefetchScalarGridSpec(
        num_scalar_prefetch=1,
        grid=(B, NP + 1),
        in_specs=[pl.BlockSpec((1, KV_DSA, _GR, D_HEAD), lambda b, j, pt: (b, 0, 0, 0)),
                  pl.BlockSpec((1, NP + 1, SROWS, PAGE), lambda b, j, pt: (b, 0, 0, 0)),
                  pl.BlockSpec((1, SROWS, 1), per_b3),
                  pl.BlockSpec((1, SROWS, 1), per_b3),
                  pl.BlockSpec((1, PAGE, LANES), per_b3),
                  pl.BlockSpec((1, PAGE, LANES), per_b3),
                  pl.BlockSpec((None, PAGE * KV_DSA, D_HEAD), page),
                  pl.BlockSpec((None, PAGE * KV_DSA, D_HEAD), page)],
        out_specs=pl.BlockSpec((1, KV_DSA, _GR, D_HEAD), lambda b, j, pt: (b, 0, 0, 0)),
        scratch_shapes=[pltpu.VMEM((KV_DSA, _GR, LANES), F32), pltpu.VMEM((KV_DSA, _GR, LANES), F32),
                        pltpu.VMEM((KV_DSA, _GR, D_HEAD), F32)])
    return pl.pallas_call(
        functools.partial(_dsa_att_sample_kernel, n_pages=NP),
        grid_spec=grid_spec,
        out_shape=jax.ShapeDtypeStruct((B, KV_DSA, _GR, D_HEAD), F32),
        compiler_params=_cparams("parallel", "arbitrary"),
        name="dsa_att_sample",
    )(pt, q, keys, thr, jl, kn, vn, kc, vc)


def _mm_norm_res_kernel(*refs, n):
    a_refs, w_refs = refs[:n], refs[n:2 * n]
    g_ref, x_ref, o_ref = refs[2 * n:]
    acc = _dot(a_refs[0][...].astype(BF16), w_refs[0][...])
    for a_ref, w_ref in zip(a_refs[1:], w_refs[1:]):
        acc = acc + _dot(a_ref[...].astype(BF16), w_ref[...])
    o_ref[...] = x_ref[...] + _rms(acc, g_ref[...])


def _mm_norm_res(a_list, w_list, g, x, tm, name):
    M = x.shape[0]
    n = len(a_list)
    row = lambda i: (i, 0)
    const = lambda i: (0, 0)
    return pl.pallas_call(
        functools.partial(_mm_norm_res_kernel, n=n),
        grid=(M // tm,),
        in_specs=([pl.BlockSpec((tm, a.shape[1]), row) for a in a_list]
                  + [pl.BlockSpec(w.shape, const) for w in w_list]
                  + [pl.BlockSpec((1, D_MODEL), const), pl.BlockSpec((tm, D_MODEL), row)]),
        out_specs=pl.BlockSpec((tm, D_MODEL), row),
        out_shape=jax.ShapeDtypeStruct((M, D_MODEL), F32),
        compiler_params=_cparams("parallel"),
        name=name,
    )(*a_list, *w_list, g, x)


_GELU_C = math.sqrt(2.0 / math.pi)


def _gelu_tanh(x):
    return x * (0.5 * (1.0 + jnp.tanh(_GELU_C * (x + 0.044715 * (x * x * x)))))


def _ffn_up_kernel(x_ref, g_ref, wg_ref, wv_ref, cwg_ref, cwv_ref, cbg_ref, cbv_ref, pg_ref, pv_ref,
                   act_ref, og_ref, ov_ref, eg_scr, ev_scr, *, tm, T, streamed):
    i = pl.program_id(1)
    h = _rms(x_ref[...], g_ref[...]).astype(BF16)
    trow = lax.broadcasted_iota(I32, (tm, 1), 0) % T

    def conv(w_ref, cw_ref, cb_ref, p_ref, o_ref, e_scr):
        u = _dot(h, w_ref[...])
        e_scr[8:8 + tm, :] = u
        if streamed:
            @pl.when((i * tm) % T == 0)
            def _():
                e_scr[6:8, :] = p_ref[...]
            o_ref[...] = u[tm - 2:tm, :]
            u1 = e_scr[7:7 + tm, :]
            u2 = e_scr[6:6 + tm, :]
            e_scr[6:8, :] = u[tm - 2:tm, :]
        else:
            e_scr[0:8, :] = jnp.zeros((8, u.shape[1]), F32)
            o_ref[...] = u
            u1 = jnp.where(trow >= 1, e_scr[7:7 + tm, :], p_ref[0])
            u2 = jnp.where(trow >= 2, e_scr[6:6 + tm, :], p_ref[1])
        return cb_ref[...] + cw_ref[0:1, :] * u2 + cw_ref[1:2, :] * u1 + cw_ref[2:3, :] * u

    gate = conv(wg_ref, cwg_ref, cbg_ref, pg_ref, og_ref, eg_scr)
    val = conv(wv_ref, cwv_ref, cbv_ref, pv_ref, ov_ref, ev_scr)
    act_ref[...] = (_gelu_tanh(gate) * val).astype(BF16)


def _ffn_up(x, g, w_up, conv_w, conv_b, prev, T, tm, tn):
    M = x.shape[0]
    nj = D_FF // tn
    streamed = T >= tm
    tpb = max(T // tm, 1)
    if streamed:
        p_spec = lambda off: pl.BlockSpec((None, 2, tn), lambda j, i: (i // tpb, 0, j + off))
        o_shape = jax.ShapeDtypeStruct((M // T, 2, D_FF), F32)
        o_spec = pl.BlockSpec((None, 2, tn), lambda j, i: (i // tpb, 0, j))
    else:
        p_spec = lambda off: pl.BlockSpec((2, tm, tn), lambda j, i: (0, i, j + off))
        o_shape = jax.ShapeDtypeStruct((M, D_FF), F32)
        o_spec = pl.BlockSpec((tm, tn), lambda j, i: (i, j))
    col = lambda rows, off: pl.BlockSpec((rows, tn), lambda j, i: (0, j + off))
    return pl.pallas_call(
        functools.partial(_ffn_up_kernel, tm=tm, T=T, streamed=streamed),
        grid=(nj, M // tm),
        in_specs=[pl.BlockSpec((tm, D_MODEL), lambda j, i: (i, 0)),
                  pl.BlockSpec((1, D_MODEL), lambda j, i: (0, 0)),
                  col(D_MODEL, 0), col(D_MODEL, nj), col(3, 0), col(3, nj), col(1, 0), col(1, nj),
                  p_spec(0), p_spec(nj)],
        out_specs=[pl.BlockSpec((tm, tn), lambda j, i: (i, j)), o_spec, o_spec],
        out_shape=[jax.ShapeDtypeStruct((M, D_FF), BF16), o_shape, o_shape],
        scratch_shapes=[pltpu.VMEM((tm + 8, tn), F32), pltpu.VMEM((tm + 8, tn), F32)],
        compiler_params=_cparams("parallel", "arbitrary"),
        name="ffn_up",
    )(x, g, w_up, w_up, conv_w, conv_w, conv_b, conv_b, prev, prev)


def _rk_pre_kernel(x_ref, g_ref, sp_ref, mix_ref, wr_ref, wk_ref, wv_ref, w1_ref, w2_ref, a1_ref, a2_ref,
                   g1_ref, g2_ref, vec_ref,
                   hn_ref, r_ref, wl_ref, kf_ref, v_ref, av_ref, bv_ref, gg_ref, e_scr, *, tm, T, streamed):
    i = pl.program_id(0)
    h = _rms(x_ref[...], g_ref[...])
    hn_ref[...] = h
    e_scr[8:8 + tm, :] = h
    if streamed:
        @pl.when((i * tm) % T == 0)
        def _():
            e_scr[7:8, :] = sp_ref[...]
        xp = e_scr[7:7 + tm, :]
        e_scr[7:8, :] = h[tm - 1:tm, :]
    else:
        e_scr[0:8, :] = jnp.zeros((8, D_MODEL), F32)
        trow = lax.broadcasted_iota(I32, (tm, 1), 0) % T
        xp = jnp.where(trow >= 1, e_scr[7:7 + tm, :], sp_ref[...])
    xx = xp - h
    mixed = lambda j: (h + xx * mix_ref[j:j + 1, :]).astype(BF16)
    w0, a0, k_k, k_a = vec_ref[0:1, :], vec_ref[1:2, :], vec_ref[2:3, :], vec_ref[3:4, :]
    r = _dot(mixed(0), wr_ref[...])
    k = _dot(mixed(2), wk_ref[...])
    v = _dot(mixed(3), wv_ref[...])
    lw = w0 + _dot(jnp.tanh(_dot(mixed(1), w1_ref[...])).astype(BF16), w2_ref[...])
    w_log = -(jnp.maximum(-lw, 0.0) + jnp.log(1.0 + jnp.exp(-jnp.abs(lw)))) - 0.5
    wl_ref[...] = -jnp.exp(w_log)
    a = jax.nn.sigmoid(a0 + _dot(_dot(mixed(4), a1_ref[...]).astype(BF16), a2_ref[...]))
    gg_ref[...] = _dot(jax.nn.sigmoid(_dot(mixed(5), g1_ref[...])).astype(BF16), g2_ref[...])
    kk = k * k_k
    bd = _seg_ones()
    kk = kk / jnp.maximum(jnp.sqrt(_seg64_sum(kk * kk, bd)), 1e-12)
    r_ref[...] = r
    v_ref[...] = v
    kf_ref[...] = k * (1.0 + (a - 1.0) * k_a)
    av_ref[...] = -kk
    bv_ref[...] = kk * a


def _rk_pre(x, g, sp, mix, wr, wk, wv, w1, w2, a1, a2, g1, g2, vecs, T, tm):
    M = x.shape[0]
    streamed = T >= tm
    tpb = max(T // tm, 1)
    row = lambda i: (i, 0)
    const = lambda i: (0, 0)
    sp_spec = (pl.BlockSpec((None, 1, D_MODEL), lambda i: (i // tpb, 0, 0)) if streamed
               else pl.BlockSpec((tm, D_MODEL), row))
    full = lambda a: pl.BlockSpec(a.shape, const)
    return pl.pallas_call(
        functools.partial(_rk_pre_kernel, tm=tm, T=T, streamed=streamed),
        grid=(M // tm,),
        in_specs=[pl.BlockSpec((tm, D_MODEL), row), pl.BlockSpec((1, D_MODEL), const), sp_spec,
                  full(mix), full(wr), full(wk), full(wv), full(w1), full(w2), full(a1), full(a2),
                  full(g1), full(g2), full(vecs)],
        out_specs=[pl.BlockSpec((tm, D_MODEL), row)] * 8,
        out_shape=[jax.ShapeDtypeStruct((M, D_MODEL), F32)] * 8,
        scratch_shapes=[pltpu.VMEM((tm + 8, D_MODEL), F32)],
        compiler_params=_cparams("arbitrary"),
        name="rk_pre",
    )(x, g, sp, mix, wr, wk, wv, w1, w2, a1, a2, g1, g2, vecs)


_HP = lax.Precision.HIGHEST


def _hdot(a, b):
    return jnp.dot(a, b, preferred_element_type=F32, precision=_HP)


def _hdot_nt(a, b):
    return lax.dot_general(a, b, (((1,), (1,)), ((), ())), preferred_element_type=F32, precision=_HP)


def _hdot_tn(a, b):
    return lax.dot_general(a, b, (((0,), (0,)), ((), ())), preferred_element_type=F32, precision=_HP)


def _rk_scan_kernel(r_ref, wl_ref, k_ref, v_ref, a_ref, b_ref, s0_ref, y_ref, sf_ref, s_scr, *, C, npair):
    c = pl.program_id(2)

    @pl.when(c == 0)
    def _():
        s_scr[...] = s0_ref[0]

    lane = lax.broadcasted_iota(I32, (1, LANES), 1)
    m0 = lane < 64
    rr = lax.broadcasted_iota(I32, (C, C), 0)
    cc = lax.broadcasted_iota(I32, (C, C), 1)
    tri_incl = (cc <= rr).astype(F32)
    r2 = lax.broadcasted_iota(I32, (2 * C, 2 * C), 0)
    c2 = lax.broadcasted_iota(I32, (2 * C, 2 * C), 1)
    strict = (r2 % C) > (c2 % C)
    incl = (r2 % C) >= (c2 % C)
    eye = (r2 == c2).astype(F32)
    n_double = max(int(math.log2(C)) - 1, 0)

    def stack2(z):
        return jnp.concatenate([jnp.where(m0, z, 0.0), jnp.where(m0, 0.0, z)], axis=0)

    for p in range(npair):
        sl = slice(LANES * p, LANES * (p + 1))
        wl = wl_ref[:, sl]
        ld = _hdot(tri_incl, wl)
        dfull = jnp.exp(ld)
        dinv = jnp.exp(-ld)
        As = stack2(a_ref[:, sl] * jnp.exp(ld - wl))
        Bs = stack2(b_ref[:, sl] * dinv)
        Ks = stack2(k_ref[:, sl] * dinv)
        Rs = stack2(r_ref[:, sl] * dfull)
        Vs = stack2(v_ref[:, sl])
        BK = jnp.concatenate([Bs, Ks], axis=0)
        G1 = _hdot_nt(As, BK)
        G2 = _hdot_nt(Rs, BK)
        Lab = jnp.where(strict, G1[:, :2 * C], 0.0)
        Lak = jnp.where(strict, G1[:, 2 * C:], 0.0)
        Mrb = jnp.where(incl, G2[:, :2 * C], 0.0)
        Mrk = jnp.where(incl, G2[:, 2 * C:], 0.0)
        Tm = eye + Lab
        P = Lab
        for _ in range(n_double):
            P = _hdot(P, P)
            Tm = Tm + _hdot(Tm, P)
        S = s_scr[p]
        U = _hdot(Tm, _hdot_nt(As, S) + _hdot(Lak, Vs))
        Ys = _hdot_nt(Rs, S) + _hdot(Mrb, U) + _hdot(Mrk, Vs)
        y_ref[:, sl] = Ys[:C] + Ys[C:]
        s_scr[p] = (S + _hdot_tn(U, Bs) + _hdot_tn(Vs, Ks)) * dfull[C - 1:C, :]

    @pl.when(c == pl.num_programs(2) - 1)
    def _():
        sf_ref[0] = s_scr[...]


def _rk_scan(r, wl, k, v, a, b, s0, B, T, C, npair):
    nc = T // C
    ng = 8 // npair
    blk = pl.BlockSpec((C, LANES * npair), lambda bb, g, c: (bb * nc + c, g))
    s_spec = pl.BlockSpec((1, npair, LANES, LANES), lambda bb, g, c: (bb, g, 0, 0))
    return pl.pallas_call(
        functools.partial(_rk_scan_kernel, C=C, npair=npair),
        grid=(B, ng, nc),
        in_specs=[blk] * 6 + [s_spec],
        out_specs=[blk, s_spec],
        out_shape=[jax.ShapeDtypeStruct(r.shape, F32), jax.ShapeDtypeStruct(s0.shape, F32)],
        scratch_shapes=[pltpu.VMEM((npair, LANES, LANES), F32)],
        compiler_params=_cparams("parallel", "parallel", "arbitrary"),
        name="rk_scan",
    )(r, wl, k, v, a, b, s0)


def _rk_post_kernel(y_ref, r_ref, kf_ref, v_ref, gg_ref, vec_ref, o_ref):
    bd = _seg_ones()
    ln_w, ln_b, r_k = vec_ref[0:1, :], vec_ref[1:2, :], vec_ref[2:3, :]
    y = y_ref[...]
    mu = _seg64_sum(y, bd) * (1.0 / RK_N)
    d = y - mu
    var = _seg64_sum(d * d, bd) * (1.0 / RK_N)
    yn = d * lax.rsqrt(var + RK_GN_EPS) * ln_w + ln_b
    bonus = _seg64_sum(r_ref[...] * kf_ref[...] * r_k, bd) * v_ref[...]
    o_ref[...] = ((yn + bonus) * gg_ref[...]).astype(BF16)


def _rk_post(y, r, kf, v, gg, vecs, tm):
    M = y.shape[0]
    row = lambda i: (i, 0)
    return pl.pallas_call(
        _rk_post_kernel,
        grid=(M // tm,),
        in_specs=[pl.BlockSpec((tm, D_MODEL), row)] * 5 + [pl.BlockSpec(vecs.shape, lambda i: (0, 0))],
        out_specs=pl.BlockSpec((tm, D_MODEL), row),
        out_shape=jax.ShapeDtypeStruct((M, D_MODEL), BF16),
        compiler_params=_cparams("parallel"),
        name="rk_post",
    )(y, r, kf, v, gg, vecs)


def _row_tile(M):
    return min(256, M)


def _pad_rows(a, n):
    return jnp.pad(a, ((0, 0), (0, n - a.shape[1])) + ((0, 0),) * (a.ndim - 2))


def _attn_layer_prompt(x, B, T, g_pre, w_in, w_outs, g_post):
    M = B * T
    tm = _row_tile(M)
    cos, sin = _rope_tables(jnp.arange(T, dtype=I32))
    qa, ka, va, qb, kb, vb, qi, ki2, wi = _attn_proj(x, g_pre, w_in, cos, sin, tm)
    n_sel = min(DSA_TOPK, T // 4)
    oa = _sb_prompt(qa, ka, va, B, T, min(256, T))
    ob = _dsa_prompt(qi, wi, ki2, qb, kb, vb, B, T, min(128, T), n_sel)
    x = _mm_norm_res([oa, ob], w_outs, g_post, x, tm, "attn_out")
    rows = (ka.reshape(1, B, T, H_SB, D_HEAD), va.reshape(1, B, T, H_SB, D_HEAD),
            kb.reshape(1, B, T, KV_DSA, D_HEAD), vb.reshape(1, B, T, KV_DSA, D_HEAD),
            ki2[:, :D_IDX].reshape(1, B, T, D_IDX))
    return x, rows


def _attn_layer_sample(x, B, T, P, page_table, caches, g_pre, w_in, w_outs, g_post):
    M = B * T
    cos, sin = _rope_tables(P + jnp.arange(T, dtype=I32))
    cos, sin = jnp.tile(cos, (B, 1)), jnp.tile(sin, (B, 1))
    qa, ka, va, qb, kb, vb, qi, ki2, wi = _attn_proj(x, g_pre, w_in, cos, sin, M)
    c_sb_k, c_sb_v, c_dsa_k, c_dsa_v, c_idx = caches
    n_pool = c_sb_k.shape[0]
    n_sel = min(DSA_TOPK, (P + T) // 4)
    b3 = lambda a: a.reshape(B, T, a.shape[-1])

    oa = _sb_sample(page_table, _pad_rows(b3(qa), SROWS), _pad_rows(b3(ka), PAGE), _pad_rows(b3(va), PAGE),
                    c_sb_k.reshape(n_pool, PAGE * H_SB, D_HEAD), c_sb_v.reshape(n_pool, PAGE * H_SB, D_HEAD))
    oa = oa[:, :T].reshape(M, H_SB * D_HEAD)

    qi_s = _pad_rows(b3(qi).reshape(B, T, H_IDX, D_IDX), SROWS).transpose(0, 2, 1, 3).reshape(B, H_IDX * SROWS, D_IDX)
    wi_s = _pad_rows(b3(wi)[:, :, :H_IDX], SROWS).transpose(0, 2, 1).reshape(B, H_IDX * SROWS, 1)
    keys, thr, jl = _dsa_score_sample(page_table, qi_s, wi_s, _pad_rows(b3(ki2)[:, :, :D_IDX], PAGE), c_idx, n_sel)
    qb_s = _pad_rows(b3(qb).reshape(B, T, H_DSA, D_HEAD), SROWS).transpose(0, 2, 1, 3).reshape(B, KV_DSA, _GR, D_HEAD)
    ob = _dsa_att_sample(page_table, qb_s, keys, thr, jl, _pad_rows(b3(kb), PAGE), _pad_rows(b3(vb), PAGE),
                         c_dsa_k.reshape(n_pool, PAGE * KV_DSA, D_HEAD), c_dsa_v.reshape(n_pool, PAGE * KV_DSA, D_HEAD))
    ob = ob.reshape(B, H_DSA, SROWS, D_HEAD)[:, :, :T].transpose(0, 2, 1, 3).reshape(M, H_DSA * D_HEAD)

    x = _mm_norm_res([oa, ob], w_outs, g_post, x, M, "attn_out")
    rows = (ka.reshape(1, B, T, H_SB, D_HEAD), va.reshape(1, B, T, H_SB, D_HEAD),
            kb.reshape(1, B, T, KV_DSA, D_HEAD), vb.reshape(1, B, T, KV_DSA, D_HEAD),
            ki2[:, :D_IDX].reshape(1, B, T, D_IDX))
    return x, rows


def _pair_state(S):
    B = S.shape[0]
    S = S.reshape(B, 8, 2, RK_N, RK_N)
    z = jnp.zeros_like(S[:, :, 0])
    top = jnp.concatenate([S[:, :, 0], z], axis=-1)
    bot = jnp.concatenate([z, S[:, :, 1]], axis=-1)
    return jnp.concatenate([top, bot], axis=-2)


def _unpair_state(Sp):
    B = Sp.shape[0]
    return jnp.stack([Sp[:, :, :RK_N, :RK_N], Sp[:, :, RK_N:, RK_N:]], axis=2).reshape(B, RK_H, RK_N, RK_N)


RK_CHUNK = 64


def _rwkv_layer(x, B, T, shift_prev, S0, g_pre, prm, g_post):
    M = B * T
    tm = _row_tile(M)
    (mix, wr, wk, wv, wo, w1, w2, a1, a2, g1, g2, vec_pre, vec_post) = prm
    if T >= tm:
        sp = shift_prev.reshape(B, 1, D_MODEL)
    else:
        sp = jnp.repeat(shift_prev, T, axis=0)
    hn, r, wl, kf, v, av, bv, gg = _rk_pre(x, g_pre, sp, mix, wr, wk, wv, w1, w2, a1, a2, g1, g2, vec_pre, T, tm)
    Tp = -(-T // RK_CHUNK) * RK_CHUNK
    if Tp != T:
        padt = lambda a: _pad_rows(a.reshape(B, T, D_MODEL), Tp).reshape(B * Tp, D_MODEL)
        y, Sf = _rk_scan(padt(r), padt(wl), padt(kf), padt(v), padt(av), padt(bv), _pair_state(S0), B, Tp, RK_CHUNK, 4)
        y = y.reshape(B, Tp, D_MODEL)[:, :T].reshape(M, D_MODEL)
    else:
        y, Sf = _rk_scan(r, wl, kf, v, av, bv, _pair_state(S0), B, T, RK_CHUNK, 4)
    z = _rk_post(y, r, kf, v, gg, vec_post, tm)
    x = _mm_norm_res([z], [wo], g_post, x, tm, "rk_out")
    shift = hn.reshape(B, T, D_MODEL)[:, -1]
    return x, shift, _unpair_state(Sf)


def _ffn_layer(x, B, T, prev, g_pre, w_up, conv_w, conv_b, w_down, g_post):
    M = B * T
    tm = _row_tile(M)
    tn = D_FF // 2
    if T >= tm:
        act, cg, cv = _ffn_up(x, g_pre, w_up, conv_w, conv_b, prev, T, tm, tn)
        conv_state = jnp.concatenate([cg, cv], axis=-1)
    else:
        zeros = jnp.zeros((B, T, 2 * D_FF), F32)
        p1 = zeros.at[:, 0].set(prev[:, 1])
        p2 = zeros.at[:, 0].set(prev[:, 0]).at[:, 1].set(prev[:, 1])
        pp = jnp.stack([p1.reshape(M, -1), p2.reshape(M, -1)])
        act, ug, uv = _ffn_up(x, g_pre, w_up, conv_w, conv_b, pp, T, tm, tn)
        u = jnp.concatenate([ug, uv], axis=-1).reshape(B, T, 2 * D_FF)
        conv_state = jnp.concatenate([prev, u], axis=1)[:, -2:]
    x = _mm_norm_res([act], [w_down], g_post, x, tm, "ffn_down")
    return x, conv_state


def _forward(x_prompt, x_sample, cache_sb_k, cache_sb_v, cache_dsa_k, cache_dsa_v, cache_idx_k, page_table,
             state_wkv, state_shift, state_ffn_conv,
             norm_mix_pre, norm_mix_post, norm_ffn_pre, norm_ffn_post,
             att_w_in, att_w_out,
             rk_mix, rk_w_r, rk_w_k, rk_w_v, rk_w_o, rk_w0, rk_w1, rk_w2, rk_a0, rk_a1, rk_a2,
             rk_g1, rk_g2, rk_k_k, rk_k_a, rk_r_k, rk_ln_w, rk_ln_b,
             ffn_w_up, ffn_conv_w, ffn_conv_b, ffn_w_down):
    Bp, Tp, D = x_prompt.shape
    Bs, Ts, _ = x_sample.shape
    P = page_table.shape[1] * PAGE
    depth = norm_mix_pre.shape[0]
    xp = x_prompt.reshape(Bp * Tp, D)
    xs = x_sample.reshape(Bs * Ts, D)
    vrow = lambda a: a.reshape(1, -1)
    bf = lambda a: a.astype(BF16)
    att_p, att_s, wkv_p, wkv_s, sh_p, sh_s, cv_p, cv_s = [], [], [], [], [], [], [], []
    for i in range(depth):
        li = i // 2
        g_pre, g_post = vrow(norm_mix_pre[i]), vrow(norm_mix_post[i])
        if i % 2 == 0:
            w_in = _pack_w_in(att_w_in[li])
            w_outs = [bf(att_w_out[li][:H_SB * D_HEAD]), bf(att_w_out[li][H_SB * D_HEAD:])]
            caches = (cache_sb_k[li], cache_sb_v[li], cache_dsa_k[li], cache_dsa_v[li], cache_idx_k[li])
            xp, rows_p = _attn_layer_prompt(xp, Bp, Tp, g_pre, w_in, w_outs, g_post)
            xs, rows_s = _attn_layer_sample(xs, Bs, Ts, P, page_table, caches, g_pre, w_in, w_outs, g_post)
            att_p.append(rows_p)
            att_s.append(rows_s)
        else:
            vec_pre = jnp.stack([rk_w0[li], rk_a0[li], rk_k_k[li], rk_k_a[li]])
            vec_post = jnp.stack([rk_ln_w[li], rk_ln_b[li], rk_r_k[li].reshape(-1)])
            prm = (rk_mix[li], bf(rk_w_r[li]), bf(rk_w_k[li]), bf(rk_w_v[li]), bf(rk_w_o[li]),
                   bf(rk_w1[li]), bf(rk_w2[li]), bf(rk_a1[li]), bf(rk_a2[li]), bf(rk_g1[li]), bf(rk_g2[li]),
                   vec_pre, vec_post)
            xp, shp, Sp = _rwkv_layer(xp, Bp, Tp, jnp.zeros((Bp, D), F32), jnp.zeros((Bp, RK_H, RK_N, RK_N), F32),
                                      g_pre, prm, g_post)
            xs, shs, Ss = _rwkv_layer(xs, Bs, Ts, state_shift[li], state_wkv[li], g_pre, prm, g_post)
            wkv_p.append(Sp)
            wkv_s.append(Ss)
            sh_p.append(shp)
            sh_s.append(shs)
        f_pre, f_post = vrow(norm_ffn_pre[i]), vrow(norm_ffn_post[i])
        w_up, w_down = bf(ffn_w_up[i]), bf(ffn_w_down[i])
        cb = vrow(ffn_conv_b[i])
        xp, cp = _ffn_layer(xp, Bp, Tp, jnp.zeros((Bp, 2, 2 * D_FF), F32), f_pre, w_up, ffn_conv_w[i], cb, w_down, f_post)
        xs, cs = _ffn_layer(xs, Bs, Ts, state_ffn_conv[i], f_pre, w_up, ffn_conv_w[i], cb, w_down, f_post)
        cv_p.append(cp)
        cv_s.append(cs)
    cat = lambda rows, j: jnp.concatenate([r[j] for r in rows], axis=0)
    return (xp.reshape(Bp, Tp, D), xs.reshape(Bs, Ts, D),
            cat(att_p, 0), cat(att_s, 0), cat(att_p, 1), cat(att_s, 1),
            cat(att_p, 2), cat(att_s, 2), cat(att_p, 3), cat(att_s, 3),
            cat(att_p, 4), cat(att_s, 4),
            jnp.stack(wkv_p), jnp.stack(wkv_s), jnp.stack(sh_p), jnp.stack(sh_s),
            jnp.stack(cv_p), jnp.stack(cv_s))


def kernel(x_prompt, x_sample, cache_sb_k, cache_sb_v, cache_dsa_k, cache_dsa_v, cache_idx_k, page_table, state_wkv, state_shift, state_ffn_conv, norm_mix_pre, norm_mix_post, norm_ffn_pre, norm_ffn_post, att_w_in, att_w_out, rk_mix, rk_w_r, rk_w_k, rk_w_v, rk_w_o, rk_w0, rk_w1, rk_w2, rk_a0, rk_a1, rk_a2, rk_g1, rk_g2, rk_k_k, rk_k_a, rk_r_k, rk_ln_w, rk_ln_b, ffn_w_up, ffn_conv_w, ffn_conv_b, ffn_w_down):
    return _forward(x_prompt, x_sample, cache_sb_k, cache_sb_v, cache_dsa_k, cache_dsa_v, cache_idx_k, page_table,
                    state_wkv, state_shift, state_ffn_conv,
                    norm_mix_pre, norm_mix_post, norm_ffn_pre, norm_ffn_post,
                    att_w_in, att_w_out,
                    rk_mix, rk_w_r, rk_w_k, rk_w_v, rk_w_o, rk_w0, rk_w1, rk_w2, rk_a0, rk_a1, rk_a2,
                    rk_g1, rk_g2, rk_k_k, rk_k_a, rk_r_k, rk_ln_w, rk_ln_b,
                    ffn_w_up, ffn_conv_w, ffn_conv_b, ffn_w_down)
```

```python
import functools
import math

import jax
import jax.numpy as jnp
from jax import lax
from jax.experimental import pallas as pl
from jax.experimental.pallas import tpu as pltpu

F32 = jnp.float32
BF16 = jnp.bfloat16
I32 = jnp.int32

D_MODEL = 1024
D_HEAD = 64
H_SB = 8
H_DSA = 8
KV_DSA = 2
H_IDX = 8
D_IDX = 64
DSA_TOPK = 256
PAGE = 128
ROPE_THETA = 10000.0
RK_N = 64
RK_H = D_MODEL // RK_N
RK_GN_EPS = 64e-5
D_FF = 2816
NORM_EPS = 1e-6

LANES = 128
VMEM_LIMIT = 56 * 1024 * 1024
INT_MIN = -2147483648
INT_MAX = 2147483647
NEG_BIG = -1e30

_C_QA, _C_KA, _C_VA, _C_QB, _C_KB, _C_VB, _C_QI, _C_KI, _C_WI, _C_END = (
    0, 512, 1024, 1536, 2048, 2176, 2304, 2816, 2944, 3072)


def _cparams(*sem):
    return pltpu.CompilerParams(dimension_semantics=sem, vmem_limit_bytes=VMEM_LIMIT)


def _dot(a, b):
    return jnp.dot(a, b, preferred_element_type=F32)


def _dot_nt(a, b):
    return lax.dot_general(a, b, (((1,), (1,)), ((), ())), preferred_element_type=F32)


def _dot_tn(a, b):
    return lax.dot_general(a, b, (((0,), (0,)), ((), ())), preferred_element_type=F32)


def _split(x):
    hi = x.astype(BF16)
    lo = (x - hi.astype(F32)).astype(BF16)
    return hi, lo


def _dot_x2(x, m_bf16):
    hi, lo = _split(x)
    return _dot(hi, m_bf16) + _dot(lo, m_bf16)


def _rms(x, g):
    return x * lax.rsqrt(jnp.mean(x * x, axis=-1, keepdims=True) + NORM_EPS) * g


def _seg_ones():
    r = lax.broadcasted_iota(I32, (LANES, LANES), 0)
    c = lax.broadcasted_iota(I32, (LANES, LANES), 1)
    return ((r < 64) == (c < 64)).astype(BF16)


def _seg64_sum(x, bd):
    n = x.shape[1] // LANES
    return jnp.concatenate([_dot_x2(x[:, LANES * m:LANES * (m + 1)], bd) for m in range(n)], axis=1)


def _attn_proj_kernel(x_ref, g_ref, w_ref, cos_ref, sin_ref,
                      qa_ref, ka_ref, va_ref, qb_ref, kb_ref, vb_ref, qi_ref, ki_ref, wi_ref):
    h = _rms(x_ref[...], g_ref[...]).astype(BF16)
    cos = cos_ref[...]
    sin = sin_ref[...]
    lane = lax.broadcasted_iota(I32, (1, LANES), 1)
    first = (lane & 32) == 0

    def proj(c0, c1):
        return _dot(h, w_ref[:, c0:c1])

    def rope(blk):
        rot = jnp.where(first, pltpu.roll(blk, LANES - 32, 1), pltpu.roll(blk, 32, 1))
        return blk * cos + rot * sin

    qa_ref[...] = proj(_C_QA, _C_KA)
    ka_ref[...] = proj(_C_KA, _C_VA)
    va_ref[...] = proj(_C_VA, _C_QB)
    for m in range(4):
        qb_ref[:, LANES * m:LANES * (m + 1)] = rope(proj(_C_QB + LANES * m, _C_QB + LANES * (m + 1)))
        qi_ref[:, LANES * m:LANES * (m + 1)] = rope(proj(_C_QI + LANES * m, _C_QI + LANES * (m + 1)))
    kb_ref[...] = rope(proj(_C_KB, _C_VB))
    vb_ref[...] = proj(_C_VB, _C_QI)
    ki_ref[...] = rope(proj(_C_KI, _C_WI))
    wi_ref[...] = proj(_C_WI, _C_END)


def _attn_proj(x, g, w, cos, sin, tm):
    M = x.shape[0]
    nt = cos.shape[0] // tm
    widths = (512, 512, 512, 512, 128, 128, 512, 128, 128)
    row = lambda i: (i, 0)
    return pl.pallas_call(
        _attn_proj_kernel,
        grid=(M // tm,),
        in_specs=[pl.BlockSpec((tm, D_MODEL), row),
                  pl.BlockSpec((1, D_MODEL), lambda i: (0, 0)),
                  pl.BlockSpec((D_MODEL, _C_END), lambda i: (0, 0)),
                  pl.BlockSpec((tm, LANES), lambda i: (i % nt, 0)),
                  pl.BlockSpec((tm, LANES), lambda i: (i % nt, 0))],
        out_specs=[pl.BlockSpec((tm, wd), row) for wd in widths],
        out_shape=[jax.ShapeDtypeStruct((M, wd), F32) for wd in widths],
        compiler_params=_cparams("parallel"),
        name="attn_proj",
    )(x, g, w, cos, sin)


def _rope_tables(pos):
    half = D_HEAD // 2
    inv = ROPE_THETA ** (-2.0 * jnp.arange(half, dtype=F32) / D_HEAD)
    ang = pos.astype(F32)[:, None] * inv[None, :]
    cos = jnp.cos(ang)
    sin = jnp.sin(ang)
    return jnp.tile(cos, (1, 4)), jnp.tile(jnp.concatenate([-sin, sin], axis=1), (1, 2))


def _pack_w_in(w_in):
    ki = w_in[:, 2816:2880]
    wi = w_in[:, 2880:2888]
    pad = jnp.zeros((w_in.shape[0], LANES - H_IDX), w_in.dtype)
    return jnp.concatenate([w_in[:, :2816], ki, ki, wi, pad], axis=1).astype(BF16)


def _sb_block(z, c, tri, vb, causal):
    t = jnp.log(1.0 + jnp.exp(-jnp.abs(z)))
    lb = jnp.minimum(z, 0.0) - t
    lk = lb - z
    if causal is not None:
        lk = jnp.where(causal, lk, 0.0)
    aft = _dot(lk.astype(BF16), tri) + c
    w = jnp.exp(lb + aft)
    if causal is not None:
        w = jnp.where(causal, w, 0.0)
    pv = _dot(w.astype(BF16), vb)
    return c + jnp.sum(lk, axis=1, keepdims=True), pv


def _sb_prompt_kernel(q_ref, k_ref, v_ref, o_ref, *, tq):
    i = pl.program_id(2)
    lane = lax.broadcasted_iota(I32, (1, LANES), 1)
    row = lax.broadcasted_iota(I32, (tq, tq), 0)
    col = lax.broadcasted_iota(I32, (tq, tq), 1)
    tri = (row > col).astype(BF16)
    causal = col < row
    q = q_ref[...] * (D_HEAD ** -0.5)
    qh = [jnp.where(lane < 64, q, 0.0).astype(BF16), jnp.where(lane >= 64, q, 0.0).astype(BF16)]

    def step(js, carry, mask):
        kbs, vbs = [], []
        for j in js:
            off = pl.multiple_of(j * tq, tq)
            kbs.append(k_ref[pl.ds(off, tq), :].astype(BF16))
            vbs.append(v_ref[pl.ds(off, tq), :].astype(BF16))
        tiles = [(h2, n) for n in range(len(js)) for h2 in range(2)]
        z = [_dot_nt(qh[h2], kbs[n]) for h2, n in tiles]
        t = [jnp.log(1.0 + jnp.exp(-jnp.abs(x))) for x in z]
        lb = [jnp.minimum(x, 0.0) - y for x, y in zip(z, t)]
        lk = [x - y for x, y in zip(lb, z)]
        if mask is not None:
            lk = [jnp.where(mask, x, 0.0) for x in lk]
        loc = [_dot(x.astype(BF16), tri) for x in lk]
        tot = [jnp.sum(x, axis=1, keepdims=True) for x in lk]
        c = [carry[0], carry[2]]
        w = []
        for idx, (h2, n) in enumerate(tiles):
            x = jnp.exp(lb[idx] + loc[idx] + c[h2])
            w.append((jnp.where(mask, x, 0.0) if mask is not None else x).astype(BF16))
            c[h2] = c[h2] + tot[idx]
        pv = [_dot(w[idx], vbs[n]) for idx, (h2, n) in enumerate(tiles)]
        acc = [carry[1], carry[3]]
        for idx, (h2, n) in enumerate(tiles):
            acc[h2] = acc[h2] + pv[idx]
        return c[0], acc[0], c[1], acc[1]

    zc, za = jnp.zeros((tq, 1), F32), jnp.zeros((tq, LANES), F32)
    carry = step([i], (zc, za, zc, za), causal)
    carry = lax.fori_loop(0, i // 2, lambda jj, cr: step([i - 1 - 2 * jj, i - 2 - 2 * jj], cr, None), carry)
    carry = lax.fori_loop(0, i % 2, lambda jj, cr: step([0], cr, None), carry)
    o_ref[...] = jnp.where(lane < 64, carry[1], carry[3])


def _sb_prompt(q, k, v, B, T, tq):
    nq = T // tq
    return pl.pallas_call(
        functools.partial(_sb_prompt_kernel, tq=tq),
        grid=(B, 4, nq),
        in_specs=[pl.BlockSpec((tq, LANES), lambda b, p, i: (b * nq + i, p)),
                  pl.BlockSpec((T, LANES), lambda b, p, i: (b, p)),
                  pl.BlockSpec((T, LANES), lambda b, p, i: (b, p))],
        out_specs=pl.BlockSpec((tq, LANES), lambda b, p, i: (b * nq + i, p)),
        out_shape=jax.ShapeDtypeStruct(q.shape, F32),
        compiler_params=_cparams("parallel", "parallel", "arbitrary"),
        name="sb_prompt",
    )(q, k, v)


SROWS = 16


def _pages_per_step(n_pages):
    for g in (8, 4, 2):
        if n_pages % g == 0:
            return g
    return 1


def _sb_sample_kernel(pt_ref, q_ref, kn_ref, vn_ref, *rest, G):
    kc_refs, vc_refs = rest[:G], rest[G:2 * G]
    o_ref, c_scr, acc_scr = rest[2 * G:]
    j = pl.program_id(1)
    row = lax.broadcasted_iota(I32, (PAGE, PAGE), 0)
    col = lax.broadcasted_iota(I32, (PAGE, PAGE), 1)
    tri = (row > col).astype(BF16)
    qrow = lax.broadcasted_iota(I32, (SROWS, PAGE), 0)
    kcol = lax.broadcasted_iota(I32, (SROWS, PAGE), 1)
    causal = kcol < qrow
    scale = D_HEAD ** -0.5

    def qhead(h):
        return (q_ref[0, :, D_HEAD * h:D_HEAD * (h + 1)] * scale).astype(BF16)

    @pl.when(j == 0)
    def _():
        for h in range(H_SB):
            kb = kn_ref[0, :, D_HEAD * h:D_HEAD * (h + 1)].astype(BF16)
            vb = vn_ref[0, :, D_HEAD * h:D_HEAD * (h + 1)].astype(BF16)
            c, pv = _sb_block(_dot_nt(qhead(h), kb), jnp.zeros((SROWS, 1), F32), tri, vb, causal)
            c_scr[h] = jnp.broadcast_to(c, (SROWS, LANES))
            acc_scr[h] = pv

    order = range(G - 1, -1, -1)
    zs = []
    for h in range(H_SB):
        kt = jnp.concatenate([kc_refs[s][h] for s in order], axis=1).astype(BF16)
        z = _dot(qhead(h), kt)
        zs += [z[:, PAGE * p:PAGE * (p + 1)] for p in range(G)]
    z = jnp.concatenate(zs, axis=0)
    t = jnp.log(1.0 + jnp.exp(-jnp.abs(z)))
    lb = jnp.minimum(z, 0.0) - t
    lk = lb - z
    loc = _dot(lk.astype(BF16), tri)
    tot = jnp.sum(lk, axis=1, keepdims=True)
    cs = []
    for h in range(H_SB):
        run = c_scr[h][:, 0:1]
        per_page = [None] * G
        for p in range(G - 1, -1, -1):
            per_page[p] = run
            run = run + tot[(h * G + p) * SROWS:(h * G + p + 1) * SROWS]
        c_scr[h] = jnp.broadcast_to(run, (SROWS, LANES))
        cs += per_page
    w = jnp.exp(lb + loc + jnp.concatenate(cs, axis=0)).astype(BF16)
    for h in range(H_SB):
        wh = jnp.concatenate([w[(h * G + p) * SROWS:(h * G + p + 1) * SROWS] for p in range(G)], axis=1)
        vt = jnp.concatenate([vc_refs[s][h] for s in order], axis=1).astype(BF16)
        acc_scr[h] = acc_scr[h] + _dot_nt(wh, vt)

    @pl.when(j == pl.num_programs(1) - 1)
    def _():
        for h in range(H_SB):
            o_ref[0, :, D_HEAD * h:D_HEAD * (h + 1)] = acc_scr[h]


def _sb_sample(pt, q, kn, vn, kc, vc):
    B, NP = pt.shape
    G = _pages_per_step(NP)
    page = lambda s: pl.BlockSpec((None, H_SB, D_HEAD, PAGE),
                                  lambda b, j, pt: (pt[b, NP - 1 - (j * G + s)], 0, 0, 0))
    grid_spec = pltpu.PrefetchScalarGridSpec(
        num_scalar_prefetch=1,
        grid=(B, NP // G),
        in_specs=([pl.BlockSpec((1, SROWS, 512), lambda b, j, pt: (b, 0, 0)),
                   pl.BlockSpec((1, PAGE, 512), lambda b, j, pt: (b, 0, 0)),
                   pl.BlockSpec((1, PAGE, 512), lambda b, j, pt: (b, 0, 0))]
                  + [page(s) for s in range(G)] * 2),
        out_specs=pl.BlockSpec((1, SROWS, 512), lambda b, j, pt: (b, 0, 0)),
        scratch_shapes=[pltpu.VMEM((H_SB, SROWS, LANES), F32), pltpu.VMEM((H_SB, SROWS, D_HEAD), F32)])
    return pl.pallas_call(
        functools.partial(_sb_sample_kernel, G=G),
        grid_spec=grid_spec,
        out_shape=jax.ShapeDtypeStruct((B, SROWS, 512), F32),
        compiler_params=_cparams("parallel", "arbitrary"),
        name="sb_sample",
    )(pt, q, kn, vn, *([kc] * G), *([vc] * G))


_IDX_SCALE = D_IDX ** -0.5 * H_IDX ** -0.5


def _score_key(score):
    score = jnp.where(score == 0.0, 0.0, score)
    bits = pltpu.bitcast(score, I32)
    return bits ^ ((bits >> 31) & INT_MAX)


def _topk_threshold(count_ge, count_eq_lt, rows, n_sel, idx_bits, jl_ref):
    def bit_body(it, tu):
        cand = tu | jnp.left_shift(jnp.int32(1), 31 - it)
        return jnp.where(count_ge(cand ^ INT_MIN) >= n_sel, cand, tu)

    tu = lax.fori_loop(0, 32, bit_body, jnp.zeros((rows, 1), I32))
    thr = tu ^ INT_MIN
    n_ge = count_ge(thr)
    n_gt = jnp.where(thr == INT_MAX, 0, count_ge(thr + 1))
    need = n_sel - n_gt
    tied = n_ge > n_sel
    jl_ref[...] = jnp.full((rows, 1), INT_MAX, I32)

    @pl.when(jnp.max(tied.astype(I32)) > 0)
    def _():
        def idx_body(it, lo):
            cand = lo | jnp.left_shift(jnp.int32(1), idx_bits - 1 - it)
            return jnp.where(count_eq_lt(thr, cand) < need, cand, lo)
        lo = lax.fori_loop(0, idx_bits, idx_body, jnp.zeros((rows, 1), I32))
        jl_ref[...] = jnp.where(tied, lo, INT_MAX)

    return thr


def _dsa_prompt_kernel(qi_ref, wi_ref, ki_ref, qb_ref, kb_ref, vb_ref, o_ref, key_scr, jl_scr,
                       *, tq, n_sel, idx_bits, AG):
    i = pl.program_id(1)
    tk = tq
    lane = lax.broadcasted_iota(I32, (1, LANES), 1)
    lo_half = lane < 64
    row = lax.broadcasted_iota(I32, (tq, tk), 0)
    col = lax.broadcasted_iota(I32, (tq, tk), 1)
    diag_ok = col <= row

    qs = []
    for h in range(H_IDX):
        blk = qi_ref[:, LANES * (h // 2):LANES * (h // 2 + 1)]
        qs.append(jnp.where(lo_half if h % 2 == 0 else ~lo_half, blk, 0.0))
    qstack = jnp.concatenate(qs, axis=0).astype(BF16)
    wb = [jnp.broadcast_to(wi_ref[:, h:h + 1], (tq, tk)) for h in range(H_IDX)]

    def score_chunk(c, mask):
        off = pl.multiple_of(c * tk, tk)
        kc = ki_ref[pl.ds(off, tk), :].astype(BF16)
        s = _dot_nt(qstack, kc)
        score = wb[0] * jnp.maximum(s[0:tq], 0.0)
        for h in range(1, H_IDX):
            score = score + wb[h] * jnp.maximum(s[h * tq:(h + 1) * tq], 0.0)
        key = _score_key(score * _IDX_SCALE)
        if mask is not None:
            key = jnp.where(mask, key, INT_MIN)
        key_scr[c] = key

    def score_body(c, carry):
        score_chunk(c, None)
        return carry

    lax.fori_loop(0, i, score_body, 0)
    score_chunk(i, diag_ok)
    for u in range(1, AG):
        @pl.when((i % AG) + u < AG)
        def _():
            key_scr[i + u] = jnp.full((tq, tk), INT_MIN, I32)

    def count_ge(cand):
        def body(c, acc):
            return acc + (key_scr[c] >= cand).astype(I32)
        acc = lax.fori_loop(0, i + 1, body, jnp.zeros((tq, tk), I32))
        return jnp.sum(acc, axis=1, keepdims=True)

    def count_eq_lt(thr, jcand):
        def body(c, acc):
            return acc + ((key_scr[c] == thr) & (col + c * tk < jcand)).astype(I32)
        acc = lax.fori_loop(0, i + 1, body, jnp.zeros((tq, tk), I32))
        return jnp.sum(acc, axis=1, keepdims=True)

    thr = _topk_threshold(count_ge, count_eq_lt, tq, n_sel, idx_bits, jl_scr)
    jl = jl_scr[...]

    qs = []
    for h in range(H_DSA):
        blk = qb_ref[:, LANES * (h // 2):LANES * (h // 2 + 1)] * (D_HEAD ** -0.5)
        g = h // (H_DSA // KV_DSA)
        if (h % 2) != g:
            blk = pltpu.roll(blk, 64, 1)
        qs.append(jnp.where(lo_half if g == 0 else ~lo_half, blk, 0.0))
    qstack2 = jnp.concatenate(qs, axis=0).astype(BF16)

    ta = AG * tk
    rowa = lax.broadcasted_iota(I32, (tq, ta), 0)
    cola = lax.broadcasted_iota(I32, (tq, ta), 1)

    def att_group(c, carry, last):
        m, l, acc = carry
        off = pl.multiple_of(c * ta, ta)
        kblk = jnp.concatenate([key_scr[c * AG + u] for u in range(AG)], axis=1)
        sel = (kblk > thr) | ((kblk == thr) & (cola + off <= jl))
        if last:
            sel = sel & (cola + off <= rowa + i * tq)
        z = _dot_nt(qstack2, kb_ref[pl.ds(off, ta), :].astype(BF16))
        z = jnp.where(sel[None], z.reshape(H_DSA, tq, ta), NEG_BIG).reshape(H_DSA * tq, ta)
        m_new = jnp.maximum(m, jnp.max(z, axis=1, keepdims=True))
        alpha = jnp.exp(m - m_new)
        p = jnp.exp(z - m_new)
        l = alpha * l + jnp.sum(p, axis=1, keepdims=True)
        acc = alpha * acc + _dot(p.astype(BF16), vb_ref[pl.ds(off, ta), :].astype(BF16))
        return m_new, l, acc

    carry = (jnp.full((H_DSA * tq, 1), NEG_BIG, F32), jnp.zeros((H_DSA * tq, 1), F32),
             jnp.zeros((H_DSA * tq, LANES), F32))
    carry = lax.fori_loop(0, i // AG, lambda c, cr: att_group(c, cr, False), carry)
    m, l, acc = att_group(i // AG, carry, True)
    out = acc / l
    for mblk in range(4):
        parts = []
        for h in (2 * mblk, 2 * mblk + 1):
            o_h = out[h * tq:(h + 1) * tq]
            if (h % 2) != h // (H_DSA // KV_DSA):
                o_h = pltpu.roll(o_h, 64, 1)
            parts.append(o_h)
        o_ref[:, LANES * mblk:LANES * (mblk + 1)] = jnp.where(lo_half, parts[0], parts[1])


def _dsa_prompt(qi, wi, ki2, qb, kb, vb, B, T, tq, n_sel):
    nq = T // tq
    qrow = lambda b, i: (b * nq + i, 0)
    full = lambda b, i: (b, 0)
    AG = 4 if nq % 4 == 0 else 1
    return pl.pallas_call(
        functools.partial(_dsa_prompt_kernel, tq=tq, n_sel=n_sel, idx_bits=max(1, (T - 1).bit_length()), AG=AG),
        grid=(B, nq),
        in_specs=[pl.BlockSpec((tq, 512), qrow), pl.BlockSpec((tq, LANES), qrow),
                  pl.BlockSpec((T, LANES), full), pl.BlockSpec((tq, 512), qrow),
                  pl.BlockSpec((T, LANES), full), pl.BlockSpec((T, LANES), full)],
        out_specs=pl.BlockSpec((tq, 512), qrow),
        out_shape=jax.ShapeDtypeStruct(qb.shape, F32),
        scratch_shapes=[pltpu.VMEM((T // tq, tq, tq), I32), pltpu.VMEM((tq, 1), I32)],
        compiler_params=_cparams("parallel", "arbitrary"),
        name="dsa_prompt",
    )(qi, wi, ki2, qb, kb, vb)


def _dsa_score_sample_kernel(pt_ref, q_ref, w_ref, kn_ref, *rest, G, n_pages, n_sel, idx_bits):
    kc_refs = rest[:G]
    key_ref, thr_ref, jl_ref = rest[G:]
    j = pl.program_id(1)
    qrow = lax.broadcasted_iota(I32, (SROWS, PAGE), 0)
    kcol = lax.broadcasted_iota(I32, (SROWS, PAGE), 1)
    q = q_ref[0].astype(BF16)
    w = w_ref[0]

    def score_keys(s):
        s = jnp.maximum(s, 0.0) * w
        sc = s[0:SROWS]
        for h in range(1, H_IDX):
            sc = sc + s[h * SROWS:(h + 1) * SROWS]
        return _score_key(sc * _IDX_SCALE)

    @pl.when(j < n_pages // G)
    def _():
        kt = jnp.concatenate([kc_refs[s][...] for s in range(G)], axis=1).astype(BF16)
        key = score_keys(_dot(q, kt))
        for s in range(G):
            key_ref[0, j * G + s] = key[:, PAGE * s:PAGE * (s + 1)]

    @pl.when(j == n_pages // G)
    def _():
        key = score_keys(_dot_nt(q, kn_ref[0].astype(BF16)))
        key_ref[0, n_pages] = jnp.where(kcol <= qrow, key, INT_MIN)
        nchunk = n_pages + 1

        def count_ge(cand):
            def body(c, acc):
                return acc + (key_ref[0, c] >= cand).astype(I32)
            acc = lax.fori_loop(0, nchunk, body, jnp.zeros((SROWS, PAGE), I32))
            return jnp.sum(acc, axis=1, keepdims=True)

        def count_eq_lt(thr, jcand):
            def body(c, acc):
                return acc + ((key_ref[0, c] == thr) & (kcol + c * PAGE < jcand)).astype(I32)
            acc = lax.fori_loop(0, nchunk, body, jnp.zeros((SROWS, PAGE), I32))
            return jnp.sum(acc, axis=1, keepdims=True)

        thr_ref[0] = _topk_threshold(count_ge, count_eq_lt, SROWS, n_sel, idx_bits, jl_ref.at[0])


def _dsa_score_sample(pt, q, w, kn, kc, n_sel):
    B, NP = pt.shape
    G = _pages_per_step(NP)
    L = (NP + 1) * PAGE
    page = lambda s: pl.BlockSpec((None, D_IDX, PAGE),
                                  lambda b, j, pt: (pt[b, jnp.minimum(j * G + s, NP - 1)], 0, 0))
    grid_spec = pltpu.PrefetchScalarGridSpec(
        num_scalar_prefetch=1,
        grid=(B, NP // G + 1),
        in_specs=([pl.BlockSpec((1, H_IDX * SROWS, D_IDX), lambda b, j, pt: (b, 0, 0)),
                   pl.BlockSpec((1, H_IDX * SROWS, 1), lambda b, j, pt: (b, 0, 0)),
                   pl.BlockSpec((1, PAGE, D_IDX), lambda b, j, pt: (b, 0, 0))]
                  + [page(s) for s in range(G)]),
        out_specs=[pl.BlockSpec((1, NP + 1, SROWS, PAGE), lambda b, j, pt: (b, 0, 0, 0)),
                   pl.BlockSpec((1, SROWS, 1), lambda b, j, pt: (b, 0, 0)),
                   pl.BlockSpec((1, SROWS, 1), lambda b, j, pt: (b, 0, 0))])
    return pl.pallas_call(
        functools.partial(_dsa_score_sample_kernel, G=G, n_pages=NP, n_sel=n_sel,
                          idx_bits=max(1, (L - 1).bit_length())),
        grid_spec=grid_spec,
        out_shape=[jax.ShapeDtypeStruct((B, NP + 1, SROWS, PAGE), I32),
                   jax.ShapeDtypeStruct((B, SROWS, 1), I32),
                   jax.ShapeDtypeStruct((B, SROWS, 1), I32)],
        compiler_params=_cparams("parallel", "arbitrary"),
        name="dsa_score_sample",
    )(pt, q, w, kn, *([kc] * G))


_GR = (H_DSA // KV_DSA) * SROWS


def _dsa_att_sample_kernel(pt_ref, q_ref, key_ref, thr_ref, jl_ref, kn_ref, vn_ref, *rest, G, n_pages):
    kc_refs, vc_refs = rest[:G], rest[G:2 * G]
    o_ref, m_scr, l_scr, acc_scr = rest[2 * G:]
    j = pl.program_id(1)
    qrow = lax.broadcasted_iota(I32, (SROWS, PAGE), 0)
    kcol = lax.broadcasted_iota(I32, (SROWS, PAGE), 1)
    nrep = H_DSA // KV_DSA

    @pl.when(j == 0)
    def _():
        m_scr[...] = jnp.full(m_scr.shape, NEG_BIG, F32)
        l_scr[...] = jnp.zeros(l_scr.shape, F32)
        acc_scr[...] = jnp.zeros(acc_scr.shape, F32)

    def attend(z, v, g, sel, v_transposed=True):
        n = z.shape[1]
        z = jnp.where(sel[None], z.reshape(nrep, SROWS, n), NEG_BIG).reshape(_GR, n)
        m = m_scr[g][:, 0:1]
        l = l_scr[g][:, 0:1]
        m_new = jnp.maximum(m, jnp.max(z, axis=1, keepdims=True))
        alpha = jnp.exp(m - m_new)
        p = jnp.exp(z - m_new)
        l = alpha * l + jnp.sum(p, axis=1, keepdims=True)
        pv = _dot_nt(p.astype(BF16), v.astype(BF16)) if v_transposed else _dot(p.astype(BF16), v.astype(BF16))
        acc_scr[g] = alpha * acc_scr[g] + pv
        m_scr[g] = jnp.broadcast_to(m_new, (_GR, LANES))
        l_scr[g] = jnp.broadcast_to(l, (_GR, LANES))

    def qgroup(g):
        return (q_ref[0, g] * (D_HEAD ** -0.5)).astype(BF16)

    thr = thr_ref[0]
    jl = jl_ref[0]

    @pl.when(j < n_pages // G)
    def _():
        sels = []
        for s in range(G):
            kblk = key_ref[0, j * G + s]
            sels.append((kblk > thr) | ((kblk == thr) & (kcol + (j * G + s) * PAGE <= jl)))
        sel = jnp.concatenate(sels, axis=1)
        for g in range(KV_DSA):
            kt = jnp.concatenate([kc_refs[s][g] for s in range(G)], axis=1).astype(BF16)
            vt = jnp.concatenate([vc_refs[s][g] for s in range(G)], axis=1)
            attend(_dot(qgroup(g), kt), vt, g, sel)

    @pl.when(j == n_pages // G)
    def _():
        off = n_pages * PAGE
        kblk = key_ref[0, n_pages]
        sel = ((kblk > thr) | ((kblk == thr) & (kcol + off <= jl))) & (kcol <= qrow)
        for g in range(KV_DSA):
            kn = kn_ref[0, :, D_HEAD * g:D_HEAD * (g + 1)].astype(BF16)
            attend(_dot_nt(qgroup(g), kn), vn_ref[0, :, D_HEAD * g:D_HEAD * (g + 1)], g, sel, v_transposed=False)
        for g in range(KV_DSA):
            o_ref[0, g] = acc_scr[g] / l_scr[g][:, 0:1]


def _dsa_att_sample(pt, q, keys, thr, jl, kn, vn, kc, vc):
    B, NP = pt.shape
    G = _pages_per_step(NP)
    page = lambda s: pl.BlockSpec((None, KV_DSA, D_HEAD, PAGE),
                                  lambda b, j, pt: (pt[b, jnp.minimum(j * G + s, NP - 1)], 0, 0, 0))
    per_b3 = lambda b, j, pt: (b, 0, 0)
    grid_spec = pltpu.PrefetchScalarGridSpec(
        num_scalar_prefetch=1,
        grid=(B, NP // G + 1),
        in_specs=([pl.BlockSpec((1, KV_DSA, _GR, D_HEAD), lambda b, j, pt: (b, 0, 0, 0)),
                   pl.BlockSpec((1, NP + 1, SROWS, PAGE), lambda b, j, pt: (b, 0, 0, 0)),
                   pl.BlockSpec((1, SROWS, 1), per_b3),
                   pl.BlockSpec((1, SROWS, 1), per_b3),
                   pl.BlockSpec((1, PAGE, LANES), per_b3),
                   pl.BlockSpec((1, PAGE, LANES), per_b3)]
                  + [page(s) for s in range(G)] * 2),
        out_specs=pl.BlockSpec((1, KV_DSA, _GR, D_HEAD), lambda b, j, pt: (b, 0, 0, 0)),
        scratch_shapes=[pltpu.VMEM((KV_DSA, _GR, LANES), F32), pltpu.VMEM((KV_DSA, _GR, LANES), F32),
                        pltpu.VMEM((KV_DSA, _GR, D_HEAD), F32)])
    return pl.pallas_call(
        functools.partial(_dsa_att_sample_kernel, G=G, n_pages=NP),
        grid_spec=grid_spec,
        out_shape=jax.ShapeDtypeStruct((B, KV_DSA, _GR, D_HEAD), F32),
        compiler_params=_cparams("parallel", "arbitrary"),
        name="dsa_att_sample",
    )(pt, q, keys, thr, jl, kn, vn, *([kc] * G), *([vc] * G))


def _mm_norm_res_kernel(*refs, n):
    a_refs, w_refs = refs[:n], refs[n:2 * n]
    g_ref, x_ref, o_ref = refs[2 * n:]
    acc = _dot(a_refs[0][...].astype(BF16), w_refs[0][...])
    for a_ref, w_ref in zip(a_refs[1:], w_refs[1:]):
        acc = acc + _dot(a_ref[...].astype(BF16), w_ref[...])
    o_ref[...] = x_ref[...] + _rms(acc, g_ref[...])


def _mm_norm_res(a_list, w_list, g, x, tm, name):
    M = x.shape[0]
    n = len(a_list)
    row = lambda i: (i, 0)
    const = lambda i: (0, 0)
    return pl.pallas_call(
        functools.partial(_mm_norm_res_kernel, n=n),
        grid=(M // tm,),
        in_specs=([pl.BlockSpec((tm, a.shape[1]), row) for a in a_list]
                  + [pl.BlockSpec(w.shape, const) for w in w_list]
                  + [pl.BlockSpec((1, D_MODEL), const), pl.BlockSpec((tm, D_MODEL), row)]),
        out_specs=pl.BlockSpec((tm, D_MODEL), row),
        out_shape=jax.ShapeDtypeStruct((M, D_MODEL), F32),
        compiler_params=_cparams("parallel"),
        name=name,
    )(*a_list, *w_list, g, x)


_GELU_C = math.sqrt(2.0 / math.pi)


def _gelu_tanh(x):
    return x * (0.5 * (1.0 + jnp.tanh(_GELU_C * (x + 0.044715 * (x * x * x)))))


def _ffn_up_kernel(x_ref, g_ref, wg_ref, wv_ref, cwg_ref, cwv_ref, cbg_ref, cbv_ref, pg_ref, pv_ref,
                   act_ref, og_ref, ov_ref, eg_scr, ev_scr, *, tm, T, streamed):
    i = pl.program_id(1)
    h = _rms(x_ref[...], g_ref[...]).astype(BF16)
    trow = lax.broadcasted_iota(I32, (tm, 1), 0) % T

    def conv(w_ref, cw_ref, cb_ref, p_ref, o_ref, e_scr):
        u = _dot(h, w_ref[...])
        e_scr[8:8 + tm, :] = u
        if streamed:
            @pl.when((i * tm) % T == 0)
            def _():
                e_scr[6:8, :] = p_ref[...]
            o_ref[...] = u[tm - 2:tm, :]
            u1 = e_scr[7:7 + tm, :]
            u2 = e_scr[6:6 + tm, :]
            e_scr[6:8, :] = u[tm - 2:tm, :]
        else:
            e_scr[0:8, :] = jnp.zeros((8, u.shape[1]), F32)
            o_ref[...] = u
            u1 = jnp.where(trow >= 1, e_scr[7:7 + tm, :], p_ref[0])
            u2 = jnp.where(trow >= 2, e_scr[6:6 + tm, :], p_ref[1])
        return cb_ref[...] + cw_ref[0:1, :] * u2 + cw_ref[1:2, :] * u1 + cw_ref[2:3, :] * u

    gate = conv(wg_ref, cwg_ref, cbg_ref, pg_ref, og_ref, eg_scr)
    val = conv(wv_ref, cwv_ref, cbv_ref, pv_ref, ov_ref, ev_scr)
    act_ref[...] = (_gelu_tanh(gate) * val).astype(BF16)


def _ffn_up(x, g, w_up, conv_w, conv_b, prev, T, tm, tn):
    M = x.shape[0]
    nj = D_FF // tn
    streamed = T >= tm
    tpb = max(T // tm, 1)
    if streamed:
        p_spec = lambda off: pl.BlockSpec((None, 2, tn), lambda j, i: (i // tpb, 0, j + off))
        o_shape = jax.ShapeDtypeStruct((M // T, 2, D_FF), F32)
        o_spec = pl.BlockSpec((None, 2, tn), lambda j, i: (i // tpb, 0, j))
    else:
        p_spec = lambda off: pl.BlockSpec((2, tm, tn), lambda j, i: (0, i, j + off))
        o_shape = jax.ShapeDtypeStruct((M, D_FF), F32)
        o_spec = pl.BlockSpec((tm, tn), lambda j, i: (i, j))
    col = lambda rows, off: pl.BlockSpec((rows, tn), lambda j, i: (0, j + off))
    return pl.pallas_call(
        functools.partial(_ffn_up_kernel, tm=tm, T=T, streamed=streamed),
        grid=(nj, M // tm),
        in_specs=[pl.BlockSpec((tm, D_MODEL), lambda j, i: (i, 0)),
                  pl.BlockSpec((1, D_MODEL), lambda j, i: (0, 0)),
                  col(D_MODEL, 0), col(D_MODEL, nj), col(3, 0), col(3, nj), col(1, 0), col(1, nj),
                  p_spec(0), p_spec(nj)],
        out_specs=[pl.BlockSpec((tm, tn), lambda j, i: (i, j)), o_spec, o_spec],
        out_shape=[jax.ShapeDtypeStruct((M, D_FF), BF16), o_shape, o_shape],
        scratch_shapes=[pltpu.VMEM((tm + 8, tn), F32), pltpu.VMEM((tm + 8, tn), F32)],
        compiler_params=_cparams("parallel", "arbitrary"),
        name="ffn_up",
    )(x, g, w_up, w_up, conv_w, conv_w, conv_b, conv_b, prev, prev)


def _rk_pre_kernel(x_ref, g_ref, sp_ref, mix_ref, wr_ref, wk_ref, wv_ref, w1_ref, w2_ref, a1_ref, a2_ref,
                   g1_ref, g2_ref, vec_ref,
                   hn_ref, r_ref, wl_ref, kf_ref, v_ref, av_ref, bv_ref, gg_ref, e_scr, *, tm, T, streamed):
    i = pl.program_id(0)
    h = _rms(x_ref[...], g_ref[...])
    hn_ref[...] = h
    e_scr[8:8 + tm, :] = h
    if streamed:
        @pl.when((i * tm) % T == 0)
        def _():
            e_scr[7:8, :] = sp_ref[...]
        xp = e_scr[7:7 + tm, :]
        e_scr[7:8, :] = h[tm - 1:tm, :]
    else:
        e_scr[0:8, :] = jnp.zeros((8, D_MODEL), F32)
        trow = lax.broadcasted_iota(I32, (tm, 1), 0) % T
        xp = jnp.where(trow >= 1, e_scr[7:7 + tm, :], sp_ref[...])
    xx = xp - h
    mixed = lambda j: (h + xx * mix_ref[j:j + 1, :]).astype(BF16)
    w0, a0, k_k, k_a = vec_ref[0:1, :], vec_ref[1:2, :], vec_ref[2:3, :], vec_ref[3:4, :]
    r = _dot(mixed(0), wr_ref[...])
    k = _dot(mixed(2), wk_ref[...])
    v = _dot(mixed(3), wv_ref[...])
    lw = w0 + _dot(jnp.tanh(_dot(mixed(1), w1_ref[...])).astype(BF16), w2_ref[...])
    w_log = -(jnp.maximum(-lw, 0.0) + jnp.log(1.0 + jnp.exp(-jnp.abs(lw)))) - 0.5
    wl_ref[...] = -jnp.exp(w_log)
    a = jax.nn.sigmoid(a0 + _dot(_dot(mixed(4), a1_ref[...]).astype(BF16), a2_ref[...]))
    gg_ref[...] = _dot(jax.nn.sigmoid(_dot(mixed(5), g1_ref[...])).astype(BF16), g2_ref[...])
    kk = k * k_k
    bd = _seg_ones()
    kk = kk / jnp.maximum(jnp.sqrt(_seg64_sum(kk * kk, bd)), 1e-12)
    r_ref[...] = r
    v_ref[...] = v
    kf_ref[...] = k * (1.0 + (a - 1.0) * k_a)
    av_ref[...] = -kk
    bv_ref[...] = kk * a


def _rk_pre(x, g, sp, mix, wr, wk, wv, w1, w2, a1, a2, g1, g2, vecs, T, tm):
    M = x.shape[0]
    streamed = T >= tm
    tpb = max(T // tm, 1)
    row = lambda i: (i, 0)
    const = lambda i: (0, 0)
    sp_spec = (pl.BlockSpec((None, 1, D_MODEL), lambda i: (i // tpb, 0, 0)) if streamed
               else pl.BlockSpec((tm, D_MODEL), row))
    full = lambda a: pl.BlockSpec(a.shape, const)
    return pl.pallas_call(
        functools.partial(_rk_pre_kernel, tm=tm, T=T, streamed=streamed),
        grid=(M // tm,),
        in_specs=[pl.BlockSpec((tm, D_MODEL), row), pl.BlockSpec((1, D_MODEL), const), sp_spec,
                  full(mix), full(wr), full(wk), full(wv), full(w1), full(w2), full(a1), full(a2),
                  full(g1), full(g2), full(vecs)],
        out_specs=[pl.BlockSpec((tm, D_MODEL), row)] * 8,
        out_shape=[jax.ShapeDtypeStruct((M, D_MODEL), F32)] * 8,
        scratch_shapes=[pltpu.VMEM((tm + 8, D_MODEL), F32)],
        compiler_params=_cparams("arbitrary"),
        name="rk_pre",
    )(x, g, sp, mix, wr, wk, wv, w1, w2, a1, a2, g1, g2, vecs)


def _bdot(a, b):
    return _dot(a.astype(BF16), b.astype(BF16))


def _rk_scan_kernel(r_ref, wl_ref, k_ref, v_ref, a_ref, b_ref, s0_ref, y_ref, sf_ref, s_scr, *, C, npair):
    c = pl.program_id(2)

    @pl.when(c == 0)
    def _():
        s_scr[...] = s0_ref[0]

    lane = lax.broadcasted_iota(I32, (1, LANES), 1)
    m0 = lane < 64
    rr = lax.broadcasted_iota(I32, (C, C), 0)
    cc = lax.broadcasted_iota(I32, (C, C), 1)
    tri_incl = (cc <= rr).astype(BF16)
    r2 = lax.broadcasted_iota(I32, (2 * C, 2 * C), 0)
    c2 = lax.broadcasted_iota(I32, (2 * C, 2 * C), 1)
    strict = (r2 % C) > (c2 % C)
    incl = (r2 % C) >= (c2 % C)
    eye = (r2 == c2).astype(F32)
    n_double = max(int(math.log2(C)) - 1, 0)

    def stack2(z):
        return jnp.concatenate([jnp.where(m0, z, 0.0), jnp.where(m0, 0.0, z)], axis=0)

    pairs = range(npair)
    sls = [slice(LANES * p, LANES * (p + 1)) for p in pairs]
    wl = [wl_ref[:, sl] for sl in sls]
    ld = []
    for p in pairs:
        wl_hi, wl_lo = _split(wl[p])
        ld.append(_dot(tri_incl, wl_hi) + _dot(tri_incl, wl_lo))
    dfull = [jnp.exp(x) for x in ld]
    dinv = [jnp.exp(-x) for x in ld]
    As = [stack2(a_ref[:, sls[p]] * jnp.exp(ld[p] - wl[p])).astype(BF16) for p in pairs]
    Bs = [stack2(b_ref[:, sls[p]] * dinv[p]) for p in pairs]
    Ks = [stack2(k_ref[:, sls[p]] * dinv[p]) for p in pairs]
    Rs = [stack2(r_ref[:, sls[p]] * dfull[p]).astype(BF16) for p in pairs]
    Vs = [stack2(v_ref[:, sls[p]]) for p in pairs]
    Vb = [x.astype(BF16) for x in Vs]
    BK = [jnp.concatenate([Bs[p], Ks[p]], axis=0).astype(BF16) for p in pairs]
    G1 = [_dot_nt(As[p], BK[p]) for p in pairs]
    G2 = [_dot_nt(Rs[p], BK[p]) for p in pairs]
    Lab = [jnp.where(strict, g[:, :2 * C], 0.0) for g in G1]
    Lak = [jnp.where(strict, g[:, 2 * C:], 0.0).astype(BF16) for g in G1]
    Mrb = [jnp.where(incl, g[:, :2 * C], 0.0).astype(BF16) for g in G2]
    Mrk = [jnp.where(incl, g[:, 2 * C:], 0.0).astype(BF16) for g in G2]
    Tm = [eye + x for x in Lab]
    P = Lab
    for _ in range(n_double):
        Pb = [x.astype(BF16) for x in P]
        P = [_dot(x, x) for x in Pb]
        Tm = [Tm[p] + _bdot(Tm[p], P[p]) for p in pairs]
    S = [s_scr[p] for p in pairs]
    Sb = [x.astype(BF16) for x in S]
    rhs = [_dot_nt(As[p], Sb[p]) + _dot(Lak[p], Vb[p]) for p in pairs]
    U = [_bdot(Tm[p], rhs[p]) for p in pairs]
    Ub = [x.astype(BF16) for x in U]
    Ys = [_dot_nt(Rs[p], Sb[p]) + _dot(Mrb[p], Ub[p]) + _dot(Mrk[p], Vb[p]) for p in pairs]
    upd = [_dot(U[p].T.astype(BF16), Bs[p].astype(BF16)) + _dot(Vs[p].T.astype(BF16), Ks[p].astype(BF16))
           for p in pairs]
    for p in pairs:
        y_ref[:, sls[p]] = Ys[p][:C] + Ys[p][C:]
        s_scr[p] = (S[p] + upd[p]) * dfull[p][C - 1:C, :]

    @pl.when(c == pl.num_programs(2) - 1)
    def _():
        sf_ref[0] = s_scr[...]


def _rk_scan(r, wl, k, v, a, b, s0, B, T, C, npair):
    nc = T // C
    ng = 8 // npair
    blk = pl.BlockSpec((C, LANES * npair), lambda bb, g, c: (bb * nc + c, g))
    s_spec = pl.BlockSpec((1, npair, LANES, LANES), lambda bb, g, c: (bb, g, 0, 0))
    return pl.pallas_call(
        functools.partial(_rk_scan_kernel, C=C, npair=npair),
        grid=(B, ng, nc),
        in_specs=[blk] * 6 + [s_spec],
        out_specs=[blk, s_spec],
        out_shape=[jax.ShapeDtypeStruct(r.shape, F32), jax.ShapeDtypeStruct(s0.shape, F32)],
        scratch_shapes=[pltpu.VMEM((npair, LANES, LANES), F32)],
        compiler_params=_cparams("parallel", "parallel", "arbitrary"),
        name="rk_scan",
    )(r, wl, k, v, a, b, s0)


def _rk_post_kernel(y_ref, r_ref, kf_ref, v_ref, gg_ref, vec_ref, o_ref):
    bd = _seg_ones()
    ln_w, ln_b, r_k = vec_ref[0:1, :], vec_ref[1:2, :], vec_ref[2:3, :]
    y = y_ref[...]
    mu = _seg64_sum(y, bd) * (1.0 / RK_N)
    d = y - mu
    var = _seg64_sum(d * d, bd) * (1.0 / RK_N)
    yn = d * lax.rsqrt(var + RK_GN_EPS) * ln_w + ln_b
    bonus = _seg64_sum(r_ref[...] * kf_ref[...] * r_k, bd) * v_ref[...]
    o_ref[...] = ((yn + bonus) * gg_ref[...]).astype(BF16)


def _rk_post(y, r, kf, v, gg, vecs, tm):
    M = y.shape[0]
    row = lambda i: (i, 0)
    return pl.pallas_call(
        _rk_post_kernel,
        grid=(M // tm,),
        in_specs=[pl.BlockSpec((tm, D_MODEL), row)] * 5 + [pl.BlockSpec(vecs.shape, lambda i: (0, 0))],
        out_specs=pl.BlockSpec((tm, D_MODEL), row),
        out_shape=jax.ShapeDtypeStruct((M, D_MODEL), BF16),
        compiler_params=_cparams("parallel"),
        name="rk_post",
    )(y, r, kf, v, gg, vecs)


def _row_tile(M):
    return min(256, M)


def _pad_rows(a, n):
    return jnp.pad(a, ((0, 0), (0, n - a.shape[1])) + ((0, 0),) * (a.ndim - 2))


def _attn_layer_prompt(x, B, T, g_pre, w_in, w_outs, g_post):
    M = B * T
    tm = _row_tile(M)
    cos, sin = _rope_tables(jnp.arange(T, dtype=I32))
    qa, ka, va, qb, kb, vb, qi, ki2, wi = _attn_proj(x, g_pre, w_in, cos, sin, tm)
    n_sel = min(DSA_TOPK, T // 4)
    oa = _sb_prompt(qa, ka, va, B, T, min(256, T))
    ob = _dsa_prompt(qi, wi, ki2, qb, kb, vb, B, T, min(128, T), n_sel)
    x = _mm_norm_res([oa, ob], w_outs, g_post, x, tm, "attn_out")
    rows = (ka.reshape(1, B, T, H_SB, D_HEAD), va.reshape(1, B, T, H_SB, D_HEAD),
            kb.reshape(1, B, T, KV_DSA, D_HEAD), vb.reshape(1, B, T, KV_DSA, D_HEAD),
            ki2[:, :D_IDX].reshape(1, B, T, D_IDX))
    return x, rows


def _attn_layer_sample(x, B, T, P, page_table, caches, g_pre, w_in, w_outs, g_post):
    M = B * T
    cos, sin = _rope_tables(P + jnp.arange(T, dtype=I32))
    cos, sin = jnp.tile(cos, (B, 1)), jnp.tile(sin, (B, 1))
    qa, ka, va, qb, kb, vb, qi, ki2, wi = _attn_proj(x, g_pre, w_in, cos, sin, M)
    c_sb_k, c_sb_v, c_dsa_k, c_dsa_v, c_idx = caches
    n_pool = c_sb_k.shape[0]
    n_sel = min(DSA_TOPK, (P + T) // 4)
    b3 = lambda a: a.reshape(B, T, a.shape[-1])

    row_minor = lambda c: jnp.moveaxis(c, 1, -1)
    oa = _sb_sample(page_table, _pad_rows(b3(qa), SROWS), _pad_rows(b3(ka), PAGE), _pad_rows(b3(va), PAGE),
                    row_minor(c_sb_k), row_minor(c_sb_v))
    oa = oa[:, :T].reshape(M, H_SB * D_HEAD)

    qi_s = _pad_rows(b3(qi).reshape(B, T, H_IDX, D_IDX), SROWS).transpose(0, 2, 1, 3).reshape(B, H_IDX * SROWS, D_IDX)
    wi_s = _pad_rows(b3(wi)[:, :, :H_IDX], SROWS).transpose(0, 2, 1).reshape(B, H_IDX * SROWS, 1)
    keys, thr, jl = _dsa_score_sample(page_table, qi_s, wi_s, _pad_rows(b3(ki2)[:, :, :D_IDX], PAGE),
                                      row_minor(c_idx), n_sel)
    qb_s = _pad_rows(b3(qb).reshape(B, T, H_DSA, D_HEAD), SROWS).transpose(0, 2, 1, 3).reshape(B, KV_DSA, _GR, D_HEAD)
    ob = _dsa_att_sample(page_table, qb_s, keys, thr, jl, _pad_rows(b3(kb), PAGE), _pad_rows(b3(vb), PAGE),
                         row_minor(c_dsa_k), row_minor(c_dsa_v))
    ob = ob.reshape(B, H_DSA, SROWS, D_HEAD)[:, :, :T].transpose(0, 2, 1, 3).reshape(M, H_DSA * D_HEAD)

    x = _mm_norm_res([oa, ob], w_outs, g_post, x, M, "attn_out")
    rows = (ka.reshape(1, B, T, H_SB, D_HEAD), va.reshape(1, B, T, H_SB, D_HEAD),
            kb.reshape(1, B, T, KV_DSA, D_HEAD), vb.reshape(1, B, T, KV_DSA, D_HEAD),
            ki2[:, :D_IDX].reshape(1, B, T, D_IDX))
    return x, rows


def _pair_state(S):
    B = S.shape[0]
    S = S.reshape(B, 8, 2, RK_N, RK_N)
    z = jnp.zeros_like(S[:, :, 0])
    top = jnp.concatenate([S[:, :, 0], z], axis=-1)
    bot = jnp.concatenate([z, S[:, :, 1]], axis=-1)
    return jnp.concatenate([top, bot], axis=-2)


def _unpair_state(Sp):
    B = Sp.shape[0]
    return jnp.stack([Sp[:, :, :RK_N, :RK_N], Sp[:, :, RK_N:, RK_N:]], axis=2).reshape(B, RK_H, RK_N, RK_N)


RK_CHUNK = 64
RK_PAIRS = 8


def _rwkv_layer(x, B, T, shift_prev, S0, g_pre, prm, g_post):
    M = B * T
    tm = _row_tile(M)
    (mix, wr, wk, wv, wo, w1, w2, a1, a2, g1, g2, vec_pre, vec_post) = prm
    if T >= tm:
        sp = shift_prev.reshape(B, 1, D_MODEL)
    else:
        sp = jnp.repeat(shift_prev, T, axis=0)
    hn, r, wl, kf, v, av, bv, gg = _rk_pre(x, g_pre, sp, mix, wr, wk, wv, w1, w2, a1, a2, g1, g2, vec_pre, T, tm)
    Tp = -(-T // RK_CHUNK) * RK_CHUNK
    if Tp != T:
        padt = lambda a: _pad_rows(a.reshape(B, T, D_MODEL), Tp).reshape(B * Tp, D_MODEL)
        y, Sf = _rk_scan(padt(r), padt(wl), padt(kf), padt(v), padt(av), padt(bv), _pair_state(S0), B, Tp, RK_CHUNK, RK_PAIRS)
        y = y.reshape(B, Tp, D_MODEL)[:, :T].reshape(M, D_MODEL)
    else:
        y, Sf = _rk_scan(r, wl, kf, v, av, bv, _pair_state(S0), B, T, RK_CHUNK, RK_PAIRS)
    z = _rk_post(y, r, kf, v, gg, vec_post, tm)
    x = _mm_norm_res([z], [wo], g_post, x, tm, "rk_out")
    shift = hn.reshape(B, T, D_MODEL)[:, -1]
    return x, shift, _unpair_state(Sf)


def _ffn_layer(x, B, T, prev, g_pre, w_up, conv_w, conv_b, w_down, g_post):
    M = B * T
    tm = _row_tile(M)
    tn = D_FF // 2
    if T >= tm:
        act, cg, cv = _ffn_up(x, g_pre, w_up, conv_w, conv_b, prev, T, tm, tn)
        conv_state = jnp.concatenate([cg, cv], axis=-1)
    else:
        zeros = jnp.zeros((B, T, 2 * D_FF), F32)
        p1 = zeros.at[:, 0].set(prev[:, 1])
        p2 = zeros.at[:, 0].set(prev[:, 0]).at[:, 1].set(prev[:, 1])
        pp = jnp.stack([p1.reshape(M, -1), p2.reshape(M, -1)])
        act, ug, uv = _ffn_up(x, g_pre, w_up, conv_w, conv_b, pp, T, tm, tn)
        u = jnp.concatenate([ug, uv], axis=-1).reshape(B, T, 2 * D_FF)
        conv_state = jnp.concatenate([prev, u], axis=1)[:, -2:]
    x = _mm_norm_res([act], [w_down], g_post, x, tm, "ffn_down")
    return x, conv_state


def _forward(x_prompt, x_sample, cache_sb_k, cache_sb_v, cache_dsa_k, cache_dsa_v, cache_idx_k, page_table,
             state_wkv, state_shift, state_ffn_conv,
             norm_mix_pre, norm_mix_post, norm_ffn_pre, norm_ffn_post,
             att_w_in, att_w_out,
             rk_mix, rk_w_r, rk_w_k, rk_w_v, rk_w_o, rk_w0, rk_w1, rk_w2, rk_a0, rk_a1, rk_a2,
             rk_g1, rk_g2, rk_k_k, rk_k_a, rk_r_k, rk_ln_w, rk_ln_b,
             ffn_w_up, ffn_conv_w, ffn_conv_b, ffn_w_down):
    Bp, Tp, D = x_prompt.shape
    Bs, Ts, _ = x_sample.shape
    P = page_table.shape[1] * PAGE
    depth = norm_mix_pre.shape[0]
    xp = x_prompt.reshape(Bp * Tp, D)
    xs = x_sample.reshape(Bs * Ts, D)
    vrow = lambda a: a.reshape(1, -1)
    bf = lambda a: a.astype(BF16)
    att_p, att_s, wkv_p, wkv_s, sh_p, sh_s, cv_p, cv_s = [], [], [], [], [], [], [], []
    for i in range(depth):
        li = i // 2
        g_pre, g_post = vrow(norm_mix_pre[i]), vrow(norm_mix_post[i])
        if i % 2 == 0:
            w_in = _pack_w_in(att_w_in[li])
            w_outs = [bf(att_w_out[li][:H_SB * D_HEAD]), bf(att_w_out[li][H_SB * D_HEAD:])]
            caches = (cache_sb_k[li], cache_sb_v[li], cache_dsa_k[li], cache_dsa_v[li], cache_idx_k[li])
            xp, rows_p = _attn_layer_prompt(xp, Bp, Tp, g_pre, w_in, w_outs, g_post)
            xs, rows_s = _attn_layer_sample(xs, Bs, Ts, P, page_table, caches, g_pre, w_in, w_outs, g_post)
            att_p.append(rows_p)
            att_s.append(rows_s)
        else:
            vec_pre = jnp.stack([rk_w0[li], rk_a0[li], rk_k_k[li], rk_k_a[li]])
            vec_post = jnp.stack([rk_ln_w[li], rk_ln_b[li], rk_r_k[li].reshape(-1)])
            prm = (rk_mix[li], bf(rk_w_r[li]), bf(rk_w_k[li]), bf(rk_w_v[li]), bf(rk_w_o[li]),
                   bf(rk_w1[li]), bf(rk_w2[li]), bf(rk_a1[li]), bf(rk_a2[li]), bf(rk_g1[li]), bf(rk_g2[li]),
                   vec_pre, vec_post)
            xp, shp, Sp = _rwkv_layer(xp, Bp, Tp, jnp.zeros((Bp, D), F32), jnp.zeros((Bp, RK_H, RK_N, RK_N), F32),
                                      g_pre, prm, g_post)
            xs, shs, Ss = _rwkv_layer(xs, Bs, Ts, state_shift[li], state_wkv[li], g_pre, prm, g_post)
            wkv_p.append(Sp)
            wkv_s.append(Ss)
            sh_p.append(shp)
            sh_s.append(shs)
        f_pre, f_post = vrow(norm_ffn_pre[i]), vrow(norm_ffn_post[i])
        w_up, w_down = bf(ffn_w_up[i]), bf(ffn_w_down[i])
        cb = vrow(ffn_conv_b[i])
        xp, cp = _ffn_layer(xp, Bp, Tp, jnp.zeros((Bp, 2, 2 * D_FF), F32), f_pre, w_up, ffn_conv_w[i], cb, w_down, f_post)
        xs, cs = _ffn_layer(xs, Bs, Ts, state_ffn_conv[i], f_pre, w_up, ffn_conv_w[i], cb, w_down, f_post)
        cv_p.append(cp)
        cv_s.append(cs)
    cat = lambda rows, j: jnp.concatenate([r[j] for r in rows], axis=0)
    return (xp.reshape(Bp, Tp, D), xs.reshape(Bs, Ts, D),
            cat(att_p, 0), cat(att_s, 0), cat(att_p, 1), cat(att_s, 1),
            cat(att_p, 2), cat(att_s, 2), cat(att_p, 3), cat(att_s, 3),
            cat(att_p, 4), cat(att_s, 4),
            jnp.stack(wkv_p), jnp.stack(wkv_s), jnp.stack(sh_p), jnp.stack(sh_s),
            jnp.stack(cv_p), jnp.stack(cv_s))


def kernel(x_prompt, x_sample, cache_sb_k, cache_sb_v, cache_dsa_k, cache_dsa_v, cache_idx_k, page_table, state_wkv, state_shift, state_ffn_conv, norm_mix_pre, norm_mix_post, norm_ffn_pre, norm_ffn_post, att_w_in, att_w_out, rk_mix, rk_w_r, rk_w_k, rk_w_v, rk_w_o, rk_w0, rk_w1, rk_w2, rk_a0, rk_a1, rk_a2, rk_g1, rk_g2, rk_k_k, rk_k_a, rk_r_k, rk_ln_w, rk_ln_b, ffn_w_up, ffn_conv_w, ffn_conv_b, ffn_w_down):
    return _forward(x_prompt, x_sample, cache_sb_k, cache_sb_v, cache_dsa_k, cache_dsa_v, cache_idx_k, page_table,
                    state_wkv, state_shift, state_ffn_conv,
                    norm_mix_pre, norm_mix_post, norm_ffn_pre, norm_ffn_post,
                    att_w_in, att_w_out,
                    rk_mix, rk_w_r, rk_w_k, rk_w_v, rk_w_o, rk_w0, rk_w1, rk_w2, rk_a0, rk_a1, rk_a2,
                    rk_g1, rk_g2, rk_k_k, rk_k_a, rk_r_k, rk_ln_w, rk_ln_b,
                    ffn_w_up, ffn_conv_w, ffn_conv_b, ffn_w_down)
```

```python
import functools
import math

import jax
import jax.numpy as jnp
from jax import lax
from jax.experimental import pallas as pl
from jax.experimental.pallas import tpu as pltpu

F32 = jnp.float32
BF16 = jnp.bfloat16
I32 = jnp.int32

D_MODEL = 1024
D_HEAD = 64
H_SB = 8
H_DSA = 8
KV_DSA = 2
H_IDX = 8
D_IDX = 64
DSA_TOPK = 256
PAGE = 128
ROPE_THETA = 10000.0
RK_N = 64
RK_H = D_MODEL // RK_N
RK_GN_EPS = 64e-5
D_FF = 2816
NORM_EPS = 1e-6

LANES = 128
VMEM_LIMIT = 56 * 1024 * 1024
INT_MIN = -2147483648
INT_MAX = 2147483647
NEG_BIG = -1e30

_C_QA, _C_KA, _C_VA, _C_QB, _C_KB, _C_VB, _C_QI, _C_KI, _C_WI, _C_END = (
    0, 512, 1024, 1536, 2048, 2176, 2304, 2816, 2944, 3072)


def _cparams(*sem):
    return pltpu.CompilerParams(dimension_semantics=sem, vmem_limit_bytes=VMEM_LIMIT)


def _dot(a, b):
    return jnp.dot(a, b, preferred_element_type=F32)


def _dot_nt(a, b):
    return lax.dot_general(a, b, (((1,), (1,)), ((), ())), preferred_element_type=F32)


def _dot_tn(a, b):
    return lax.dot_general(a, b, (((0,), (0,)), ((), ())), preferred_element_type=F32)


def _split(x):
    hi = x.astype(BF16)
    lo = (x - hi.astype(F32)).astype(BF16)
    return hi, lo


def _dot_x2(x, m_bf16):
    hi, lo = _split(x)
    return _dot(hi, m_bf16) + _dot(lo, m_bf16)


def _rms(x, g):
    return x * lax.rsqrt(jnp.mean(x * x, axis=-1, keepdims=True) + NORM_EPS) * g


def _seg_ones():
    r = lax.broadcasted_iota(I32, (LANES, LANES), 0)
    c = lax.broadcasted_iota(I32, (LANES, LANES), 1)
    return ((r < 64) == (c < 64)).astype(BF16)


def _seg64_sum(x, bd):
    n = x.shape[1] // LANES
    return jnp.concatenate([_dot_x2(x[:, LANES * m:LANES * (m + 1)], bd) for m in range(n)], axis=1)


def _attn_proj_kernel(x_ref, g_ref, w_ref, cos_ref, sin_ref,
                      qa_ref, ka_ref, va_ref, qb_ref, kb_ref, vb_ref, qi_ref, ki_ref, wi_ref):
    h = _rms(x_ref[...], g_ref[...]).astype(BF16)
    cos = cos_ref[...]
    sin = sin_ref[...]
    lane = lax.broadcasted_iota(I32, (1, LANES), 1)
    first = (lane & 32) == 0

    def proj(c0, c1):
        return _dot(h, w_ref[:, c0:c1])

    def rope(blk):
        rot = jnp.where(first, pltpu.roll(blk, LANES - 32, 1), pltpu.roll(blk, 32, 1))
        return blk * cos + rot * sin

    qa_ref[...] = proj(_C_QA, _C_KA)
    ka_ref[...] = proj(_C_KA, _C_VA)
    va_ref[...] = proj(_C_VA, _C_QB)
    for m in range(4):
        qb_ref[:, LANES * m:LANES * (m + 1)] = rope(proj(_C_QB + LANES * m, _C_QB + LANES * (m + 1)))
        qi_ref[:, LANES * m:LANES * (m + 1)] = rope(proj(_C_QI + LANES * m, _C_QI + LANES * (m + 1)))
    kb_ref[...] = rope(proj(_C_KB, _C_VB))
    vb_ref[...] = proj(_C_VB, _C_QI)
    ki_ref[...] = rope(proj(_C_KI, _C_WI))
    wi_ref[...] = proj(_C_WI, _C_END)


def _attn_proj(x, g, w, cos, sin, tm):
    M = x.shape[0]
    nt = cos.shape[0] // tm
    widths = (512, 512, 512, 512, 128, 128, 512, 128, 128)
    row = lambda i: (i, 0)
    return pl.pallas_call(
        _attn_proj_kernel,
        grid=(M // tm,),
        in_specs=[pl.BlockSpec((tm, D_MODEL), row),
                  pl.BlockSpec((1, D_MODEL), lambda i: (0, 0)),
                  pl.BlockSpec((D_MODEL, _C_END), lambda i: (0, 0)),
                  pl.BlockSpec((tm, LANES), lambda i: (i % nt, 0)),
                  pl.BlockSpec((tm, LANES), lambda i: (i % nt, 0))],
        out_specs=[pl.BlockSpec((tm, wd), row) for wd in widths],
        out_shape=[jax.ShapeDtypeStruct((M, wd), F32) for wd in widths],
        compiler_params=_cparams("parallel"),
        name="attn_proj",
    )(x, g, w, cos, sin)


def _rope_tables(pos):
    half = D_HEAD // 2
    inv = ROPE_THETA ** (-2.0 * jnp.arange(half, dtype=F32) / D_HEAD)
    ang = pos.astype(F32)[:, None] * inv[None, :]
    cos = jnp.cos(ang)
    sin = jnp.sin(ang)
    return jnp.tile(cos, (1, 4)), jnp.tile(jnp.concatenate([-sin, sin], axis=1), (1, 2))


def _pack_w_in(w_in):
    ki = w_in[:, 2816:2880]
    wi = w_in[:, 2880:2888]
    pad = jnp.zeros((w_in.shape[0], LANES - H_IDX), w_in.dtype)
    return jnp.concatenate([w_in[:, :2816], ki, ki, wi, pad], axis=1).astype(BF16)


def _sb_block(z, c, tri, vb, causal):
    t = jnp.log(1.0 + jnp.exp(-jnp.abs(z)))
    lb = jnp.minimum(z, 0.0) - t
    lk = lb - z
    if causal is not None:
        lk = jnp.where(causal, lk, 0.0)
    aft = _dot(lk.astype(BF16), tri) + c
    w = jnp.exp(lb + aft)
    if causal is not None:
        w = jnp.where(causal, w, 0.0)
    pv = _dot(w.astype(BF16), vb)
    return c + jnp.sum(lk, axis=1, keepdims=True), pv


SB_BLOCKS = 2


def _sb_prompt_kernel(q_ref, k_ref, v_ref, o_ref, *, tq):
    i = pl.program_id(2)
    lane = lax.broadcasted_iota(I32, (1, LANES), 1)
    row = lax.broadcasted_iota(I32, (tq, tq), 0)
    col = lax.broadcasted_iota(I32, (tq, tq), 1)
    tri = (row > col).astype(BF16)
    causal = col < row
    q = q_ref[...] * (D_HEAD ** -0.5)
    qh = [jnp.where(lane < 64, q, 0.0).astype(BF16), jnp.where(lane >= 64, q, 0.0).astype(BF16)]

    def step(js, carry, mask):
        kbs, vbs = [], []
        for j in js:
            off = pl.multiple_of(j * tq, tq)
            kbs.append(k_ref[pl.ds(off, tq), :].astype(BF16))
            vbs.append(v_ref[pl.ds(off, tq), :].astype(BF16))
        tiles = [(h2, n) for n in range(len(js)) for h2 in range(2)]
        z = [_dot_nt(qh[h2], kbs[n]) for h2, n in tiles]
        t = [jnp.log(1.0 + jnp.exp(-jnp.abs(x))) for x in z]
        lb = [jnp.minimum(x, 0.0) - y for x, y in zip(z, t)]
        lk = [x - y for x, y in zip(lb, z)]
        if mask is not None:
            lk = [jnp.where(mask, x, 0.0) for x in lk]
        loc = [_dot(x.astype(BF16), tri) for x in lk]
        tot = [jnp.sum(x, axis=1, keepdims=True) for x in lk]
        c = [carry[0], carry[2]]
        w = []
        for idx, (h2, n) in enumerate(tiles):
            x = jnp.exp(lb[idx] + loc[idx] + c[h2])
            w.append((jnp.where(mask, x, 0.0) if mask is not None else x).astype(BF16))
            c[h2] = c[h2] + tot[idx]
        pv = [_dot(w[idx], vbs[n]) for idx, (h2, n) in enumerate(tiles)]
        acc = [carry[1], carry[3]]
        for idx, (h2, n) in enumerate(tiles):
            acc[h2] = acc[h2] + pv[idx]
        return c[0], acc[0], c[1], acc[1]

    zc, za = jnp.zeros((tq, 1), F32), jnp.zeros((tq, LANES), F32)
    carry = step([i], (zc, za, zc, za), causal)
    nb = SB_BLOCKS
    carry = lax.fori_loop(0, i // nb, lambda jj, cr: step([i - 1 - nb * jj - u for u in range(nb)], cr, None), carry)
    carry = lax.fori_loop(0, i % nb, lambda jj, cr: step([i % nb - 1 - jj], cr, None), carry)
    o_ref[...] = jnp.where(lane < 64, carry[1], carry[3])


def _sb_prompt(q, k, v, B, T, tq):
    nq = T // tq
    return pl.pallas_call(
        functools.partial(_sb_prompt_kernel, tq=tq),
        grid=(B, 4, nq),
        in_specs=[pl.BlockSpec((tq, LANES), lambda b, p, i: (b * nq + i, p)),
                  pl.BlockSpec((T, LANES), lambda b, p, i: (b, p)),
                  pl.BlockSpec((T, LANES), lambda b, p, i: (b, p))],
        out_specs=pl.BlockSpec((tq, LANES), lambda b, p, i: (b * nq + i, p)),
        out_shape=jax.ShapeDtypeStruct(q.shape, F32),
        compiler_params=_cparams("parallel", "parallel", "arbitrary"),
        name="sb_prompt",
    )(q, k, v)


SROWS = 16


def _pages_per_step(n_pages):
    for g in (16, 8, 4, 2):
        if n_pages % g == 0:
            return g
    return 1


def _sb_sample_kernel(pt_ref, q_ref, kn_ref, vn_ref, *rest, G):
    kc_refs, vc_refs = rest[:G], rest[G:2 * G]
    o_ref, c_scr, acc_scr = rest[2 * G:]
    j = pl.program_id(1)
    row = lax.broadcasted_iota(I32, (PAGE, PAGE), 0)
    col = lax.broadcasted_iota(I32, (PAGE, PAGE), 1)
    tri = (row > col).astype(BF16)
    qrow = lax.broadcasted_iota(I32, (SROWS, PAGE), 0)
    kcol = lax.broadcasted_iota(I32, (SROWS, PAGE), 1)
    causal = kcol < qrow
    scale = D_HEAD ** -0.5

    def qhead(h):
        return (q_ref[0, :, D_HEAD * h:D_HEAD * (h + 1)] * scale).astype(BF16)

    @pl.when(j == 0)
    def _():
        for h in range(H_SB):
            kb = kn_ref[0, :, D_HEAD * h:D_HEAD * (h + 1)].astype(BF16)
            vb = vn_ref[0, :, D_HEAD * h:D_HEAD * (h + 1)].astype(BF16)
            c, pv = _sb_block(_dot_nt(qhead(h), kb), jnp.zeros((SROWS, 1), F32), tri, vb, causal)
            c_scr[h] = jnp.broadcast_to(c, (SROWS, LANES))
            acc_scr[h] = pv

    order = range(G - 1, -1, -1)
    zs = []
    for h in range(H_SB):
        kt = jnp.concatenate([kc_refs[s][h] for s in order], axis=1).astype(BF16)
        z = _dot(qhead(h), kt)
        zs += [z[:, PAGE * p:PAGE * (p + 1)] for p in range(G)]
    z = jnp.concatenate(zs, axis=0)
    t = jnp.log(1.0 + jnp.exp(-jnp.abs(z)))
    lb = jnp.minimum(z, 0.0) - t
    lk = lb - z
    loc = _dot(lk.astype(BF16), tri)
    tot = jnp.sum(lk, axis=1, keepdims=True)
    cs = []
    for h in range(H_SB):
        run = c_scr[h][:, 0:1]
        per_page = [None] * G
        for p in range(G - 1, -1, -1):
            per_page[p] = run
            run = run + tot[(h * G + p) * SROWS:(h * G + p + 1) * SROWS]
        c_scr[h] = jnp.broadcast_to(run, (SROWS, LANES))
        cs += per_page
    w = jnp.exp(lb + loc + jnp.concatenate(cs, axis=0)).astype(BF16)
    for h in range(H_SB):
        wh = jnp.concatenate([w[(h * G + p) * SROWS:(h * G + p + 1) * SROWS] for p in range(G)], axis=1)
        vt = jnp.concatenate([vc_refs[s][h] for s in order], axis=1).astype(BF16)
        acc_scr[h] = acc_scr[h] + _dot_nt(wh, vt)

    @pl.when(j == pl.num_programs(1) - 1)
    def _():
        for h in range(H_SB):
            o_ref[0, :, D_HEAD * h:D_HEAD * (h + 1)] = acc_scr[h]


def _sb_sample(pt, q, kn, vn, kc, vc):
    B, NP = pt.shape
    G = _pages_per_step(NP)
    page = lambda s: pl.BlockSpec((None, H_SB, D_HEAD, PAGE),
                                  lambda b, j, pt: (pt[b, NP - 1 - (j * G + s)], 0, 0, 0))
    grid_spec = pltpu.PrefetchScalarGridSpec(
        num_scalar_prefetch=1,
        grid=(B, NP // G),
        in_specs=([pl.BlockSpec((1, SROWS, 512), lambda b, j, pt: (b, 0, 0)),
                   pl.BlockSpec((1, PAGE, 512), lambda b, j, pt: (b, 0, 0)),
                   pl.BlockSpec((1, PAGE, 512), lambda b, j, pt: (b, 0, 0))]
                  + [page(s) for s in range(G)] * 2),
        out_specs=pl.BlockSpec((1, SROWS, 512), lambda b, j, pt: (b, 0, 0)),
        scratch_shapes=[pltpu.VMEM((H_SB, SROWS, LANES), F32), pltpu.VMEM((H_SB, SROWS, D_HEAD), F32)])
    return pl.pallas_call(
        functools.partial(_sb_sample_kernel, G=G),
        grid_spec=grid_spec,
        out_shape=jax.ShapeDtypeStruct((B, SROWS, 512), F32),
        compiler_params=_cparams("parallel", "arbitrary"),
        name="sb_sample",
    )(pt, q, kn, vn, *([kc] * G), *([vc] * G))


_IDX_SCALE = D_IDX ** -0.5 * H_IDX ** -0.5


def _score_key(score):
    score = jnp.where(score == 0.0, 0.0, score)
    bits = pltpu.bitcast(score, I32)
    return bits ^ ((bits >> 31) & INT_MAX)


def _topk_threshold(count_ge, count_eq_lt, rows, real_rows, n_sel, idx_bits, jl_ref):
    def bit_body(it, tu):
        cand = tu | jnp.left_shift(jnp.int32(1), 31 - it)
        return jnp.where(count_ge(cand ^ INT_MIN) >= n_sel, cand, tu)

    tu = lax.fori_loop(0, 32, bit_body, jnp.zeros((rows, 1), I32))
    thr = tu ^ INT_MIN
    n_ge = count_ge(thr)
    n_gt = jnp.where(thr == INT_MAX, 0, count_ge(thr + 1))
    need = n_sel - n_gt
    tied = (n_ge > n_sel) & (thr != INT_MIN)
    if real_rows < rows:
        tied = tied & (lax.broadcasted_iota(I32, (rows, 1), 0) < real_rows)
    jl_ref[...] = jnp.full((rows, 1), INT_MAX, I32)

    @pl.when(jnp.max(tied.astype(I32)) > 0)
    def _():
        def idx_body(it, lo):
            cand = lo | jnp.left_shift(jnp.int32(1), idx_bits - 1 - it)
            return jnp.where(count_eq_lt(thr, cand) < need, cand, lo)
        lo = lax.fori_loop(0, idx_bits, idx_body, jnp.zeros((rows, 1), I32))
        jl_ref[...] = jnp.where(tied, lo, INT_MAX)

    return thr


def _dsa_prompt_kernel(qi_ref, wi_ref, ki_ref, qb_ref, kb_ref, vb_ref, o_ref, key_scr, jl_scr,
                       *, tq, n_sel, idx_bits, AG):
    i = pl.program_id(1)
    tk = tq
    lane = lax.broadcasted_iota(I32, (1, LANES), 1)
    lo_half = lane < 64
    row = lax.broadcasted_iota(I32, (tq, tk), 0)
    col = lax.broadcasted_iota(I32, (tq, tk), 1)
    diag_ok = col <= row

    qs = []
    for h in range(H_IDX):
        blk = qi_ref[:, LANES * (h // 2):LANES * (h // 2 + 1)]
        qs.append(jnp.where(lo_half if h % 2 == 0 else ~lo_half, blk, 0.0))
    qstack = jnp.concatenate(qs, axis=0).astype(BF16)
    wb = [jnp.broadcast_to(wi_ref[:, h:h + 1], (tq, tk)) for h in range(H_IDX)]

    def score_chunks(c0, n, mask):
        off = pl.multiple_of(c0 * tk, tk)
        kc = ki_ref[pl.ds(off, n * tk), :].astype(BF16)
        s = _dot_nt(qstack, kc)
        relu = [[jnp.maximum(s[h * tq:(h + 1) * tq, u * tk:(u + 1) * tk], 0.0) for h in range(H_IDX)]
                for u in range(n)]
        score = [wb[0] * r[0] for r in relu]
        for h in range(1, H_IDX):
            score = [sc + wb[h] * r[h] for sc, r in zip(score, relu)]
        for u in range(n):
            key = _score_key(score[u] * _IDX_SCALE)
            if mask is not None:
                key = jnp.where(mask, key, INT_MIN)
            key_scr[c0 + u] = key

    def score_group(g, carry):
        score_chunks(g * AG, AG, None)
        return carry

    def score_single(c, carry):
        score_chunks(c, 1, None)
        return carry

    lax.fori_loop(0, i // AG, score_group, 0)
    lax.fori_loop((i // AG) * AG, i, score_single, 0)
    score_chunks(i, 1, diag_ok)
    for u in range(1, AG):
        @pl.when((i % AG) + u < AG)
        def _():
            key_scr[i + u] = jnp.full((tq, tk), INT_MIN, I32)

    def count_ge(cand):
        def body(g, acc):
            for u in range(AG):
                acc = acc + (key_scr[g * AG + u] >= cand).astype(I32)
            return acc
        acc = lax.fori_loop(0, i // AG + 1, body, jnp.zeros((tq, tk), I32))
        return jnp.sum(acc, axis=1, keepdims=True)

    def count_eq_lt(thr, jcand):
        def body(g, acc):
            for u in range(AG):
                c = g * AG + u
                acc = acc + ((key_scr[c] == thr) & (col + c * tk < jcand)).astype(I32)
            return acc
        acc = lax.fori_loop(0, i // AG + 1, body, jnp.zeros((tq, tk), I32))
        return jnp.sum(acc, axis=1, keepdims=True)

    thr = _topk_threshold(count_ge, count_eq_lt, tq, tq, n_sel, idx_bits, jl_scr)
    jl = jl_scr[...]

    qs = []
    for h in range(H_DSA):
        blk = qb_ref[:, LANES * (h // 2):LANES * (h // 2 + 1)] * (D_HEAD ** -0.5)
        g = h // (H_DSA // KV_DSA)
        if (h % 2) != g:
            blk = pltpu.roll(blk, 64, 1)
        qs.append(jnp.where(lo_half if g == 0 else ~lo_half, blk, 0.0))
    qstack2 = jnp.concatenate(qs, axis=0).astype(BF16)

    ta = AG * tk
    rowa = lax.broadcasted_iota(I32, (tq, ta), 0)
    cola = lax.broadcasted_iota(I32, (tq, ta), 1)

    def att_group(c, carry, last):
        m, l, acc = carry
        off = pl.multiple_of(c * ta, ta)
        kblk = jnp.concatenate([key_scr[c * AG + u] for u in range(AG)], axis=1)
        sel = (kblk > thr) | ((kblk == thr) & (cola + off <= jl))
        if last:
            sel = sel & (cola + off <= rowa + i * tq)
        z = _dot_nt(qstack2, kb_ref[pl.ds(off, ta), :].astype(BF16))
        z = jnp.where(sel[None], z.reshape(H_DSA, tq, ta), NEG_BIG).reshape(H_DSA * tq, ta)
        m_new = jnp.maximum(m, jnp.max(z, axis=1, keepdims=True))
        alpha = jnp.exp(m - m_new)
        p = jnp.exp(z - m_new)
        l = alpha * l + jnp.sum(p, axis=1, keepdims=True)
        acc = alpha * acc + _dot(p.astype(BF16), vb_ref[pl.ds(off, ta), :].astype(BF16))
        return m_new, l, acc

    carry = (jnp.full((H_DSA * tq, 1), NEG_BIG, F32), jnp.zeros((H_DSA * tq, 1), F32),
             jnp.zeros((H_DSA * tq, LANES), F32))
    carry = lax.fori_loop(0, i // AG, lambda c, cr: att_group(c, cr, False), carry)
    m, l, acc = att_group(i // AG, carry, True)
    out = acc / l
    for mblk in range(4):
        parts = []
        for h in (2 * mblk, 2 * mblk + 1):
            o_h = out[h * tq:(h + 1) * tq]
            if (h % 2) != h // (H_DSA // KV_DSA):
                o_h = pltpu.roll(o_h, 64, 1)
            parts.append(o_h)
        o_ref[:, LANES * mblk:LANES * (mblk + 1)] = jnp.where(lo_half, parts[0], parts[1])


def _dsa_prompt(qi, wi, ki2, qb, kb, vb, B, T, tq, n_sel):
    nq = T // tq
    qrow = lambda b, i: (b * nq + i, 0)
    full = lambda b, i: (b, 0)
    AG = 4 if nq % 4 == 0 else 1
    return pl.pallas_call(
        functools.partial(_dsa_prompt_kernel, tq=tq, n_sel=n_sel, idx_bits=max(1, (T - 1).bit_length()), AG=AG),
        grid=(B, nq),
        in_specs=[pl.BlockSpec((tq, 512), qrow), pl.BlockSpec((tq, LANES), qrow),
                  pl.BlockSpec((T, LANES), full), pl.BlockSpec((tq, 512), qrow),
                  pl.BlockSpec((T, LANES), full), pl.BlockSpec((T, LANES), full)],
        out_specs=pl.BlockSpec((tq, 512), qrow),
        out_shape=jax.ShapeDtypeStruct(qb.shape, F32),
        scratch_shapes=[pltpu.VMEM((T // tq, tq, tq), I32), pltpu.VMEM((tq, 1), I32)],
        compiler_params=_cparams("parallel", "arbitrary"),
        name="dsa_prompt",
    )(qi, wi, ki2, qb, kb, vb)


def _dsa_score_sample_kernel(pt_ref, q_ref, w_ref, kn_ref, *rest, G, n_pages, n_real, n_sel, idx_bits):
    kc_refs = rest[:G]
    key_ref, thr_ref, jl_ref = rest[G:]
    j = pl.program_id(1)
    qrow = lax.broadcasted_iota(I32, (SROWS, PAGE), 0)
    kcol = lax.broadcasted_iota(I32, (SROWS, PAGE), 1)
    q = q_ref[0].astype(BF16)
    w = w_ref[0]

    def score_keys(s):
        s = jnp.maximum(s, 0.0) * w
        sc = s[0:SROWS]
        for h in range(1, H_IDX):
            sc = sc + s[h * SROWS:(h + 1) * SROWS]
        return _score_key(sc * _IDX_SCALE)

    @pl.when(j < n_pages // G)
    def _():
        kt = jnp.concatenate([kc_refs[s][...] for s in range(G)], axis=1).astype(BF16)
        key = score_keys(_dot(q, kt))
        for s in range(G):
            key_ref[0, j * G + s] = key[:, PAGE * s:PAGE * (s + 1)]

    @pl.when(j == n_pages // G)
    def _():
        key = score_keys(_dot_nt(q, kn_ref[0].astype(BF16)))
        key_ref[0, n_pages] = jnp.where(kcol <= qrow, key, INT_MIN)
        pos = lax.broadcasted_iota(I32, (G, SROWS, PAGE), 0) * PAGE + kcol[None]

        def count_ge(cand):
            def body(g, acc):
                blk = key_ref[0, pl.ds(g * G, G)]
                return acc + jnp.sum((blk >= cand[None]).astype(I32), axis=0)
            acc = lax.fori_loop(0, n_pages // G, body, jnp.zeros((SROWS, PAGE), I32))
            acc = acc + (key_ref[0, n_pages] >= cand).astype(I32)
            return jnp.sum(acc, axis=1, keepdims=True)

        def count_eq_lt(thr, jcand):
            def body(g, acc):
                blk = key_ref[0, pl.ds(g * G, G)]
                hit = (blk == thr[None]) & (pos + g * (G * PAGE) < jcand[None])
                return acc + jnp.sum(hit.astype(I32), axis=0)
            acc = lax.fori_loop(0, n_pages // G, body, jnp.zeros((SROWS, PAGE), I32))
            last = (key_ref[0, n_pages] == thr) & (kcol + n_pages * PAGE < jcand)
            return jnp.sum(acc + last.astype(I32), axis=1, keepdims=True)

        thr_ref[0] = _topk_threshold(count_ge, count_eq_lt, SROWS, n_real, n_sel, idx_bits, jl_ref.at[0])


def _dsa_score_sample(pt, q, w, kn, kc, n_real, n_sel):
    B, NP = pt.shape
    G = _pages_per_step(NP)
    L = (NP + 1) * PAGE
    page = lambda s: pl.BlockSpec((None, D_IDX, PAGE),
                                  lambda b, j, pt: (pt[b, jnp.minimum(j * G + s, NP - 1)], 0, 0))
    grid_spec = pltpu.PrefetchScalarGridSpec(
        num_scalar_prefetch=1,
        grid=(B, NP // G + 1),
        in_specs=([pl.BlockSpec((1, H_IDX * SROWS, D_IDX), lambda b, j, pt: (b, 0, 0)),
                   pl.BlockSpec((1, H_IDX * SROWS, 1), lambda b, j, pt: (b, 0, 0)),
                   pl.BlockSpec((1, PAGE, D_IDX), lambda b, j, pt: (b, 0, 0))]
                  + [page(s) for s in range(G)]),
        out_specs=[pl.BlockSpec((1, NP + 1, SROWS, PAGE), lambda b, j, pt: (b, 0, 0, 0)),
                   pl.BlockSpec((1, SROWS, 1), lambda b, j, pt: (b, 0, 0)),
                   pl.BlockSpec((1, SROWS, 1), lambda b, j, pt: (b, 0, 0))])
    return pl.pallas_call(
        functools.partial(_dsa_score_sample_kernel, G=G, n_pages=NP, n_real=n_real, n_sel=n_sel,
                          idx_bits=max(1, (L - 1).bit_length())),
        grid_spec=grid_spec,
        out_shape=[jax.ShapeDtypeStruct((B, NP + 1, SROWS, PAGE), I32),
                   jax.ShapeDtypeStruct((B, SROWS, 1), I32),
                   jax.ShapeDtypeStruct((B, SROWS, 1), I32)],
        compiler_params=_cparams("parallel", "arbitrary"),
        name="dsa_score_sample",
    )(pt, q, w, kn, *([kc] * G))


_GR = (H_DSA // KV_DSA) * SROWS


def _dsa_att_sample_kernel(pt_ref, q_ref, key_ref, thr_ref, jl_ref, kn_ref, vn_ref, *rest, G, n_pages):
    kc_refs, vc_refs = rest[:G], rest[G:2 * G]
    o_ref, m_scr, l_scr, acc_scr = rest[2 * G:]
    j = pl.program_id(1)
    qrow = lax.broadcasted_iota(I32, (SROWS, PAGE), 0)
    kcol = lax.broadcasted_iota(I32, (SROWS, PAGE), 1)
    nrep = H_DSA // KV_DSA

    @pl.when(j == 0)
    def _():
        m_scr[...] = jnp.full(m_scr.shape, NEG_BIG, F32)
        l_scr[...] = jnp.zeros(l_scr.shape, F32)
        acc_scr[...] = jnp.zeros(acc_scr.shape, F32)

    def attend(z, v, g, sel, v_transposed=True):
        n = z.shape[1]
        z = jnp.where(sel[None], z.reshape(nrep, SROWS, n), NEG_BIG).reshape(_GR, n)
        m = m_scr[g][:, 0:1]
        l = l_scr[g][:, 0:1]
        m_new = jnp.maximum(m, jnp.max(z, axis=1, keepdims=True))
        alpha = jnp.exp(m - m_new)
        p = jnp.exp(z - m_new)
        l = alpha * l + jnp.sum(p, axis=1, keepdims=True)
        pv = _dot_nt(p.astype(BF16), v.astype(BF16)) if v_transposed else _dot(p.astype(BF16), v.astype(BF16))
        acc_scr[g] = alpha * acc_scr[g] + pv
        m_scr[g] = jnp.broadcast_to(m_new, (_GR, LANES))
        l_scr[g] = jnp.broadcast_to(l, (_GR, LANES))

    def qgroup(g):
        return (q_ref[0, g] * (D_HEAD ** -0.5)).astype(BF16)

    thr = thr_ref[0]
    jl = jl_ref[0]

    @pl.when(j < n_pages // G)
    def _():
        sels = []
        for s in range(G):
            kblk = key_ref[0, j * G + s]
            sels.append((kblk > thr) | ((kblk == thr) & (kcol + (j * G + s) * PAGE <= jl)))
        sel = jnp.concatenate(sels, axis=1)
        for g in range(KV_DSA):
            kt = jnp.concatenate([kc_refs[s][g] for s in range(G)], axis=1).astype(BF16)
            vt = jnp.concatenate([vc_refs[s][g] for s in range(G)], axis=1)
            attend(_dot(qgroup(g), kt), vt, g, sel)

    @pl.when(j == n_pages // G)
    def _():
        off = n_pages * PAGE
        kblk = key_ref[0, n_pages]
        sel = ((kblk > thr) | ((kblk == thr) & (kcol + off <= jl))) & (kcol <= qrow)
        for g in range(KV_DSA):
            kn = kn_ref[0, :, D_HEAD * g:D_HEAD * (g + 1)].astype(BF16)
            attend(_dot_nt(qgroup(g), kn), vn_ref[0, :, D_HEAD * g:D_HEAD * (g + 1)], g, sel, v_transposed=False)
        for g in range(KV_DSA):
            o_ref[0, g] = acc_scr[g] / l_scr[g][:, 0:1]


def _dsa_att_sample(pt, q, keys, thr, jl, kn, vn, kc, vc):
    B, NP = pt.shape
    G = _pages_per_step(NP)
    page = lambda s: pl.BlockSpec((None, KV_DSA, D_HEAD, PAGE),
                                  lambda b, j, pt: (pt[b, jnp.minimum(j * G + s, NP - 1)], 0, 0, 0))
    per_b3 = lambda b, j, pt: (b, 0, 0)
    grid_spec = pltpu.PrefetchScalarGridSpec(
        num_scalar_prefetch=1,
        grid=(B, NP // G + 1),
        in_specs=([pl.BlockSpec((1, KV_DSA, _GR, D_HEAD), lambda b, j, pt: (b, 0, 0, 0)),
                   pl.BlockSpec((1, NP + 1, SROWS, PAGE), lambda b, j, pt: (b, 0, 0, 0)),
                   pl.BlockSpec((1, SROWS, 1), per_b3),
                   pl.BlockSpec((1, SROWS, 1), per_b3),
                   pl.BlockSpec((1, PAGE, LANES), per_b3),
                   pl.BlockSpec((1, PAGE, LANES), per_b3)]
                  + [page(s) for s in range(G)] * 2),
        out_specs=pl.BlockSpec((1, KV_DSA, _GR, D_HEAD), lambda b, j, pt: (b, 0, 0, 0)),
        scratch_shapes=[pltpu.VMEM((KV_DSA, _GR, LANES), F32), pltpu.VMEM((KV_DSA, _GR, LANES), F32),
                        pltpu.VMEM((KV_DSA, _GR, D_HEAD), F32)])
    return pl.pallas_call(
        functools.partial(_dsa_att_sample_kernel, G=G, n_pages=NP),
        grid_spec=grid_spec,
        out_shape=jax.ShapeDtypeStruct((B, KV_DSA, _GR, D_HEAD), F32),
        compiler_params=_cparams("parallel", "arbitrary"),
        name="dsa_att_sample",
    )(pt, q, keys, thr, jl, kn, vn, *([kc] * G), *([vc] * G))


def _mm_norm_res_kernel(*refs, n):
    a_refs, w_refs = refs[:n], refs[n:2 * n]
    g_ref, x_ref, o_ref = refs[2 * n:]
    acc = _dot(a_refs[0][...].astype(BF16), w_refs[0][...])
    for a_ref, w_ref in zip(a_refs[1:], w_refs[1:]):
        acc = acc + _dot(a_ref[...].astype(BF16), w_ref[...])
    o_ref[...] = x_ref[...] + _rms(acc, g_ref[...])


def _mm_norm_res(a_list, w_list, g, x, tm, name):
    M = x.shape[0]
    n = len(a_list)
    row = lambda i: (i, 0)
    const = lambda i: (0, 0)
    return pl.pallas_call(
        functools.partial(_mm_norm_res_kernel, n=n),
        grid=(M // tm,),
        in_specs=([pl.BlockSpec((tm, a.shape[1]), row) for a in a_list]
                  + [pl.BlockSpec(w.shape, const) for w in w_list]
                  + [pl.BlockSpec((1, D_MODEL), const), pl.BlockSpec((tm, D_MODEL), row)]),
        out_specs=pl.BlockSpec((tm, D_MODEL), row),
        out_shape=jax.ShapeDtypeStruct((M, D_MODEL), F32),
        compiler_params=_cparams("parallel"),
        name=name,
    )(*a_list, *w_list, g, x)


_GELU_C = math.sqrt(2.0 / math.pi)
FFN_SUB = 256


def _gelu_tanh(x):
    return x * (0.5 * (1.0 + jnp.tanh(_GELU_C * (x + 0.044715 * (x * x * x)))))


def _ffn_up_kernel(x_ref, g_ref, wg_ref, wv_ref, cwg_ref, cwv_ref, cbg_ref, cbv_ref, pg_ref, pv_ref,
                   act_ref, og_ref, ov_ref, eg_scr, ev_scr, *, tm, T, streamed):
    i = pl.program_id(1)
    h = _rms(x_ref[...], g_ref[...]).astype(BF16)
    trow = lax.broadcasted_iota(I32, (tm, 1), 0) % T

    tn = act_ref.shape[1]
    if streamed:
        @pl.when((i * tm) % T == 0)
        def _():
            eg_scr[6:8, :] = pg_ref[...]
            ev_scr[6:8, :] = pv_ref[...]
    else:
        eg_scr[0:8, :] = jnp.zeros((8, tn), F32)
        ev_scr[0:8, :] = jnp.zeros((8, tn), F32)

    def conv(u, cs, cw_ref, cb_ref, p_ref, o_ref, e_scr):
        e_scr[8:8 + tm, cs] = u
        if streamed:
            o_ref[:, cs] = u[tm - 2:tm, :]
            u1 = e_scr[7:7 + tm, cs]
            u2 = e_scr[6:6 + tm, cs]
            e_scr[6:8, cs] = u[tm - 2:tm, :]
        else:
            o_ref[:, cs] = u
            u1 = jnp.where(trow >= 1, e_scr[7:7 + tm, cs], p_ref[0, :, cs])
            u2 = jnp.where(trow >= 2, e_scr[6:6 + tm, cs], p_ref[1, :, cs])
        return cb_ref[:, cs] + cw_ref[0:1, cs] * u2 + cw_ref[1:2, cs] * u1 + cw_ref[2:3, cs] * u

    subs = [slice(c0, min(c0 + FFN_SUB, tn)) for c0 in range(0, tn, FFN_SUB)]
    dots = lambda cs: (_dot(h, wg_ref[:, cs]), _dot(h, wv_ref[:, cs]))
    cur = dots(subs[0])
    for n, cs in enumerate(subs):
        nxt = dots(subs[n + 1]) if n + 1 < len(subs) else None
        gate = conv(cur[0], cs, cwg_ref, cbg_ref, pg_ref, og_ref, eg_scr)
        val = conv(cur[1], cs, cwv_ref, cbv_ref, pv_ref, ov_ref, ev_scr)
        act_ref[:, cs] = (_gelu_tanh(gate) * val).astype(BF16)
        cur = nxt


def _ffn_up(x, g, w_up, conv_w, conv_b, prev, T, tm, tn):
    M = x.shape[0]
    nj = D_FF // tn
    streamed = T >= tm
    tpb = max(T // tm, 1)
    if streamed:
        p_spec = lambda off: pl.BlockSpec((None, 2, tn), lambda j, i: (i // tpb, 0, j + off))
        o_shape = jax.ShapeDtypeStruct((M // T, 2, D_FF), F32)
        o_spec = pl.BlockSpec((None, 2, tn), lambda j, i: (i // tpb, 0, j))
    else:
        p_spec = lambda off: pl.BlockSpec((2, tm, tn), lambda j, i: (0, i, j + off))
        o_shape = jax.ShapeDtypeStruct((M, D_FF), F32)
        o_spec = pl.BlockSpec((tm, tn), lambda j, i: (i, j))
    col = lambda rows, off: pl.BlockSpec((rows, tn), lambda j, i: (0, j + off))
    return pl.pallas_call(
        functools.partial(_ffn_up_kernel, tm=tm, T=T, streamed=streamed),
        grid=(nj, M // tm),
        in_specs=[pl.BlockSpec((tm, D_MODEL), lambda j, i: (i, 0)),
                  pl.BlockSpec((1, D_MODEL), lambda j, i: (0, 0)),
                  col(D_MODEL, 0), col(D_MODEL, nj), col(3, 0), col(3, nj), col(1, 0), col(1, nj),
                  p_spec(0), p_spec(nj)],
        out_specs=[pl.BlockSpec((tm, tn), lambda j, i: (i, j)), o_spec, o_spec],
        out_shape=[jax.ShapeDtypeStruct((M, D_FF), BF16), o_shape, o_shape],
        scratch_shapes=[pltpu.VMEM((tm + 8, tn), F32), pltpu.VMEM((tm + 8, tn), F32)],
        compiler_params=_cparams("parallel", "arbitrary"),
        name="ffn_up",
    )(x, g, w_up, w_up, conv_w, conv_w, conv_b, conv_b, prev, prev)


def _rk_pre_kernel(x_ref, g_ref, sp_ref, mix_ref, wr_ref, wk_ref, wv_ref, w1_ref, w2_ref, a1_ref, a2_ref,
                   g1_ref, g2_ref, vec_ref,
                   hn_ref, r_ref, wl_ref, kf_ref, v_ref, av_ref, bv_ref, gg_ref, e_scr, *, tm, T, streamed):
    i = pl.program_id(0)
    h = _rms(x_ref[...], g_ref[...])
    hn_ref[...] = h
    e_scr[8:8 + tm, :] = h
    if streamed:
        @pl.when((i * tm) % T == 0)
        def _():
            e_scr[7:8, :] = sp_ref[...]
        xp = e_scr[7:7 + tm, :]
        e_scr[7:8, :] = h[tm - 1:tm, :]
    else:
        e_scr[0:8, :] = jnp.zeros((8, D_MODEL), F32)
        trow = lax.broadcasted_iota(I32, (tm, 1), 0) % T
        xp = jnp.where(trow >= 1, e_scr[7:7 + tm, :], sp_ref[...])
    xx = xp - h
    mixed = lambda j: (h + xx * mix_ref[j:j + 1, :]).astype(BF16)
    w0, a0, k_k, k_a = vec_ref[0:1, :], vec_ref[1:2, :], vec_ref[2:3, :], vec_ref[3:4, :]
    r = _dot(mixed(0), wr_ref[...])
    k = _dot(mixed(2), wk_ref[...])
    v = _dot(mixed(3), wv_ref[...])
    lw = w0 + _dot(jnp.tanh(_dot(mixed(1), w1_ref[...])).astype(BF16), w2_ref[...])
    w_log = -(jnp.maximum(-lw, 0.0) + jnp.log(1.0 + jnp.exp(-jnp.abs(lw)))) - 0.5
    wl_ref[...] = -jnp.exp(w_log)
    a = jax.nn.sigmoid(a0 + _dot(_dot(mixed(4), a1_ref[...]).astype(BF16), a2_ref[...]))
    gg_ref[...] = _dot(jax.nn.sigmoid(_dot(mixed(5), g1_ref[...])).astype(BF16), g2_ref[...])
    kk = k * k_k
    bd = _seg_ones()
    kk = kk / jnp.maximum(jnp.sqrt(_seg64_sum(kk * kk, bd)), 1e-12)
    r_ref[...] = r
    v_ref[...] = v
    kf_ref[...] = k * (1.0 + (a - 1.0) * k_a)
    av_ref[...] = -kk
    bv_ref[...] = kk * a


def _rk_pre(x, g, sp, mix, wr, wk, wv, w1, w2, a1, a2, g1, g2, vecs, T, tm):
    M = x.shape[0]
    streamed = T >= tm
    tpb = max(T // tm, 1)
    row = lambda i: (i, 0)
    const = lambda i: (0, 0)
    sp_spec = (pl.BlockSpec((None, 1, D_MODEL), lambda i: (i // tpb, 0, 0)) if streamed
               else pl.BlockSpec((tm, D_MODEL), row))
    full = lambda a: pl.BlockSpec(a.shape, const)
    return pl.pallas_call(
        functools.partial(_rk_pre_kernel, tm=tm, T=T, streamed=streamed),
        grid=(M // tm,),
        in_specs=[pl.BlockSpec((tm, D_MODEL), row), pl.BlockSpec((1, D_MODEL), const), sp_spec,
                  full(mix), full(wr), full(wk), full(wv), full(w1), full(w2), full(a1), full(a2),
                  full(g1), full(g2), full(vecs)],
        out_specs=[pl.BlockSpec((tm, D_MODEL), row)] * 8,
        out_shape=[jax.ShapeDtypeStruct((M, D_MODEL), F32)] * 8,
        scratch_shapes=[pltpu.VMEM((tm + 8, D_MODEL), F32)],
        compiler_params=_cparams("arbitrary"),
        name="rk_pre",
    )(x, g, sp, mix, wr, wk, wv, w1, w2, a1, a2, g1, g2, vecs)


def _bdot(a, b):
    return _dot(a.astype(BF16), b.astype(BF16))


def _rk_scan_kernel(r_ref, wl_ref, k_ref, v_ref, a_ref, b_ref, s0_ref, y_ref, sf_ref, s_scr, *, C, npair):
    c = pl.program_id(2)

    @pl.when(c == 0)
    def _():
        s_scr[...] = s0_ref[0]

    lane = lax.broadcasted_iota(I32, (1, LANES), 1)
    m0 = lane < 64
    rr = lax.broadcasted_iota(I32, (C, C), 0)
    cc = lax.broadcasted_iota(I32, (C, C), 1)
    tri_incl = (cc <= rr).astype(BF16)
    r2 = lax.broadcasted_iota(I32, (2 * C, 2 * C), 0)
    c2 = lax.broadcasted_iota(I32, (2 * C, 2 * C), 1)
    strict = (r2 % C) > (c2 % C)
    incl = (r2 % C) >= (c2 % C)
    eye = (r2 == c2).astype(F32)
    n_double = max(int(math.log2(C)) - 1, 0)

    def stack2(z):
        return jnp.concatenate([jnp.where(m0, z, 0.0), jnp.where(m0, 0.0, z)], axis=0)

    pairs = range(npair)
    sls = [slice(LANES * p, LANES * (p + 1)) for p in pairs]
    wl = [wl_ref[:, sl] for sl in sls]
    ld = []
    for p in pairs:
        wl_hi, wl_lo = _split(wl[p])
        ld.append(_dot(tri_incl, wl_hi) + _dot(tri_incl, wl_lo))
    dfull = [jnp.exp(x) for x in ld]
    dinv = [jnp.exp(-x) for x in ld]
    As = [stack2(a_ref[:, sls[p]] * jnp.exp(ld[p] - wl[p])).astype(BF16) for p in pairs]
    Bs = [stack2(b_ref[:, sls[p]] * dinv[p]) for p in pairs]
    Ks = [stack2(k_ref[:, sls[p]] * dinv[p]) for p in pairs]
    Rs = [stack2(r_ref[:, sls[p]] * dfull[p]).astype(BF16) for p in pairs]
    Vs = [stack2(v_ref[:, sls[p]]) for p in pairs]
    Vb = [x.astype(BF16) for x in Vs]
    BK = [jnp.concatenate([Bs[p], Ks[p]], axis=0).astype(BF16) for p in pairs]
    G1 = [_dot_nt(As[p], BK[p]) for p in pairs]
    G2 = [_dot_nt(Rs[p], BK[p]) for p in pairs]
    Lab = [jnp.where(strict, g[:, :2 * C], 0.0) for g in G1]
    Lak = [jnp.where(strict, g[:, 2 * C:], 0.0).astype(BF16) for g in G1]
    Mrb = [jnp.where(incl, g[:, :2 * C], 0.0).astype(BF16) for g in G2]
    Mrk = [jnp.where(incl, g[:, 2 * C:], 0.0).astype(BF16) for g in G2]
    Tm = [eye + x for x in Lab]
    P = Lab
    for _ in range(n_double):
        Pb = [x.astype(BF16) for x in P]
        P = [_dot(x, x) for x in Pb]
        Tm = [Tm[p] + _bdot(Tm[p], P[p]) for p in pairs]
    S = [s_scr[p] for p in pairs]
    Sb = [x.astype(BF16) for x in S]
    rhs = [_dot_nt(As[p], Sb[p]) + _dot(Lak[p], Vb[p]) for p in pairs]
    U = [_bdot(Tm[p], rhs[p]) for p in pairs]
    Ub = [x.astype(BF16) for x in U]
    Ys = [_dot_nt(Rs[p], Sb[p]) + _dot(Mrb[p], Ub[p]) + _dot(Mrk[p], Vb[p]) for p in pairs]
    upd = [_dot(U[p].T.astype(BF16), Bs[p].astype(BF16)) + _dot(Vs[p].T.astype(BF16), Ks[p].astype(BF16))
           for p in pairs]
    for p in pairs:
        y_ref[:, sls[p]] = Ys[p][:C] + Ys[p][C:]
        s_scr[p] = (S[p] + upd[p]) * dfull[p][C - 1:C, :]

    @pl.when(c == pl.num_programs(2) - 1)
    def _():
        sf_ref[0] = s_scr[...]


def _rk_scan(r, wl, k, v, a, b, s0, B, T, C, npair):
    nc = T // C
    ng = 8 // npair
    blk = pl.BlockSpec((C, LANES * npair), lambda bb, g, c: (bb * nc + c, g))
    s_spec = pl.BlockSpec((1, npair, LANES, LANES), lambda bb, g, c: (bb, g, 0, 0))
    return pl.pallas_call(
        functools.partial(_rk_scan_kernel, C=C, npair=npair),
        grid=(B, ng, nc),
        in_specs=[blk] * 6 + [s_spec],
        out_specs=[blk, s_spec],
        out_shape=[jax.ShapeDtypeStruct(r.shape, F32), jax.ShapeDtypeStruct(s0.shape, F32)],
        scratch_shapes=[pltpu.VMEM((npair, LANES, LANES), F32)],
        compiler_params=_cparams("parallel", "parallel", "arbitrary"),
        name="rk_scan",
    )(r, wl, k, v, a, b, s0)


def _rk_post_kernel(y_ref, r_ref, kf_ref, v_ref, gg_ref, vec_ref, o_ref):
    bd = _seg_ones()
    ln_w, ln_b, r_k = vec_ref[0:1, :], vec_ref[1:2, :], vec_ref[2:3, :]
    y = y_ref[...]
    mu = _seg64_sum(y, bd) * (1.0 / RK_N)
    d = y - mu
    var = _seg64_sum(d * d, bd) * (1.0 / RK_N)
    yn = d * lax.rsqrt(var + RK_GN_EPS) * ln_w + ln_b
    bonus = _seg64_sum(r_ref[...] * kf_ref[...] * r_k, bd) * v_ref[...]
    o_ref[...] = ((yn + bonus) * gg_ref[...]).astype(BF16)


def _rk_post(y, r, kf, v, gg, vecs, tm):
    M = y.shape[0]
    row = lambda i: (i, 0)
    return pl.pallas_call(
        _rk_post_kernel,
        grid=(M // tm,),
        in_specs=[pl.BlockSpec((tm, D_MODEL), row)] * 5 + [pl.BlockSpec(vecs.shape, lambda i: (0, 0))],
        out_specs=pl.BlockSpec((tm, D_MODEL), row),
        out_shape=jax.ShapeDtypeStruct((M, D_MODEL), BF16),
        compiler_params=_cparams("parallel"),
        name="rk_post",
    )(y, r, kf, v, gg, vecs)


def _row_tile(M):
    return min(256, M)


def _pad_rows(a, n):
    return jnp.pad(a, ((0, 0), (0, n - a.shape[1])) + ((0, 0),) * (a.ndim - 2))


def _attn_layer_prompt(x, B, T, g_pre, w_in, w_outs, g_post):
    M = B * T
    tm = _row_tile(M)
    cos, sin = _rope_tables(jnp.arange(T, dtype=I32))
    qa, ka, va, qb, kb, vb, qi, ki2, wi = _attn_proj(x, g_pre, w_in, cos, sin, tm)
    n_sel = min(DSA_TOPK, T // 4)
    oa = _sb_prompt(qa, ka, va, B, T, min(256, T))
    ob = _dsa_prompt(qi, wi, ki2, qb, kb, vb, B, T, min(128, T), n_sel)
    x = _mm_norm_res([oa, ob], w_outs, g_post, x, tm, "attn_out")
    rows = (ka.reshape(1, B, T, H_SB, D_HEAD), va.reshape(1, B, T, H_SB, D_HEAD),
            kb.reshape(1, B, T, KV_DSA, D_HEAD), vb.reshape(1, B, T, KV_DSA, D_HEAD),
            ki2[:, :D_IDX].reshape(1, B, T, D_IDX))
    return x, rows


def _attn_layer_sample(x, B, T, P, page_table, caches, g_pre, w_in, w_outs, g_post):
    M = B * T
    cos, sin = _rope_tables(P + jnp.arange(T, dtype=I32))
    cos, sin = jnp.tile(cos, (B, 1)), jnp.tile(sin, (B, 1))
    qa, ka, va, qb, kb, vb, qi, ki2, wi = _attn_proj(x, g_pre, w_in, cos, sin, M)
    c_sb_k, c_sb_v, c_dsa_k, c_dsa_v, c_idx = caches
    n_pool = c_sb_k.shape[0]
    n_sel = min(DSA_TOPK, (P + T) // 4)
    b3 = lambda a: a.reshape(B, T, a.shape[-1])

    row_minor = lambda c: jnp.moveaxis(c, 1, -1)
    oa = _sb_sample(page_table, _pad_rows(b3(qa), SROWS), _pad_rows(b3(ka), PAGE), _pad_rows(b3(va), PAGE),
                    row_minor(c_sb_k), row_minor(c_sb_v))
    oa = oa[:, :T].reshape(M, H_SB * D_HEAD)

    qi_s = _pad_rows(b3(qi).reshape(B, T, H_IDX, D_IDX), SROWS).transpose(0, 2, 1, 3).reshape(B, H_IDX * SROWS, D_IDX)
    wi_s = _pad_rows(b3(wi)[:, :, :H_IDX], SROWS).transpose(0, 2, 1).reshape(B, H_IDX * SROWS, 1)
    keys, thr, jl = _dsa_score_sample(page_table, qi_s, wi_s, _pad_rows(b3(ki2)[:, :, :D_IDX], PAGE),
                                      row_minor(c_idx), T, n_sel)
    qb_s = _pad_rows(b3(qb).reshape(B, T, H_DSA, D_HEAD), SROWS).transpose(0, 2, 1, 3).reshape(B, KV_DSA, _GR, D_HEAD)
    ob = _dsa_att_sample(page_table, qb_s, keys, thr, jl, _pad_rows(b3(kb), PAGE), _pad_rows(b3(vb), PAGE),
                         row_minor(c_dsa_k), row_minor(c_dsa_v))
    ob = ob.reshape(B, H_DSA, SROWS, D_HEAD)[:, :, :T].transpose(0, 2, 1, 3).reshape(M, H_DSA * D_HEAD)

    x = _mm_norm_res([oa, ob], w_outs, g_post, x, M, "attn_out")
    rows = (ka.reshape(1, B, T, H_SB, D_HEAD), va.reshape(1, B, T, H_SB, D_HEAD),
            kb.reshape(1, B, T, KV_DSA, D_HEAD), vb.reshape(1, B, T, KV_DSA, D_HEAD),
            ki2[:, :D_IDX].reshape(1, B, T, D_IDX))
    return x, rows


def _pair_state(S):
    B = S.shape[0]
    S = S.reshape(B, 8, 2, RK_N, RK_N)
    z = jnp.zeros_like(S[:, :, 0])
    top = jnp.concatenate([S[:, :, 0], z], axis=-1)
    bot = jnp.concatenate([z, S[:, :, 1]], axis=-1)
    return jnp.concatenate([top, bot], axis=-2)


def _unpair_state(Sp):
    B = Sp.shape[0]
    return jnp.stack([Sp[:, :, :RK_N, :RK_N], Sp[:, :, RK_N:, RK_N:]], axis=2).reshape(B, RK_H, RK_N, RK_N)


RK_CHUNK = 64
RK_PAIRS = 8


def _rwkv_layer(x, B, T, shift_prev, S0, g_pre, prm, g_post):
    M = B * T
    tm = _row_tile(M)
    (mix, wr, wk, wv, wo, w1, w2, a1, a2, g1, g2, vec_pre, vec_post) = prm
    if T >= tm:
        sp = shift_prev.reshape(B, 1, D_MODEL)
    else:
        sp = jnp.repeat(shift_prev, T, axis=0)
    hn, r, wl, kf, v, av, bv, gg = _rk_pre(x, g_pre, sp, mix, wr, wk, wv, w1, w2, a1, a2, g1, g2, vec_pre, T, tm)
    Tp = -(-T // RK_CHUNK) * RK_CHUNK
    if Tp != T:
        padt = lambda a: _pad_rows(a.reshape(B, T, D_MODEL), Tp).reshape(B * Tp, D_MODEL)
        y, Sf = _rk_scan(padt(r), padt(wl), padt(kf), padt(v), padt(av), padt(bv), _pair_state(S0), B, Tp, RK_CHUNK, RK_PAIRS)
        y = y.reshape(B, Tp, D_MODEL)[:, :T].reshape(M, D_MODEL)
    else:
        y, Sf = _rk_scan(r, wl, kf, v, av, bv, _pair_state(S0), B, T, RK_CHUNK, RK_PAIRS)
    z = _rk_post(y, r, kf, v, gg, vec_post, tm)
    x = _mm_norm_res([z], [wo], g_post, x, tm, "rk_out")
    shift = hn.reshape(B, T, D_MODEL)[:, -1]
    return x, shift, _unpair_state(Sf)


def _ffn_layer(x, B, T, prev, g_pre, w_up, conv_w, conv_b, w_down, g_post):
    M = B * T
    tm = _row_tile(M)
    tn = D_FF // 2
    if T >= tm:
        act, cg, cv = _ffn_up(x, g_pre, w_up, conv_w, conv_b, prev, T, tm, tn)
        conv_state = jnp.concatenate([cg, cv], axis=-1)
    else:
        zeros = jnp.zeros((B, T, 2 * D_FF), F32)
        p1 = zeros.at[:, 0].set(prev[:, 1])
        p2 = zeros.at[:, 0].set(prev[:, 0]).at[:, 1].set(prev[:, 1])
        pp = jnp.stack([p1.reshape(M, -1), p2.reshape(M, -1)])
        act, ug, uv = _ffn_up(x, g_pre, w_up, conv_w, conv_b, pp, T, tm, tn)
        u = jnp.concatenate([ug, uv], axis=-1).reshape(B, T, 2 * D_FF)
        conv_state = jnp.concatenate([prev, u], axis=1)[:, -2:]
    x = _mm_norm_res([act], [w_down], g_post, x, tm, "ffn_down")
    return x, conv_state


def _forward(x_prompt, x_sample, cache_sb_k, cache_sb_v, cache_dsa_k, cache_dsa_v, cache_idx_k, page_table,
             state_wkv, state_shift, state_ffn_conv,
             norm_mix_pre, norm_mix_post, norm_ffn_pre, norm_ffn_post,
             att_w_in, att_w_out,
             rk_mix, rk_w_r, rk_w_k, rk_w_v, rk_w_o, rk_w0, rk_w1, rk_w2, rk_a0, rk_a1, rk_a2,
             rk_g1, rk_g2, rk_k_k, rk_k_a, rk_r_k, rk_ln_w, rk_ln_b,
             ffn_w_up, ffn_conv_w, ffn_conv_b, ffn_w_down):
    Bp, Tp, D = x_prompt.shape
    Bs, Ts, _ = x_sample.shape
    P = page_table.shape[1] * PAGE
    depth = norm_mix_pre.shape[0]
    xp = x_prompt.reshape(Bp * Tp, D)
    xs = x_sample.reshape(Bs * Ts, D)
    vrow = lambda a: a.reshape(1, -1)
    bf = lambda a: a.astype(BF16)
    att_p, att_s, wkv_p, wkv_s, sh_p, sh_s, cv_p, cv_s = [], [], [], [], [], [], [], []
    for i in range(depth):
        li = i // 2
        g_pre, g_post = vrow(norm_mix_pre[i]), vrow(norm_mix_post[i])
        if i % 2 == 0:
            w_in = _pack_w_in(att_w_in[li])
            w_outs = [bf(att_w_out[li][:H_SB * D_HEAD]), bf(att_w_out[li][H_SB * D_HEAD:])]
            caches = (cache_sb_k[li], cache_sb_v[li], cache_dsa_k[li], cache_dsa_v[li], cache_idx_k[li])
            xp, rows_p = _attn_layer_prompt(xp, Bp, Tp, g_pre, w_in, w_outs, g_post)
            xs, rows_s = _attn_layer_sample(xs, Bs, Ts, P, page_table, caches, g_pre, w_in, w_outs, g_post)
            att_p.append(rows_p)
            att_s.append(rows_s)
        else:
            vec_pre = jnp.stack([rk_w0[li], rk_a0[li], rk_k_k[li], rk_k_a[li]])
            vec_post = jnp.stack([rk_ln_w[li], rk_ln_b[li], rk_r_k[li].reshape(-1)])
            prm = (rk_mix[li], bf(rk_w_r[li]), bf(rk_w_k[li]), bf(rk_w_v[li]), bf(rk_w_o[li]),
                   bf(rk_w1[li]), bf(rk_w2[li]), bf(rk_a1[li]), bf(rk_a2[li]), bf(rk_g1[li]), bf(rk_g2[li]),
                   vec_pre, vec_post)
            xp, shp, Sp = _rwkv_layer(xp, Bp, Tp, jnp.zeros((Bp, D), F32), jnp.zeros((Bp, RK_H, RK_N, RK_N), F32),
                                      g_pre, prm, g_post)
            xs, shs, Ss = _rwkv_layer(xs, Bs, Ts, state_shift[li], state_wkv[li], g_pre, prm, g_post)
            wkv_p.append(Sp)
            wkv_s.append(Ss)
            sh_p.append(shp)
            sh_s.append(shs)
        f_pre, f_post = vrow(norm_ffn_pre[i]), vrow(norm_ffn_post[i])
        w_up, w_down = bf(ffn_w_up[i]), bf(ffn_w_down[i])
        cb = vrow(ffn_conv_b[i])
        xp, cp = _ffn_layer(xp, Bp, Tp, jnp.zeros((Bp, 2, 2 * D_FF), F32), f_pre, w_up, ffn_conv_w[i], cb, w_down, f_post)
        xs, cs = _ffn_layer(xs, Bs, Ts, state_ffn_conv[i], f_pre, w_up, ffn_conv_w[i], cb, w_down, f_post)
        cv_p.append(cp)
        cv_s.append(cs)
    cat = lambda rows, j: jnp.concatenate([r[j] for r in rows], axis=0)
    return (xp.reshape(Bp, Tp, D), xs.reshape(Bs, Ts, D),
            cat(att_p, 0), cat(att_s, 0), cat(att_p, 1), cat(att_s, 1),
            cat(att_p, 2), cat(att_s, 2), cat(att_p, 3), cat(att_s, 3),
            cat(att_p, 4), cat(att_s, 4),
            jnp.stack(wkv_p), jnp.stack(wkv_s), jnp.stack(sh_p), jnp.stack(sh_s),
            jnp.stack(cv_p), jnp.stack(cv_s))


def kernel(x_prompt, x_sample, cache_sb_k, cache_sb_v, cache_dsa_k, cache_dsa_v, cache_idx_k, page_table, state_wkv, state_shift, state_ffn_conv, norm_mix_pre, norm_mix_post, norm_ffn_pre, norm_ffn_post, att_w_in, att_w_out, rk_mix, rk_w_r, rk_w_k, rk_w_v, rk_w_o, rk_w0, rk_w1, rk_w2, rk_a0, rk_a1, rk_a2, rk_g1, rk_g2, rk_k_k, rk_k_a, rk_r_k, rk_ln_w, rk_ln_b, ffn_w_up, ffn_conv_w, ffn_conv_b, ffn_w_down):
    return _forward(x_prompt, x_sample, cache_sb_k, cache_sb_v, cache_dsa_k, cache_dsa_v, cache_idx_k, page_table,
                    state_wkv, state_shift, state_ffn_conv,
                    norm_mix_pre, norm_mix_post, norm_ffn_pre, norm_ffn_post,
                    att_w_in, att_w_out,
                    rk_mix, rk_w_r, rk_w_k, rk_w_v, rk_w_o, rk_w0, rk_w1, rk_w2, rk_a0, rk_a1, rk_a2,
                    rk_g1, rk_g2, rk_k_k, rk_k_a, rk_r_k, rk_ln_w, rk_ln_b,
                    ffn_w_up, ffn_conv_w, ffn_conv_b, ffn_w_down)
```

```python
import functools
import math

import jax
import jax.numpy as jnp
from jax import lax
from jax.experimental import pallas as pl
from jax.experimental.pallas import tpu as pltpu

F32 = jnp.float32
BF16 = jnp.bfloat16
I32 = jnp.int32

D_MODEL = 1024
D_HEAD = 64
H_SB = 8
H_DSA = 8
KV_DSA = 2
H_IDX = 8
D_IDX = 64
DSA_TOPK = 256
PAGE = 128
ROPE_THETA = 10000.0
RK_N = 64
RK_H = D_MODEL // RK_N
RK_GN_EPS = 64e-5
D_FF = 2816
NORM_EPS = 1e-6

LANES = 128
VMEM_LIMIT = 56 * 1024 * 1024
INT_MIN = -2147483648
INT_MAX = 2147483647
NEG_BIG = -1e30

_C_QA, _C_KA, _C_VA, _C_QB, _C_KB, _C_VB, _C_QI, _C_KI, _C_WI, _C_END = (
    0, 512, 1024, 1536, 2048, 2176, 2304, 2816, 2944, 3072)


def _cparams(*sem):
    return pltpu.CompilerParams(dimension_semantics=sem, vmem_limit_bytes=VMEM_LIMIT)


def _dot(a, b):
    return jnp.dot(a, b, preferred_element_type=F32)


def _dot_nt(a, b):
    return lax.dot_general(a, b, (((1,), (1,)), ((), ())), preferred_element_type=F32)


def _dot_tn(a, b):
    return lax.dot_general(a, b, (((0,), (0,)), ((), ())), preferred_element_type=F32)


def _split(x):
    hi = x.astype(BF16)
    lo = (x - hi.astype(F32)).astype(BF16)
    return hi, lo


def _dot_x2(x, m_bf16):
    hi, lo = _split(x)
    return _dot(hi, m_bf16) + _dot(lo, m_bf16)


def _rms(x, g):
    return x * lax.rsqrt(jnp.mean(x * x, axis=-1, keepdims=True) + NORM_EPS) * g


def _seg_ones():
    r = lax.broadcasted_iota(I32, (LANES, LANES), 0)
    c = lax.broadcasted_iota(I32, (LANES, LANES), 1)
    return ((r < 64) == (c < 64)).astype(BF16)


def _seg64_sum(x, bd):
    n = x.shape[1] // LANES
    return jnp.concatenate([_dot_x2(x[:, LANES * m:LANES * (m + 1)], bd) for m in range(n)], axis=1)


def _attn_proj_kernel(x_ref, g_ref, w_ref, cos_ref, sin_ref,
                      qa_ref, ka_ref, va_ref, qb_ref, kb_ref, vb_ref, qi_ref, ki_ref, wi_ref):
    h = _rms(x_ref[...], g_ref[...]).astype(BF16)
    cos = cos_ref[...]
    sin = sin_ref[...]
    lane = lax.broadcasted_iota(I32, (1, LANES), 1)
    first = (lane & 32) == 0

    def proj(c0, c1):
        return _dot(h, w_ref[:, c0:c1])

    def rope(blk):
        rot = jnp.where(first, pltpu.roll(blk, LANES - 32, 1), pltpu.roll(blk, 32, 1))
        return blk * cos + rot * sin

    qa_ref[...] = proj(_C_QA, _C_KA)
    ka_ref[...] = proj(_C_KA, _C_VA)
    va_ref[...] = proj(_C_VA, _C_QB)
    for m in range(4):
        qb_ref[:, LANES * m:LANES * (m + 1)] = rope(proj(_C_QB + LANES * m, _C_QB + LANES * (m + 1)))
        qi_ref[:, LANES * m:LANES * (m + 1)] = rope(proj(_C_QI + LANES * m, _C_QI + LANES * (m + 1)))
    kb_ref[...] = rope(proj(_C_KB, _C_VB))
    vb_ref[...] = proj(_C_VB, _C_QI)
    ki_ref[...] = rope(proj(_C_KI, _C_WI))
    wi_ref[...] = proj(_C_WI, _C_END)


def _attn_proj(x, g, w, cos, sin, tm):
    M = x.shape[0]
    nt = cos.shape[0] // tm
    widths = (512, 512, 512, 512, 128, 128, 512, 128, 128)
    row = lambda i: (i, 0)
    return pl.pallas_call(
        _attn_proj_kernel,
        grid=(M // tm,),
        in_specs=[pl.BlockSpec((tm, D_MODEL), row),
                  pl.BlockSpec((1, D_MODEL), lambda i: (0, 0)),
                  pl.BlockSpec((D_MODEL, _C_END), lambda i: (0, 0)),
                  pl.BlockSpec((tm, LANES), lambda i: (i % nt, 0)),
                  pl.BlockSpec((tm, LANES), lambda i: (i % nt, 0))],
        out_specs=[pl.BlockSpec((tm, wd), row) for wd in widths],
        out_shape=[jax.ShapeDtypeStruct((M, wd), F32) for wd in widths],
        compiler_params=_cparams("parallel"),
        name="attn_proj",
    )(x, g, w, cos, sin)


def _rope_tables(pos):
    half = D_HEAD // 2
    inv = ROPE_THETA ** (-2.0 * jnp.arange(half, dtype=F32) / D_HEAD)
    ang = pos.astype(F32)[:, None] * inv[None, :]
    cos = jnp.cos(ang)
    sin = jnp.sin(ang)
    return jnp.tile(cos, (1, 4)), jnp.tile(jnp.concatenate([-sin, sin], axis=1), (1, 2))


def _pack_w_in(w_in):
    ki = w_in[:, 2816:2880]
    wi = w_in[:, 2880:2888]
    pad = jnp.zeros((w_in.shape[0], LANES - H_IDX), w_in.dtype)
    return jnp.concatenate([w_in[:, :2816], ki, ki, wi, pad], axis=1).astype(BF16)


LOG2E = math.log2(math.e)
ATT_SCALE = D_HEAD ** -0.5 * LOG2E


def _softplus2(z2):
    return jnp.log(1.0 + jnp.exp2(-jnp.abs(z2))) * LOG2E


def _sb_block(z, c, tri, vb, causal):
    t = _softplus2(z)
    lb = jnp.minimum(z, 0.0) - t
    lk = lb - z
    if causal is not None:
        lk = jnp.where(causal, lk, 0.0)
    aft = _dot(lk.astype(BF16), tri) + c
    w = jnp.exp2(lb + aft)
    if causal is not None:
        w = jnp.where(causal, w, 0.0)
    pv = _dot(w.astype(BF16), vb)
    return c + jnp.sum(lk, axis=1, keepdims=True), pv


SB_BLOCKS = 2


def _sb_prompt_kernel(q_ref, k_ref, v_ref, o_ref, *, tq):
    i = pl.program_id(2)
    lane = lax.broadcasted_iota(I32, (1, LANES), 1)
    row = lax.broadcasted_iota(I32, (tq, tq), 0)
    col = lax.broadcasted_iota(I32, (tq, tq), 1)
    tri = (row > col).astype(BF16)
    causal = col < row
    q = q_ref[...] * ATT_SCALE
    qh = [jnp.where(lane < 64, q, 0.0).astype(BF16), jnp.where(lane >= 64, q, 0.0).astype(BF16)]

    def step(js, carry, mask):
        kbs, vbs = [], []
        for j in js:
            off = pl.multiple_of(j * tq, tq)
            kbs.append(k_ref[pl.ds(off, tq), :].astype(BF16))
            vbs.append(v_ref[pl.ds(off, tq), :].astype(BF16))
        tiles = [(h2, n) for n in range(len(js)) for h2 in range(2)]
        z = [_dot_nt(qh[h2], kbs[n]) for h2, n in tiles]
        t = [_softplus2(x) for x in z]
        lb = [jnp.minimum(x, 0.0) - y for x, y in zip(z, t)]
        lk = [x - y for x, y in zip(lb, z)]
        if mask is not None:
            lk = [jnp.where(mask, x, 0.0) for x in lk]
        loc = [_dot(x.astype(BF16), tri) for x in lk]
        tot = [jnp.sum(x, axis=1, keepdims=True) for x in lk]
        c = [carry[0], carry[2]]
        w = []
        for idx, (h2, n) in enumerate(tiles):
            x = jnp.exp2(lb[idx] + loc[idx] + c[h2])
            w.append((jnp.where(mask, x, 0.0) if mask is not None else x).astype(BF16))
            c[h2] = c[h2] + tot[idx]
        pv = [_dot(w[idx], vbs[n]) for idx, (h2, n) in enumerate(tiles)]
        acc = [carry[1], carry[3]]
        for idx, (h2, n) in enumerate(tiles):
            acc[h2] = acc[h2] + pv[idx]
        return c[0], acc[0], c[1], acc[1]

    zc, za = jnp.zeros((tq, 1), F32), jnp.zeros((tq, LANES), F32)
    carry = step([i], (zc, za, zc, za), causal)
    nb = SB_BLOCKS
    carry = lax.fori_loop(0, i // nb, lambda jj, cr: step([i - 1 - nb * jj - u for u in range(nb)], cr, None), carry)
    carry = lax.fori_loop(0, i % nb, lambda jj, cr: step([i % nb - 1 - jj], cr, None), carry)
    o_ref[...] = jnp.where(lane < 64, carry[1], carry[3])


def _sb_prompt(q, k, v, B, T, tq):
    nq = T // tq
    return pl.pallas_call(
        functools.partial(_sb_prompt_kernel, tq=tq),
        grid=(B, 4, nq),
        in_specs=[pl.BlockSpec((tq, LANES), lambda b, p, i: (b * nq + i, p)),
                  pl.BlockSpec((T, LANES), lambda b, p, i: (b, p)),
                  pl.BlockSpec((T, LANES), lambda b, p, i: (b, p))],
        out_specs=pl.BlockSpec((tq, LANES), lambda b, p, i: (b * nq + i, p)),
        out_shape=jax.ShapeDtypeStruct(q.shape, F32),
        compiler_params=_cparams("parallel", "parallel", "arbitrary"),
        name="sb_prompt",
    )(q, k, v)


SROWS = 16


def _pages_per_step(n_pages):
    for g in (32, 16, 8, 4, 2):
        if n_pages % g == 0:
            return g
    return 1


def _sb_sample_kernel(pt_ref, q_ref, kn_ref, vn_ref, *rest, G):
    kc_refs, vc_refs = rest[:G], rest[G:2 * G]
    o_ref, c_scr, acc_scr = rest[2 * G:]
    j = pl.program_id(1)
    row = lax.broadcasted_iota(I32, (PAGE, PAGE), 0)
    col = lax.broadcasted_iota(I32, (PAGE, PAGE), 1)
    tri = (row > col).astype(BF16)
    qrow = lax.broadcasted_iota(I32, (SROWS, PAGE), 0)
    kcol = lax.broadcasted_iota(I32, (SROWS, PAGE), 1)
    causal = kcol < qrow
    scale = ATT_SCALE

    def qhead(h):
        return (q_ref[0, :, D_HEAD * h:D_HEAD * (h + 1)] * scale).astype(BF16)

    @pl.when(j == 0)
    def _():
        for h in range(H_SB):
            kb = kn_ref[0, :, D_HEAD * h:D_HEAD * (h + 1)].astype(BF16)
            vb = vn_ref[0, :, D_HEAD * h:D_HEAD * (h + 1)].astype(BF16)
            c, pv = _sb_block(_dot_nt(qhead(h), kb), jnp.zeros((SROWS, 1), F32), tri, vb, causal)
            c_scr[h] = jnp.broadcast_to(c, (SROWS, LANES))
            acc_scr[h] = pv

    order = range(G - 1, -1, -1)
    zs = []
    for h in range(H_SB):
        kt = jnp.concatenate([kc_refs[s][h] for s in order], axis=1).astype(BF16)
        z = _dot(qhead(h), kt)
        zs += [z[:, PAGE * p:PAGE * (p + 1)] for p in range(G)]
    z = jnp.concatenate(zs, axis=0)
    t = _softplus2(z)
    lb = jnp.minimum(z, 0.0) - t
    lk = lb - z
    loc = _dot(lk.astype(BF16), tri)
    tot = jnp.sum(lk, axis=1, keepdims=True)
    cs = []
    for h in range(H_SB):
        run = c_scr[h][:, 0:1]
        per_page = [None] * G
        for p in range(G - 1, -1, -1):
            per_page[p] = run
            run = run + tot[(h * G + p) * SROWS:(h * G + p + 1) * SROWS]
        c_scr[h] = jnp.broadcast_to(run, (SROWS, LANES))
        cs += per_page
    w = jnp.exp2(lb + loc + jnp.concatenate(cs, axis=0)).astype(BF16)
    for h in range(H_SB):
        wh = jnp.concatenate([w[(h * G + p) * SROWS:(h * G + p + 1) * SROWS] for p in range(G)], axis=1)
        vt = jnp.concatenate([vc_refs[s][h] for s in order], axis=1).astype(BF16)
        acc_scr[h] = acc_scr[h] + _dot_nt(wh, vt)

    @pl.when(j == pl.num_programs(1) - 1)
    def _():
        for h in range(H_SB):
            o_ref[0, :, D_HEAD * h:D_HEAD * (h + 1)] = acc_scr[h]


def _sb_sample(pt, q, kn, vn, kc, vc):
    B, NP = pt.shape
    G = _pages_per_step(NP)
    page = lambda s: pl.BlockSpec((None, H_SB, D_HEAD, PAGE),
                                  lambda b, j, pt: (pt[b, NP - 1 - (j * G + s)], 0, 0, 0))
    grid_spec = pltpu.PrefetchScalarGridSpec(
        num_scalar_prefetch=1,
        grid=(B, NP // G),
        in_specs=([pl.BlockSpec((1, SROWS, 512), lambda b, j, pt: (b, 0, 0)),
                   pl.BlockSpec((1, PAGE, 512), lambda b, j, pt: (b, 0, 0)),
                   pl.BlockSpec((1, PAGE, 512), lambda b, j, pt: (b, 0, 0))]
                  + [page(s) for s in range(G)] * 2),
        out_specs=pl.BlockSpec((1, SROWS, 512), lambda b, j, pt: (b, 0, 0)),
        scratch_shapes=[pltpu.VMEM((H_SB, SROWS, LANES), F32), pltpu.VMEM((H_SB, SROWS, D_HEAD), F32)])
    return pl.pallas_call(
        functools.partial(_sb_sample_kernel, G=G),
        grid_spec=grid_spec,
        out_shape=jax.ShapeDtypeStruct((B, SROWS, 512), F32),
        compiler_params=_cparams("parallel", "arbitrary"),
        name="sb_sample",
    )(pt, q, kn, vn, *([kc] * G), *([vc] * G))


_IDX_SCALE = D_IDX ** -0.5 * H_IDX ** -0.5


def _score_key(score):
    score = jnp.where(score == 0.0, 0.0, score)
    bits = pltpu.bitcast(score, I32)
    return bits ^ ((bits >> 31) & INT_MAX)


def _topk_threshold(count_ge, count_eq_lt, rows, real_rows, n_sel, idx_bits, jl_ref):
    def bit_body(it, tu):
        cand = tu | jnp.left_shift(jnp.int32(1), 31 - it)
        return jnp.where(count_ge(cand ^ INT_MIN) >= n_sel, cand, tu)

    tu = lax.fori_loop(0, 32, bit_body, jnp.zeros((rows, 1), I32))
    thr = tu ^ INT_MIN
    n_ge = count_ge(thr)
    n_gt = jnp.where(thr == INT_MAX, 0, count_ge(thr + 1))
    need = n_sel - n_gt
    tied = (n_ge > n_sel) & (thr != INT_MIN)
    if real_rows < rows:
        tied = tied & (lax.broadcasted_iota(I32, (rows, 1), 0) < real_rows)
    jl_ref[...] = jnp.full((rows, 1), INT_MAX, I32)

    @pl.when(jnp.max(tied.astype(I32)) > 0)
    def _():
        def idx_body(it, lo):
            cand = lo | jnp.left_shift(jnp.int32(1), idx_bits - 1 - it)
            return jnp.where(count_eq_lt(thr, cand) < need, cand, lo)
        lo = lax.fori_loop(0, idx_bits, idx_body, jnp.zeros((rows, 1), I32))
        jl_ref[...] = jnp.where(tied, lo, INT_MAX)

    return thr


def _dsa_prompt_kernel(qi_ref, wi_ref, ki_ref, qb_ref, kb_ref, vb_ref, o_ref, key_scr, jl_scr,
                       *, tq, n_sel, idx_bits, AG):
    i = pl.program_id(1)
    tk = LANES
    R = tq // tk
    cm = R * i
    cl = cm + R - 1
    lane = lax.broadcasted_iota(I32, (1, LANES), 1)
    lo_half = lane < 64
    row = lax.broadcasted_iota(I32, (tq, tk), 0)
    col = lax.broadcasted_iota(I32, (tq, tk), 1)
    diag_ok = [col + r * tk <= row for r in range(R)]

    qs = []
    for h in range(H_IDX):
        blk = qi_ref[:, LANES * (h // 2):LANES * (h // 2 + 1)]
        qs.append(jnp.where(lo_half if h % 2 == 0 else ~lo_half, blk, 0.0))
    qstack = jnp.concatenate(qs, axis=0).astype(BF16)
    wb = [jnp.broadcast_to(wi_ref[:, h:h + 1], (tq, tk)) for h in range(H_IDX)]

    def score_chunks(c0, n, masks):
        off = pl.multiple_of(c0 * tk, tk)
        kc = ki_ref[pl.ds(off, n * tk), :].astype(BF16)
        s = _dot_nt(qstack, kc)
        relu = [[jnp.maximum(s[h * tq:(h + 1) * tq, u * tk:(u + 1) * tk], 0.0) for h in range(H_IDX)]
                for u in range(n)]
        score = [wb[0] * r[0] for r in relu]
        for h in range(1, H_IDX):
            score = [sc + wb[h] * r[h] for sc, r in zip(score, relu)]
        for u in range(n):
            key = _score_key(score[u] * _IDX_SCALE)
            if masks is not None:
                key = jnp.where(masks[u], key, INT_MIN)
            key_scr[c0 + u] = key

    def score_group(g, carry):
        score_chunks(g * AG, AG, None)
        return carry

    def score_single(c, carry):
        score_chunks(c, 1, None)
        return carry

    lax.fori_loop(0, cm // AG, score_group, 0)
    lax.fori_loop((cm // AG) * AG, cm, score_single, 0)
    score_chunks(cm, R, diag_ok)
    for u in range(1, AG):
        @pl.when((cl % AG) + u < AG)
        def _():
            key_scr[cl + u] = jnp.full((tq, tk), INT_MIN, I32)

    def count_ge(cand):
        def body(g, acc):
            for u in range(AG):
                acc = acc + (key_scr[g * AG + u] >= cand).astype(F32)
            return acc
        acc = lax.fori_loop(0, cl // AG + 1, body, jnp.zeros((tq, tk), F32))
        return jnp.sum(acc, axis=1, keepdims=True).astype(I32)

    def count_eq_lt(thr, jcand):
        def body(g, acc):
            for u in range(AG):
                c = g * AG + u
                acc = acc + ((key_scr[c] == thr) & (col + c * tk < jcand)).astype(F32)
            return acc
        acc = lax.fori_loop(0, cl // AG + 1, body, jnp.zeros((tq, tk), F32))
        return jnp.sum(acc, axis=1, keepdims=True).astype(I32)

    thr = _topk_threshold(count_ge, count_eq_lt, tq, tq, n_sel, idx_bits, jl_scr)
    jl = jl_scr[...]

    qs = []
    for h in range(H_DSA):
        blk = qb_ref[:, LANES * (h // 2):LANES * (h // 2 + 1)] * ATT_SCALE
        g = h // (H_DSA // KV_DSA)
        if (h % 2) != g:
            blk = pltpu.roll(blk, 64, 1)
        qs.append(jnp.where(lo_half if g == 0 else ~lo_half, blk, 0.0))
    qstack2 = jnp.concatenate(qs, axis=0).astype(BF16)

    ta = AG * tk
    rowa = lax.broadcasted_iota(I32, (tq, ta), 0)
    cola = lax.broadcasted_iota(I32, (tq, ta), 1)

    def att_group(c, carry, last):
        m, l, acc = carry
        off = pl.multiple_of(c * ta, ta)
        kblk = jnp.concatenate([key_scr[c * AG + u] for u in range(AG)], axis=1)
        sel = (kblk > thr) | ((kblk == thr) & (cola + off <= jl))
        if last:
            sel = sel & (cola + off <= rowa + i * tq)
        z = _dot_nt(qstack2, kb_ref[pl.ds(off, ta), :].astype(BF16))
        z = jnp.where(sel[None], z.reshape(H_DSA, tq, ta), NEG_BIG).reshape(H_DSA * tq, ta)
        m_new = jnp.maximum(m, jnp.max(z, axis=1, keepdims=True))
        alpha = jnp.exp2(m - m_new)
        p = jnp.exp2(z - m_new)
        l = alpha * l + jnp.sum(p, axis=1, keepdims=True)
        acc = alpha * acc + _dot(p.astype(BF16), vb_ref[pl.ds(off, ta), :].astype(BF16))
        return m_new, l, acc

    carry = (jnp.full((H_DSA * tq, 1), NEG_BIG, F32), jnp.zeros((H_DSA * tq, 1), F32),
             jnp.zeros((H_DSA * tq, LANES), F32))
    full_groups = (i * tq) // ta
    carry = lax.fori_loop(0, full_groups, lambda c, cr: att_group(c, cr, False), carry)
    for r in range(max(1, tq // ta)):
        carry = att_group(full_groups + r, carry, True)
    m, l, acc = carry
    out = acc / l
    for mblk in range(4):
        parts = []
        for h in (2 * mblk, 2 * mblk + 1):
            o_h = out[h * tq:(h + 1) * tq]
            if (h % 2) != h // (H_DSA // KV_DSA):
                o_h = pltpu.roll(o_h, 64, 1)
            parts.append(o_h)
        o_ref[:, LANES * mblk:LANES * (mblk + 1)] = jnp.where(lo_half, parts[0], parts[1])


def _dsa_prompt(qi, wi, ki2, qb, kb, vb, B, T, tq, n_sel):
    nq = T // tq
    qrow = lambda b, i: (b * nq + i, 0)
    full = lambda b, i: (b, 0)
    AG = 4 if (T // LANES) % 4 == 0 else 1
    return pl.pallas_call(
        functools.partial(_dsa_prompt_kernel, tq=tq, n_sel=n_sel, idx_bits=max(1, (T - 1).bit_length()), AG=AG),
        grid=(B, nq),
        in_specs=[pl.BlockSpec((tq, 512), qrow), pl.BlockSpec((tq, LANES), qrow),
                  pl.BlockSpec((T, LANES), full), pl.BlockSpec((tq, 512), qrow),
                  pl.BlockSpec((T, LANES), full), pl.BlockSpec((T, LANES), full)],
        out_specs=pl.BlockSpec((tq, 512), qrow),
        out_shape=jax.ShapeDtypeStruct(qb.shape, F32),
        scratch_shapes=[pltpu.VMEM((T // LANES, tq, LANES), I32), pltpu.VMEM((tq, 1), I32)],
        compiler_params=_cparams("parallel", "arbitrary"),
        name="dsa_prompt",
    )(qi, wi, ki2, qb, kb, vb)


def _dsa_score_sample_kernel(pt_ref, q_ref, w_ref, kn_ref, *rest, G, n_pages, n_real, n_sel, idx_bits):
    kc_refs = rest[:G]
    key_ref, thr_ref, jl_ref = rest[G:]
    j = pl.program_id(1)
    qrow = lax.broadcasted_iota(I32, (SROWS, PAGE), 0)
    kcol = lax.broadcasted_iota(I32, (SROWS, PAGE), 1)
    q = q_ref[0].astype(BF16)
    w = w_ref[0]

    def score_keys(s):
        s = jnp.maximum(s, 0.0) * w
        sc = s[0:SROWS]
        for h in range(1, H_IDX):
            sc = sc + s[h * SROWS:(h + 1) * SROWS]
        return _score_key(sc * _IDX_SCALE)

    @pl.when(j < n_pages // G)
    def _():
        kt = jnp.concatenate([kc_refs[s][...] for s in range(G)], axis=1).astype(BF16)
        key = score_keys(_dot(q, kt))
        for s in range(G):
            key_ref[0, j * G + s] = key[:, PAGE * s:PAGE * (s + 1)]

    @pl.when(j == n_pages // G)
    def _():
        key = score_keys(_dot_nt(q, kn_ref[0].astype(BF16)))
        key_ref[0, n_pages] = jnp.where(kcol <= qrow, key, INT_MIN)
        pos = lax.broadcasted_iota(I32, (G, SROWS, PAGE), 0) * PAGE + kcol[None]

        def count_ge(cand):
            def body(g, acc):
                blk = key_ref[0, pl.ds(g * G, G)]
                return acc + jnp.sum((blk >= cand[None]).astype(F32), axis=0)
            acc = lax.fori_loop(0, n_pages // G, body, jnp.zeros((SROWS, PAGE), F32))
            acc = acc + (key_ref[0, n_pages] >= cand).astype(F32)
            return jnp.sum(acc, axis=1, keepdims=True).astype(I32)

        def count_eq_lt(thr, jcand):
            def body(g, acc):
                blk = key_ref[0, pl.ds(g * G, G)]
                hit = (blk == thr[None]) & (pos + g * (G * PAGE) < jcand[None])
                return acc + jnp.sum(hit.astype(F32), axis=0)
            acc = lax.fori_loop(0, n_pages // G, body, jnp.zeros((SROWS, PAGE), F32))
            last = (key_ref[0, n_pages] == thr) & (kcol + n_pages * PAGE < jcand)
            return jnp.sum(acc + last.astype(F32), axis=1, keepdims=True).astype(I32)

        thr_ref[0] = _topk_threshold(count_ge, count_eq_lt, SROWS, n_real, n_sel, idx_bits, jl_ref.at[0])


def _dsa_score_sample(pt, q, w, kn, kc, n_real, n_sel):
    B, NP = pt.shape
    G = _pages_per_step(NP)
    L = (NP + 1) * PAGE
    page = lambda s: pl.BlockSpec((None, D_IDX, PAGE),
                                  lambda b, j, pt: (pt[b, jnp.minimum(j * G + s, NP - 1)], 0, 0))
    grid_spec = pltpu.PrefetchScalarGridSpec(
        num_scalar_prefetch=1,
        grid=(B, NP // G + 1),
        in_specs=([pl.BlockSpec((1, H_IDX * SROWS, D_IDX), lambda b, j, pt: (b, 0, 0)),
                   pl.BlockSpec((1, H_IDX * SROWS, 1), lambda b, j, pt: (b, 0, 0)),
                   pl.BlockSpec((1, PAGE, D_IDX), lambda b, j, pt: (b, 0, 0))]
                  + [page(s) for s in range(G)]),
        out_specs=[pl.BlockSpec((1, NP + 1, SROWS, PAGE), lambda b, j, pt: (b, 0, 0, 0)),
                   pl.BlockSpec((1, SROWS, 1), lambda b, j, pt: (b, 0, 0)),
                   pl.BlockSpec((1, SROWS, 1), lambda b, j, pt: (b, 0, 0))])
    return pl.pallas_call(
        functools.partial(_dsa_score_sample_kernel, G=G, n_pages=NP, n_real=n_real, n_sel=n_sel,
                          idx_bits=max(1, (L - 1).bit_length())),
        grid_spec=grid_spec,
        out_shape=[jax.ShapeDtypeStruct((B, NP + 1, SROWS, PAGE), I32),
                   jax.ShapeDtypeStruct((B, SROWS, 1), I32),
                   jax.ShapeDtypeStruct((B, SROWS, 1), I32)],
        compiler_params=_cparams("parallel", "arbitrary"),
        name="dsa_score_sample",
    )(pt, q, w, kn, *([kc] * G))


_GR = (H_DSA // KV_DSA) * SROWS


def _dsa_att_sample_kernel(pt_ref, q_ref, key_ref, thr_ref, jl_ref, kn_ref, vn_ref, *rest, G, n_pages):
    kc_refs, vc_refs = rest[:G], rest[G:2 * G]
    o_ref, m_scr, l_scr, acc_scr = rest[2 * G:]
    j = pl.program_id(1)
    qrow = lax.broadcasted_iota(I32, (SROWS, PAGE), 0)
    kcol = lax.broadcasted_iota(I32, (SROWS, PAGE), 1)
    nrep = H_DSA // KV_DSA

    @pl.when(j == 0)
    def _():
        m_scr[...] = jnp.full(m_scr.shape, NEG_BIG, F32)
        l_scr[...] = jnp.zeros(l_scr.shape, F32)
        acc_scr[...] = jnp.zeros(acc_scr.shape, F32)

    def attend(z, v, g, sel, v_transposed=True):
        n = z.shape[1]
        z = jnp.where(sel[None], z.reshape(nrep, SROWS, n), NEG_BIG).reshape(_GR, n)
        m = m_scr[g][:, 0:1]
        l = l_scr[g][:, 0:1]
        m_new = jnp.maximum(m, jnp.max(z, axis=1, keepdims=True))
        alpha = jnp.exp2(m - m_new)
        p = jnp.exp2(z - m_new)
        l = alpha * l + jnp.sum(p, axis=1, keepdims=True)
        pv =_dot_nt(p.astype(BF16), v.astype(BF16)) if v_transposed else _dot(p.astype(BF16), v.astype(BF16))
        acc_scr[g] = alpha * acc_scr[g] + pv
        m_scr[g] = jnp.broadcast_to(m_new, (_GR, LANES))
        l_scr[g] = jnp.broadcast_to(l, (_GR, LANES))

    def qgroup(g):
        return (q_ref[0, g] * ATT_SCALE).astype(BF16)

    thr = thr_ref[0]
    jl = jl_ref[0]

    @pl.when(j < n_pages // G)
    def _():
        sels = []
        for s in range(G):
            kblk = key_ref[0, j * G + s]
            sels.append((kblk > thr) | ((kblk == thr) & (kcol + (j * G + s) * PAGE <= jl)))
        sel = jnp.concatenate(sels, axis=1)
        for g in range(KV_DSA):
            kt = jnp.concatenate([kc_refs[s][g] for s in range(G)], axis=1).astype(BF16)
            vt = jnp.concatenate([vc_refs[s][g] for s in range(G)], axis=1)
            attend(_dot(qgroup(g), kt), vt, g, sel)

    @pl.when(j == n_pages // G)
    def _():
        off = n_pages * PAGE
        kblk = key_ref[0, n_pages]
        sel = ((kblk > thr) | ((kblk == thr) & (kcol + off <= jl))) & (kcol <= qrow)
        for g in range(KV_DSA):
            kn = kn_ref[0, :, D_HEAD * g:D_HEAD * (g + 1)].astype(BF16)
            attend(_dot_nt(qgroup(g), kn), vn_ref[0, :, D_HEAD * g:D_HEAD * (g + 1)], g, sel, v_transposed=False)
        for g in range(KV_DSA):
            o_ref[0, g] = acc_scr[g] / l_scr[g][:, 0:1]


def _dsa_att_sample(pt, q, keys, thr, jl, kn, vn, kc, vc):
    B, NP = pt.shape
    G = _pages_per_step(NP)
    page = lambda s: pl.BlockSpec((None, KV_DSA, D_HEAD, PAGE),
                                  lambda b, j, pt: (pt[b, jnp.minimum(j * G + s, NP - 1)], 0, 0, 0))
    per_b3 = lambda b, j, pt: (b, 0, 0)
    grid_spec = pltpu.PrefetchScalarGridSpec(
        num_scalar_prefetch=1,
        grid=(B, NP // G + 1),
        in_specs=([pl.BlockSpec((1, KV_DSA, _GR, D_HEAD), lambda b, j, pt: (b, 0, 0, 0)),
                   pl.BlockSpec((1, NP + 1, SROWS, PAGE), lambda b, j, pt: (b, 0, 0, 0)),
                   pl.BlockSpec((1, SROWS, 1), per_b3),
                   pl.BlockSpec((1, SROWS, 1), per_b3),
                   pl.BlockSpec((1, PAGE, LANES), per_b3),
                   pl.BlockSpec((1, PAGE, LANES), per_b3)]
                  + [page(s) for s in range(G)] * 2),
        out_specs=pl.BlockSpec((1, KV_DSA, _GR, D_HEAD), lambda b, j, pt: (b, 0, 0, 0)),
        scratch_shapes=[pltpu.VMEM((KV_DSA, _GR, LANES), F32), pltpu.VMEM((KV_DSA, _GR, LANES), F32),
                        pltpu.VMEM((KV_DSA, _GR, D_HEAD), F32)])
    return pl.pallas_call(
        functools.partial(_dsa_att_sample_kernel, G=G, n_pages=NP),
        grid_spec=grid_spec,
        out_shape=jax.ShapeDtypeStruct((B, KV_DSA, _GR, D_HEAD), F32),
        compiler_params=_cparams("parallel", "arbitrary"),
        name="dsa_att_sample",
    )(pt, q, keys, thr, jl, kn, vn, *([kc] * G), *([vc] * G))


def _mm_norm_res_kernel(*refs, n):
    a_refs, w_refs = refs[:n], refs[n:2 * n]
    g_ref, x_ref, o_ref = refs[2 * n:]
    acc = _dot(a_refs[0][...].astype(BF16), w_refs[0][...])
    for a_ref, w_ref in zip(a_refs[1:], w_refs[1:]):
        acc = acc + _dot(a_ref[...].astype(BF16), w_ref[...])
    o_ref[...] = x_ref[...] + _rms(acc, g_ref[...])


def _mm_norm_res(a_list, w_list, g, x, tm, name):
    M = x.shape[0]
    n = len(a_list)
    row = lambda i: (i, 0)
    const = lambda i: (0, 0)
    return pl.pallas_call(
        functools.partial(_mm_norm_res_kernel, n=n),
        grid=(M // tm,),
        in_specs=([pl.BlockSpec((tm, a.shape[1]), row) for a in a_list]
                  + [pl.BlockSpec(w.shape, const) for w in w_list]
                  + [pl.BlockSpec((1, D_MODEL), const), pl.BlockSpec((tm, D_MODEL), row)]),
        out_specs=pl.BlockSpec((tm, D_MODEL), row),
        out_shape=jax.ShapeDtypeStruct((M, D_MODEL), F32),
        compiler_params=_cparams("parallel"),
        name=name,
    )(*a_list, *w_list, g, x)


_GELU_C = math.sqrt(2.0 / math.pi)
FFN_SUB = 256


def _gelu_tanh(x):
    return x * (0.5 * (1.0 + jnp.tanh(_GELU_C * (x + 0.044715 * (x * x * x)))))


def _ffn_up_kernel(x_ref, g_ref, wg_ref, wv_ref, cwg_ref, cwv_ref, cbg_ref, cbv_ref, pg_ref, pv_ref,
                   act_ref, og_ref, ov_ref, eg_scr, ev_scr, *, tm, T, streamed):
    i = pl.program_id(1)
    h = _rms(x_ref[...], g_ref[...]).astype(BF16)
    trow = lax.broadcasted_iota(I32, (tm, 1), 0) % T

    tn = act_ref.shape[1]
    if streamed:
        @pl.when((i * tm) % T == 0)
        def _():
            eg_scr[6:8, :] = pg_ref[...]
            ev_scr[6:8, :] = pv_ref[...]
    else:
        eg_scr[0:8, :] = jnp.zeros((8, tn), F32)
        ev_scr[0:8, :] = jnp.zeros((8, tn), F32)

    def conv(u, cs, cw_ref, cb_ref, p_ref, o_ref, e_scr):
        e_scr[8:8 + tm, cs] = u
        if streamed:
            o_ref[:, cs] = u[tm - 2:tm, :]
            u1 = e_scr[7:7 + tm, cs]
            u2 = e_scr[6:6 + tm, cs]
            e_scr[6:8, cs] = u[tm - 2:tm, :]
        else:
            o_ref[:, cs] = u
            u1 = jnp.where(trow >= 1, e_scr[7:7 + tm, cs], p_ref[0, :, cs])
            u2 = jnp.where(trow >= 2, e_scr[6:6 + tm, cs], p_ref[1, :, cs])
        return cb_ref[:, cs] + cw_ref[0:1, cs] * u2 + cw_ref[1:2, cs] * u1 + cw_ref[2:3, cs] * u

    subs = [slice(c0, min(c0 + FFN_SUB, tn)) for c0 in range(0, tn, FFN_SUB)]
    dots = lambda cs: (_dot(h, wg_ref[:, cs]), _dot(h, wv_ref[:, cs]))
    cur = dots(subs[0])
    for n, cs in enumerate(subs):
        nxt = dots(subs[n + 1]) if n + 1 < len(subs) else None
        gate = conv(cur[0], cs, cwg_ref, cbg_ref, pg_ref, og_ref, eg_scr)
        val = conv(cur[1], cs, cwv_ref, cbv_ref, pv_ref, ov_ref, ev_scr)
        act_ref[:, cs] = (_gelu_tanh(gate) * val).astype(BF16)
        cur = nxt


def _ffn_up(x, g, w_up, conv_w, conv_b, prev, T, tm, tn):
    M = x.shape[0]
    nj = D_FF // tn
    streamed = T >= tm
    tpb = max(T // tm, 1)
    if streamed:
        p_spec = lambda off: pl.BlockSpec((None, 2, tn), lambda j, i: (i // tpb, 0, j + off))
        o_shape = jax.ShapeDtypeStruct((M // T, 2, D_FF), F32)
        o_spec = pl.BlockSpec((None, 2, tn), lambda j, i: (i // tpb, 0, j))
    else:
        p_spec = lambda off: pl.BlockSpec((2, tm, tn), lambda j, i: (0, i, j + off))
        o_shape = jax.ShapeDtypeStruct((M, D_FF), F32)
        o_spec = pl.BlockSpec((tm, tn), lambda j, i: (i, j))
    col = lambda rows, off: pl.BlockSpec((rows, tn), lambda j, i: (0, j + off))
    return pl.pallas_call(
        functools.partial(_ffn_up_kernel, tm=tm, T=T, streamed=streamed),
        grid=(nj, M // tm),
        in_specs=[pl.BlockSpec((tm, D_MODEL), lambda j, i: (i, 0)),
                  pl.BlockSpec((1, D_MODEL), lambda j, i: (0, 0)),
                  col(D_MODEL, 0), col(D_MODEL, nj), col(3, 0), col(3, nj), col(1, 0), col(1, nj),
                  p_spec(0), p_spec(nj)],
        out_specs=[pl.BlockSpec((tm, tn), lambda j, i: (i, j)), o_spec, o_spec],
        out_shape=[jax.ShapeDtypeStruct((M, D_FF), BF16), o_shape, o_shape],
        scratch_shapes=[pltpu.VMEM((tm + 8, tn), F32), pltpu.VMEM((tm + 8, tn), F32)],
        compiler_params=_cparams("parallel", "arbitrary"),
        name="ffn_up",
    )(x, g, w_up, w_up, conv_w, conv_w, conv_b, conv_b, prev, prev)


def _rk_pre_kernel(x_ref, g_ref, sp_ref, mix_ref, wr_ref, wk_ref, wv_ref, w1_ref, w2_ref, a1_ref, a2_ref,
                   g1_ref, g2_ref, vec_ref,
                   hn_ref, r_ref, wl_ref, kf_ref, v_ref, av_ref, bv_ref, gg_ref, e_scr, *, tm, T, streamed):
    i = pl.program_id(0)
    h = _rms(x_ref[...], g_ref[...])
    hn_ref[...] = h
    e_scr[8:8 + tm, :] = h
    if streamed:
        @pl.when((i * tm) % T == 0)
        def _():
            e_scr[7:8, :] = sp_ref[...]
        xp = e_scr[7:7 + tm, :]
        e_scr[7:8, :] = h[tm - 1:tm, :]
    else:
        e_scr[0:8, :] = jnp.zeros((8, D_MODEL), F32)
        trow = lax.broadcasted_iota(I32, (tm, 1), 0) % T
        xp = jnp.where(trow >= 1, e_scr[7:7 + tm, :], sp_ref[...])
    xx = xp - h
    mixed = lambda j: (h + xx * mix_ref[j:j + 1, :]).astype(BF16)
    w0, a0, k_k, k_a = vec_ref[0:1, :], vec_ref[1:2, :], vec_ref[2:3, :], vec_ref[3:4, :]
    r = _dot(mixed(0), wr_ref[...])
    k = _dot(mixed(2), wk_ref[...])
    v = _dot(mixed(3), wv_ref[...])
    lw = w0 + _dot(jnp.tanh(_dot(mixed(1), w1_ref[...])).astype(BF16), w2_ref[...])
    w_log = -(jnp.maximum(-lw, 0.0) + jnp.log(1.0 + jnp.exp(-jnp.abs(lw)))) - 0.5
    wl_ref[...] = -jnp.exp(w_log)
    a = jax.nn.sigmoid(a0 + _dot(_dot(mixed(4), a1_ref[...]).astype(BF16), a2_ref[...]))
    gg_ref[...] = _dot(jax.nn.sigmoid(_dot(mixed(5), g1_ref[...])).astype(BF16), g2_ref[...])
    kk = k * k_k
    bd = _seg_ones()
    kk = kk / jnp.maximum(jnp.sqrt(_seg64_sum(kk * kk, bd)), 1e-12)
    r_ref[...] = r
    v_ref[...] = v
    kf_ref[...] = k * (1.0 + (a - 1.0) * k_a)
    av_ref[...] = -kk
    bv_ref[...] = kk * a


def _rk_pre(x, g, sp, mix, wr, wk, wv, w1, w2, a1, a2, g1, g2, vecs, T, tm):
    M = x.shape[0]
    streamed = T >= tm
    tpb = max(T // tm, 1)
    row = lambda i: (i, 0)
    const = lambda i: (0, 0)
    sp_spec = (pl.BlockSpec((None, 1, D_MODEL), lambda i: (i // tpb, 0, 0)) if streamed
               else pl.BlockSpec((tm, D_MODEL), row))
    full = lambda a: pl.BlockSpec(a.shape, const)
    return pl.pallas_call(
        functools.partial(_rk_pre_kernel, tm=tm, T=T, streamed=streamed),
        grid=(M // tm,),
        in_specs=[pl.BlockSpec((tm, D_MODEL), row), pl.BlockSpec((1, D_MODEL), const), sp_spec,
                  full(mix), full(wr), full(wk), full(wv), full(w1), full(w2), full(a1), full(a2),
                  full(g1), full(g2), full(vecs)],
        out_specs=[pl.BlockSpec((tm, D_MODEL), row)] * 8,
        out_shape=[jax.ShapeDtypeStruct((M, D_MODEL), F32)] * 8,
        scratch_shapes=[pltpu.VMEM((tm + 8, D_MODEL), F32)],
        compiler_params=_cparams("arbitrary"),
        name="rk_pre",
    )(x, g, sp, mix, wr, wk, wv, w1, w2, a1, a2, g1, g2, vecs)


def _bdot(a, b):
    return _dot(a.astype(BF16), b.astype(BF16))


def _rk_scan_kernel(r_ref, wl_ref, k_ref, v_ref, a_ref, b_ref, s0_ref, y_ref, sf_ref, s_scr, *, C, npair):
    c = pl.program_id(2)

    @pl.when(c == 0)
    def _():
        s_scr[...] = s0_ref[0]

    lane = lax.broadcasted_iota(I32, (1, LANES), 1)
    m0 = lane < 64
    rr = lax.broadcasted_iota(I32, (C, C), 0)
    cc = lax.broadcasted_iota(I32, (C, C), 1)
    tri_incl = (cc <= rr).astype(BF16)
    r2 = lax.broadcasted_iota(I32, (2 * C, 2 * C), 0)
    c2 = lax.broadcasted_iota(I32, (2 * C, 2 * C), 1)
    strict = (r2 % C) > (c2 % C)
    incl = (r2 % C) >= (c2 % C)
    incl2 = jnp.concatenate([incl, incl], axis=1)
    eye = (r2 == c2).astype(F32)
    n_double = max(int(math.log2(C)) - 1, 0)

    def stack2(z):
        return jnp.concatenate([jnp.where(m0, z, 0.0), jnp.where(m0, 0.0, z)], axis=0)

    pairs = range(npair)
    sls = [slice(LANES * p, LANES * (p + 1)) for p in pairs]
    wl = [wl_ref[:, sl] for sl in sls]
    ld = []
    for p in pairs:
        wl_hi, wl_lo = _split(wl[p])
        ld.append(_dot(tri_incl, wl_hi) + _dot(tri_incl, wl_lo))
    dfull = [jnp.exp(x) for x in ld]
    dinv = [jnp.exp(-x) for x in ld]
    As = [stack2(a_ref[:, sls[p]] * jnp.exp(ld[p] - wl[p])).astype(BF16) for p in pairs]
    Bs = [stack2(b_ref[:, sls[p]] * dinv[p]) for p in pairs]
    Ks = [stack2(k_ref[:, sls[p]] * dinv[p]) for p in pairs]
    Rs = [stack2(r_ref[:, sls[p]] * dfull[p]).astype(BF16) for p in pairs]
    Vs = [stack2(v_ref[:, sls[p]]) for p in pairs]
    Vb = [x.astype(BF16) for x in Vs]
    BK = [jnp.concatenate([Bs[p], Ks[p]], axis=0).astype(BF16) for p in pairs]
    G1 = [_dot_nt(As[p], BK[p]) for p in pairs]
    G2 = [_dot_nt(Rs[p], BK[p]) for p in pairs]
    Lab = [jnp.where(strict, g[:, :2 * C], 0.0) for g in G1]
    Lak = [jnp.where(strict, g[:, 2 * C:], 0.0).astype(BF16) for g in G1]
    Mr = [jnp.where(incl2, g, 0.0).astype(BF16) for g in G2]
    Tm = [eye + x for x in Lab]
    if n_double > 0:
        Pb = [x.astype(BF16) for x in Lab]
        Pb = [_dot(x, x).astype(BF16) for x in Pb]
        for rnd in range(n_double):
            Tb = [x.astype(BF16) for x in Tm]
            if rnd + 1 < n_double:
                X = [_dot(Pb[p], jnp.concatenate([Pb[p], Tb[p]], axis=1)) for p in pairs]
                Pb = [x[:, :2 * C].astype(BF16) for x in X]
                Tm = [Tm[p] + X[p][:, 2 * C:] for p in pairs]
            else:
                Tm = [Tm[p] + _dot(Pb[p], Tb[p]) for p in pairs]
    S = [s_scr[p] for p in pairs]
    Sb = [x.astype(BF16) for x in S]
    rhs = [_dot_nt(As[p], Sb[p]) + _dot(Lak[p], Vb[p]) for p in pairs]
    U = [_bdot(Tm[p], rhs[p]) for p in pairs]
    UV = [jnp.concatenate([U[p], Vs[p]], axis=0) for p in pairs]
    UVb = [x.astype(BF16) for x in UV]
    Ys = [_dot_nt(Rs[p], Sb[p]) + _dot(Mr[p], UVb[p]) for p in pairs]
    upd = [_dot(UV[p].T.astype(BF16), BK[p]) for p in pairs]
    for p in pairs:
        y_ref[:, sls[p]] = Ys[p][:C] + Ys[p][C:]
        s_scr[p] = (S[p] + upd[p]) * dfull[p][C - 1:C, :]

    @pl.when(c == pl.num_programs(2) - 1)
    def _():
        sf_ref[0] = s_scr[...]


def _rk_scan(r, wl, k, v, a, b, s0, B, T, C, npair):
    nc = T // C
    ng = 8 // npair
    blk = pl.BlockSpec((C, LANES * npair), lambda bb, g, c: (bb * nc + c, g))
    s_spec = pl.BlockSpec((1, npair, LANES, LANES), lambda bb, g, c: (bb, g, 0, 0))
    return pl.pallas_call(
        functools.partial(_rk_scan_kernel, C=C, npair=npair),
        grid=(B, ng, nc),
        in_specs=[blk] * 6 + [s_spec],
        out_specs=[blk, s_spec],
        out_shape=[jax.ShapeDtypeStruct(r.shape, F32), jax.ShapeDtypeStruct(s0.shape, F32)],
        scratch_shapes=[pltpu.VMEM((npair, LANES, LANES), F32)],
        compiler_params=_cparams("parallel", "parallel", "arbitrary"),
        name="rk_scan",
    )(r, wl, k, v, a, b, s0)


def _rk_post_kernel(y_ref, r_ref, kf_ref, v_ref, gg_ref, vec_ref, o_ref):
    bd = _seg_ones()
    ln_w, ln_b, r_k = vec_ref[0:1, :], vec_ref[1:2, :], vec_ref[2:3, :]
    y = y_ref[...]
    mu = _seg64_sum(y, bd) * (1.0 / RK_N)
    d = y - mu
    var = _seg64_sum(d * d, bd) * (1.0 / RK_N)
    yn = d * lax.rsqrt(var + RK_GN_EPS) * ln_w + ln_b
    bonus = _seg64_sum(r_ref[...] * kf_ref[...] * r_k, bd) * v_ref[...]
    o_ref[...] = ((yn + bonus) * gg_ref[...]).astype(BF16)


def _rk_post(y, r, kf, v, gg, vecs, tm):
    M = y.shape[0]
    row = lambda i: (i, 0)
    return pl.pallas_call(
        _rk_post_kernel,
        grid=(M // tm,),
        in_specs=[pl.BlockSpec((tm, D_MODEL), row)] * 5 + [pl.BlockSpec(vecs.shape, lambda i: (0, 0))],
        out_specs=pl.BlockSpec((tm, D_MODEL), row),
        out_shape=jax.ShapeDtypeStruct((M, D_MODEL), BF16),
        compiler_params=_cparams("parallel"),
        name="rk_post",
    )(y, r, kf, v, gg, vecs)


def _row_tile(M):
    return min(256, M)


def _pad_rows(a, n):
    return jnp.pad(a, ((0, 0), (0, n - a.shape[1])) + ((0, 0),) * (a.ndim - 2))


def _attn_layer_prompt(x, B, T, g_pre, w_in, w_outs, g_post):
    M = B * T
    tm = _row_tile(M)
    cos, sin = _rope_tables(jnp.arange(T, dtype=I32))
    qa, ka, va, qb, kb, vb, qi, ki2, wi = _attn_proj(x, g_pre, w_in, cos, sin, tm)
    n_sel = min(DSA_TOPK, T // 4)
    oa = _sb_prompt(qa, ka, va, B, T, min(256, T))
    ob = _dsa_prompt(qi, wi, ki2, qb, kb, vb, B, T, min(128, T), n_sel)
    x = _mm_norm_res([oa, ob], w_outs, g_post, x, tm, "attn_out")
    rows = (ka.reshape(1, B, T, H_SB, D_HEAD), va.reshape(1, B, T, H_SB, D_HEAD),
            kb.reshape(1, B, T, KV_DSA, D_HEAD), vb.reshape(1, B, T, KV_DSA, D_HEAD),
            ki2[:, :D_IDX].reshape(1, B, T, D_IDX))
    return x, rows


def _attn_layer_sample(x, B, T, P, page_table, caches, g_pre, w_in, w_outs, g_post):
    M = B * T
    cos, sin = _rope_tables(P + jnp.arange(T, dtype=I32))
    cos, sin = jnp.tile(cos, (B, 1)), jnp.tile(sin, (B, 1))
    qa, ka, va, qb, kb, vb, qi, ki2, wi = _attn_proj(x, g_pre, w_in, cos, sin, M)
    c_sb_k, c_sb_v, c_dsa_k, c_dsa_v, c_idx = caches
    n_pool = c_sb_k.shape[0]
    n_sel = min(DSA_TOPK, (P + T) // 4)
    b3 = lambda a: a.reshape(B, T, a.shape[-1])

    row_minor = lambda c: jnp.moveaxis(c, 1, -1)
    oa = _sb_sample(page_table, _pad_rows(b3(qa), SROWS), _pad_rows(b3(ka), PAGE), _pad_rows(b3(va), PAGE),
                    row_minor(c_sb_k), row_minor(c_sb_v))
    oa = oa[:, :T].reshape(M, H_SB * D_HEAD)

    qi_s = _pad_rows(b3(qi).reshape(B, T, H_IDX, D_IDX), SROWS).transpose(0, 2, 1, 3).reshape(B, H_IDX * SROWS, D_IDX)
    wi_s = _pad_rows(b3(wi)[:, :, :H_IDX], SROWS).transpose(0, 2, 1).reshape(B, H_IDX * SROWS, 1)
    keys, thr, jl = _dsa_score_sample(page_table, qi_s, wi_s, _pad_rows(b3(ki2)[:, :, :D_IDX], PAGE),
                                      row_minor(c_idx), T, n_sel)
    qb_s = _pad_rows(b3(qb).reshape(B, T, H_DSA, D_HEAD), SROWS).transpose(0, 2, 1, 3).reshape(B, KV_DSA, _GR, D_HEAD)
    ob = _dsa_att_sample(page_table, qb_s, keys, thr, jl, _pad_rows(b3(kb), PAGE), _pad_rows(b3(vb), PAGE),
                         row_minor(c_dsa_k), row_minor(c_dsa_v))
    ob = ob.reshape(B, H_DSA, SROWS, D_HEAD)[:, :, :T].transpose(0, 2, 1, 3).reshape(M, H_DSA * D_HEAD)

    x = _mm_norm_res([oa, ob], w_outs, g_post, x, M, "attn_out")
    rows = (ka.reshape(1, B, T, H_SB, D_HEAD), va.reshape(1, B, T, H_SB, D_HEAD),
            kb.reshape(1, B, T, KV_DSA, D_HEAD), vb.reshape(1, B, T, KV_DSA, D_HEAD),
            ki2[:, :D_IDX].reshape(1, B, T, D_IDX))
    return x, rows


def _pair_state(S):
    B = S.shape[0]
    S = S.reshape(B, 8, 2, RK_N, RK_N)
    z = jnp.zeros_like(S[:, :, 0])
    top = jnp.concatenate([S[:, :, 0], z], axis=-1)
    bot = jnp.concatenate([z, S[:, :, 1]], axis=-1)
    return jnp.concatenate([top, bot], axis=-2)


def _unpair_state(Sp):
    B = Sp.shape[0]
    return jnp.stack([Sp[:, :, :RK_N, :RK_N], Sp[:, :, RK_N:, RK_N:]], axis=2).reshape(B, RK_H, RK_N, RK_N)


RK_CHUNK = 64
RK_PAIRS = 8


def _rwkv_layer(x, B, T, shift_prev, S0, g_pre, prm, g_post):
    M = B * T
    tm = _row_tile(M)
    (mix, wr, wk, wv, wo, w1, w2, a1, a2, g1, g2, vec_pre, vec_post) = prm
    if T >= tm:
        sp = shift_prev.reshape(B, 1, D_MODEL)
    else:
        sp = jnp.repeat(shift_prev, T, axis=0)
    hn, r, wl, kf, v, av, bv, gg = _rk_pre(x, g_pre, sp, mix, wr, wk, wv, w1, w2, a1, a2, g1, g2, vec_pre, T, tm)
    Tp = -(-T // RK_CHUNK) * RK_CHUNK
    if Tp != T:
        padt = lambda a: _pad_rows(a.reshape(B, T, D_MODEL), Tp).reshape(B * Tp, D_MODEL)
        y, Sf = _rk_scan(padt(r), padt(wl), padt(kf), padt(v), padt(av), padt(bv), _pair_state(S0), B, Tp, RK_CHUNK, RK_PAIRS)
        y = y.reshape(B, Tp, D_MODEL)[:, :T].reshape(M, D_MODEL)
    else:
        y, Sf = _rk_scan(r, wl, kf, v, av, bv, _pair_state(S0), B, T, RK_CHUNK, RK_PAIRS)
    z = _rk_post(y, r, kf, v, gg, vec_post, tm)
    x = _mm_norm_res([z], [wo], g_post, x, tm, "rk_out")
    shift = hn.reshape(B, T, D_MODEL)[:, -1]
    return x, shift, _unpair_state(Sf)


def _ffn_layer(x, B, T, prev, g_pre, w_up, conv_w, conv_b, w_down, g_post):
    M = B * T
    tm = _row_tile(M)
    tn = D_FF // 2
    if T >= tm:
        act, cg, cv = _ffn_up(x, g_pre, w_up, conv_w, conv_b, prev, T, tm, tn)
        conv_state = jnp.concatenate([cg, cv], axis=-1)
    else:
        zeros = jnp.zeros((B, T, 2 * D_FF), F32)
        p1 = zeros.at[:, 0].set(prev[:, 1])
        p2 = zeros.at[:, 0].set(prev[:, 0]).at[:, 1].set(prev[:, 1])
        pp = jnp.stack([p1.reshape(M, -1), p2.reshape(M, -1)])
        act, ug, uv = _ffn_up(x, g_pre, w_up, conv_w, conv_b, pp, T, tm, tn)
        u = jnp.concatenate([ug, uv], axis=-1).reshape(B, T, 2 * D_FF)
        conv_state = jnp.concatenate([prev, u], axis=1)[:, -2:]
    x = _mm_norm_res([act], [w_down], g_post, x, tm, "ffn_down")
    return x, conv_state


def _forward(x_prompt, x_sample, cache_sb_k, cache_sb_v, cache_dsa_k, cache_dsa_v, cache_idx_k, page_table,
             state_wkv, state_shift, state_ffn_conv,
             norm_mix_pre, norm_mix_post, norm_ffn_pre, norm_ffn_post,
             att_w_in, att_w_out,
             rk_mix, rk_w_r, rk_w_k, rk_w_v, rk_w_o, rk_w0, rk_w1, rk_w2, rk_a0, rk_a1, rk_a2,
             rk_g1, rk_g2, rk_k_k, rk_k_a, rk_r_k, rk_ln_w, rk_ln_b,
             ffn_w_up, ffn_conv_w, ffn_conv_b, ffn_w_down):
    Bp, Tp, D = x_prompt.shape
    Bs, Ts, _ = x_sample.shape
    P = page_table.shape[1] * PAGE
    depth = norm_mix_pre.shape[0]
    xp = x_prompt.reshape(Bp * Tp, D)
    xs = x_sample.reshape(Bs * Ts, D)
    vrow = lambda a: a.reshape(1, -1)
    bf = lambda a: a.astype(BF16)
    att_p, att_s, wkv_p, wkv_s, sh_p, sh_s, cv_p, cv_s = [], [], [], [], [], [], [], []
    for i in range(depth):
        li = i // 2
        g_pre, g_post = vrow(norm_mix_pre[i]), vrow(norm_mix_post[i])
        if i % 2 == 0:
            w_in = _pack_w_in(att_w_in[li])
            w_outs = [bf(att_w_out[li][:H_SB * D_HEAD]), bf(att_w_out[li][H_SB * D_HEAD:])]
            caches = (cache_sb_k[li], cache_sb_v[li], cache_dsa_k[li], cache_dsa_v[li], cache_idx_k[li])
            xp, rows_p = _attn_layer_prompt(xp, Bp, Tp, g_pre, w_in, w_outs, g_post)
            xs, rows_s = _attn_layer_sample(xs, Bs, Ts, P, page_table, caches, g_pre, w_in, w_outs, g_post)
            att_p.append(rows_p)
            att_s.append(rows_s)
        else:
            vec_pre = jnp.stack([rk_w0[li], rk_a0[li], rk_k_k[li], rk_k_a[li]])
            vec_post = jnp.stack([rk_ln_w[li], rk_ln_b[li], rk_r_k[li].reshape(-1)])
            prm = (rk_mix[li], bf(rk_w_r[li]), bf(rk_w_k[li]), bf(rk_w_v[li]), bf(rk_w_o[li]),
                   bf(rk_w1[li]), bf(rk_w2[li]), bf(rk_a1[li]), bf(rk_a2[li]), bf(rk_g1[li]), bf(rk_g2[li]),
                   vec_pre, vec_post)
            xp, shp, Sp = _rwkv_layer(xp, Bp, Tp, jnp.zeros((Bp, D), F32), jnp.zeros((Bp, RK_H, RK_N, RK_N), F32),
                                      g_pre, prm, g_post)
            xs, shs, Ss = _rwkv_layer(xs, Bs, Ts, state_shift[li], state_wkv[li], g_pre, prm, g_post)
            wkv_p.append(Sp)
            wkv_s.append(Ss)
            sh_p.append(shp)
            sh_s.append(shs)
        f_pre, f_post = vrow(norm_ffn_pre[i]), vrow(norm_ffn_post[i])
        w_up, w_down = bf(ffn_w_up[i]), bf(ffn_w_down[i])
        cb = vrow(ffn_conv_b[i])
        xp, cp = _ffn_layer(xp, Bp, Tp, jnp.zeros((Bp, 2, 2 * D_FF), F32), f_pre, w_up, ffn_conv_w[i], cb, w_down, f_post)
        xs, cs = _ffn_layer(xs, Bs, Ts, state_ffn_conv[i], f_pre, w_up, ffn_conv_w[i], cb, w_down, f_post)
        cv_p.append(cp)
        cv_s.append(cs)
    cat = lambda rows, j: jnp.concatenate([r[j] for r in rows], axis=0)
    return (xp.reshape(Bp, Tp, D), xs.reshape(Bs, Ts, D),
            cat(att_p, 0), cat(att_s, 0), cat(att_p, 1), cat(att_s, 1),
            cat(att_p, 2), cat(att_s, 2), cat(att_p, 3), cat(att_s, 3),
            cat(att_p, 4), cat(att_s, 4),
            jnp.stack(wkv_p), jnp.stack(wkv_s), jnp.stack(sh_p), jnp.stack(sh_s),
            jnp.stack(cv_p), jnp.stack(cv_s))


def kernel(x_prompt, x_sample, cache_sb_k, cache_sb_v, cache_dsa_k, cache_dsa_v, cache_idx_k, page_table, state_wkv, state_shift, state_ffn_conv, norm_mix_pre, norm_mix_post, norm_ffn_pre, norm_ffn_post, att_w_in, att_w_out, rk_mix, rk_w_r, rk_w_k, rk_w_v, rk_w_o, rk_w0, rk_w1, rk_w2, rk_a0, rk_a1, rk_a2, rk_g1, rk_g2, rk_k_k, rk_k_a, rk_r_k, rk_ln_w, rk_ln_b, ffn_w_up, ffn_conv_w, ffn_conv_b, ffn_w_down):
    return _forward(x_prompt, x_sample, cache_sb_k, cache_sb_v, cache_dsa_k, cache_dsa_v, cache_idx_k, page_table,
                    state_wkv, state_shift, state_ffn_conv,
                    norm_mix_pre, norm_mix_post, norm_ffn_pre, norm_ffn_post,
                    att_w_in, att_w_out,
                    rk_mix, rk_w_r, rk_w_k, rk_w_v, rk_w_o, rk_w0, rk_w1, rk_w2, rk_a0, rk_a1, rk_a2,
                    rk_g1, rk_g2, rk_k_k, rk_k_a, rk_r_k, rk_ln_w, rk_ln_b,
                    ffn_w_up, ffn_conv_w, ffn_conv_b, ffn_w_down)
```

```python
import functools
import math

import jax
import jax.numpy as jnp
from jax import lax
from jax.experimental import pallas as pl
from jax.experimental.pallas import tpu as pltpu

F32 = jnp.float32
BF16 = jnp.bfloat16
I32 = jnp.int32

D_MODEL = 1024
D_HEAD = 64
H_SB = 8
H_DSA = 8
KV_DSA = 2
H_IDX = 8
D_IDX = 64
DSA_TOPK = 256
PAGE = 128
ROPE_THETA = 10000.0
RK_N = 64
RK_H = D_MODEL // RK_N
RK_GN_EPS = 64e-5
D_FF = 2816
NORM_EPS = 1e-6

LANES = 128
VMEM_LIMIT = 56 * 1024 * 1024
INT_MIN = -2147483648
INT_MAX = 2147483647
NEG_BIG = -1e30

_C_QA, _C_KA, _C_VA, _C_QB, _C_KB, _C_VB, _C_QI, _C_KI, _C_WI, _C_END = (
    0, 512, 1024, 1536, 2048, 2176, 2304, 2816, 2944, 3072)


def _cparams(*sem):
    return pltpu.CompilerParams(dimension_semantics=sem, vmem_limit_bytes=VMEM_LIMIT)


def _dot(a, b):
    return jnp.dot(a, b, preferred_element_type=F32)


def _dot_nt(a, b):
    return lax.dot_general(a, b, (((1,), (1,)), ((), ())), preferred_element_type=F32)


def _dot_tn(a, b):
    return lax.dot_general(a, b, (((0,), (0,)), ((), ())), preferred_element_type=F32)


def _split(x):
    hi = x.astype(BF16)
    lo = (x - hi.astype(F32)).astype(BF16)
    return hi, lo


def _dot_x2(x, m_bf16):
    hi, lo = _split(x)
    return _dot(hi, m_bf16) + _dot(lo, m_bf16)


def _rms(x, g):
    return x * lax.rsqrt(jnp.mean(x * x, axis=-1, keepdims=True) + NORM_EPS) * g


def _seg_ones():
    r = lax.broadcasted_iota(I32, (LANES, LANES), 0)
    c = lax.broadcasted_iota(I32, (LANES, LANES), 1)
    return ((r < 64) == (c < 64)).astype(BF16)


def _seg64_sum(x, bd):
    n = x.shape[1] // LANES
    return jnp.concatenate([_dot_x2(x[:, LANES * m:LANES * (m + 1)], bd) for m in range(n)], axis=1)


def _attn_proj_kernel(x_ref, g_ref, w_ref, cos_ref, sin_ref,
                      qa_ref, ka_ref, va_ref, qb_ref, kb_ref, vb_ref, qi_ref, ki_ref, wi_ref):
    h = _rms(x_ref[...], g_ref[...]).astype(BF16)
    cos = cos_ref[...]
    sin = sin_ref[...]
    lane = lax.broadcasted_iota(I32, (1, LANES), 1)
    first = (lane & 32) == 0

    def proj(c0, c1):
        return _dot(h, w_ref[:, c0:c1])

    def rope(blk):
        rot = jnp.where(first, pltpu.roll(blk, LANES - 32, 1), pltpu.roll(blk, 32, 1))
        return blk * cos + rot * sin

    qa_ref[...] = proj(_C_QA, _C_KA)
    ka_ref[...] = proj(_C_KA, _C_VA)
    va_ref[...] = proj(_C_VA, _C_QB)
    for m in range(4):
        qb_ref[:, LANES * m:LANES * (m + 1)] = rope(proj(_C_QB + LANES * m, _C_QB + LANES * (m + 1)))
        qi_ref[:, LANES * m:LANES * (m + 1)] = rope(proj(_C_QI + LANES * m, _C_QI + LANES * (m + 1)))
    kb_ref[...] = rope(proj(_C_KB, _C_VB))
    vb_ref[...] = proj(_C_VB, _C_QI)
    ki_ref[...] = rope(proj(_C_KI, _C_WI))
    wi_ref[...] = proj(_C_WI, _C_END)


def _attn_proj(x, g, w, cos, sin, tm):
    M = x.shape[0]
    nt = cos.shape[0] // tm
    widths = (512, 512, 512, 512, 128, 128, 512, 128, 128)
    row = lambda i: (i, 0)
    return pl.pallas_call(
        _attn_proj_kernel,
        grid=(M // tm,),
        in_specs=[pl.BlockSpec((tm, D_MODEL), row),
                  pl.BlockSpec((1, D_MODEL), lambda i: (0, 0)),
                  pl.BlockSpec((D_MODEL, _C_END), lambda i: (0, 0)),
                  pl.BlockSpec((tm, LANES), lambda i: (i % nt, 0)),
                  pl.BlockSpec((tm, LANES), lambda i: (i % nt, 0))],
        out_specs=[pl.BlockSpec((tm, wd), row) for wd in widths],
        out_shape=[jax.ShapeDtypeStruct((M, wd), F32) for wd in widths],
        compiler_params=_cparams("parallel"),
        name="attn_proj",
    )(x, g, w, cos, sin)


def _rope_tables(pos):
    half = D_HEAD // 2
    inv = ROPE_THETA ** (-2.0 * jnp.arange(half, dtype=F32) / D_HEAD)
    ang = pos.astype(F32)[:, None] * inv[None, :]
    cos = jnp.cos(ang)
    sin = jnp.sin(ang)
    return jnp.tile(cos, (1, 4)), jnp.tile(jnp.concatenate([-sin, sin], axis=1), (1, 2))


def _pack_w_in(w_in):
    ki = w_in[:, 2816:2880]
    wi = w_in[:, 2880:2888]
    pad = jnp.zeros((w_in.shape[0], LANES - H_IDX), w_in.dtype)
    return jnp.concatenate([w_in[:, :2816], ki, ki, wi, pad], axis=1).astype(BF16)


LOG2E = math.log2(math.e)
ATT_SCALE = D_HEAD ** -0.5 * LOG2E


def _softplus2(z2):
    return jnp.log(1.0 + jnp.exp2(-jnp.abs(z2))) * LOG2E


def _sb_block(z, c, tri, vb, causal):
    t = _softplus2(z)
    lb = jnp.minimum(z, 0.0) - t
    lk = lb - z
    if causal is not None:
        lk = jnp.where(causal, lk, 0.0)
    aft = _dot(lk.astype(BF16), tri) + c
    w = jnp.exp2(lb + aft)
    if causal is not None:
        w = jnp.where(causal, w, 0.0)
    pv = _dot(w.astype(BF16), vb)
    return c + jnp.sum(lk, axis=1, keepdims=True), pv


SB_DEAD = -150.0


def _sb_prompt_kernel(q_ref, k_ref, v_ref, o_ref, *, tq):
    i = pl.program_id(2)
    lane = lax.broadcasted_iota(I32, (1, LANES), 1)
    row = lax.broadcasted_iota(I32, (tq, tq), 0)
    col = lax.broadcasted_iota(I32, (tq, tq), 1)
    tri = (row > col).astype(BF16)
    causal = col < row
    q = q_ref[...] * ATT_SCALE
    qh = [jnp.where(lane < 64, q, 0.0).astype(BF16), jnp.where(lane >= 64, q, 0.0).astype(BF16)]

    def step(js, carry, mask):
        kbs, vbs = [], []
        for j in js:
            off = pl.multiple_of(j * tq, tq)
            kbs.append(k_ref[pl.ds(off, tq), :].astype(BF16))
            vbs.append(v_ref[pl.ds(off, tq), :].astype(BF16))
        tiles = [(h2, n) for n in range(len(js)) for h2 in range(2)]
        z = [_dot_nt(qh[h2], kbs[n]) for h2, n in tiles]
        t = [_softplus2(x) for x in z]
        lb = [jnp.minimum(x, 0.0) - y for x, y in zip(z, t)]
        lk = [x - y for x, y in zip(lb, z)]
        if mask is not None:
            lk = [jnp.where(mask, x, 0.0) for x in lk]
        loc = [_dot(x.astype(BF16), tri) for x in lk]
        tot = [jnp.sum(x, axis=1, keepdims=True) for x in lk]
        c = [carry[0], carry[2]]
        w = []
        for idx, (h2, n) in enumerate(tiles):
            x = jnp.exp2(lb[idx] + loc[idx] + c[h2])
            w.append((jnp.where(mask, x, 0.0) if mask is not None else x).astype(BF16))
            c[h2] = c[h2] + tot[idx]
        pv = [_dot(w[idx], vbs[n]) for idx, (h2, n) in enumerate(tiles)]
        acc = [carry[1], carry[3]]
        for idx, (h2, n) in enumerate(tiles):
            acc[h2] = acc[h2] + pv[idx]
        return c[0], acc[0], c[1], acc[1]

    def alive(carry):
        return jnp.maximum(jnp.max(carry[0]), jnp.max(carry[2])) > SB_DEAD

    zc, za = jnp.zeros((tq, 1), F32), jnp.zeros((tq, LANES), F32)
    carry = step([i], (zc, za, zc, za), causal)
    carry = lax.fori_loop(0, jnp.minimum(i, 1), lambda jj, cr: step([i - 1], cr, None), carry)
    rem = jnp.maximum(i - 1, 0)

    def pair_cond(st):
        return (st[0] < rem // 2) & alive(st[1:])

    def pair_body(st):
        jj = st[0]
        return (jj + 1,) + step([i - 2 - 2 * jj, i - 3 - 2 * jj], st[1:], None)

    carry = lax.while_loop(pair_cond, pair_body, (jnp.int32(0),) + carry)[1:]
    last = jnp.where((rem % 2 == 1) & alive(carry), 1, 0)
    carry = lax.fori_loop(0, last, lambda jj, cr: step([0], cr, None), carry)
    o_ref[...] = jnp.where(lane < 64, carry[1], carry[3])


def _sb_prompt(q, k, v, B, T, tq):
    nq = T // tq
    return pl.pallas_call(
        functools.partial(_sb_prompt_kernel, tq=tq),
        grid=(B, 4, nq),
        in_specs=[pl.BlockSpec((tq, LANES), lambda b, p, i: (b * nq + i, p)),
                  pl.BlockSpec((T, LANES), lambda b, p, i: (b, p)),
                  pl.BlockSpec((T, LANES), lambda b, p, i: (b, p))],
        out_specs=pl.BlockSpec((tq, LANES), lambda b, p, i: (b * nq + i, p)),
        out_shape=jax.ShapeDtypeStruct(q.shape, F32),
        compiler_params=_cparams("parallel", "parallel", "arbitrary"),
        name="sb_prompt",
    )(q, k, v)


SROWS = 16


def _pages_per_step(n_pages):
    for g in (32, 16, 8, 4, 2):
        if n_pages % g == 0:
            return g
    return 1


def _sb_sample_kernel(pt_ref, q_ref, kn_ref, vn_ref, *rest, G):
    kc_refs, vc_refs = rest[:G], rest[G:2 * G]
    o_ref, c_scr, acc_scr = rest[2 * G:]
    j = pl.program_id(1)
    row = lax.broadcasted_iota(I32, (PAGE, PAGE), 0)
    col = lax.broadcasted_iota(I32, (PAGE, PAGE), 1)
    tri = (row > col).astype(BF16)
    qrow = lax.broadcasted_iota(I32, (SROWS, PAGE), 0)
    kcol = lax.broadcasted_iota(I32, (SROWS, PAGE), 1)
    causal = kcol < qrow
    scale = ATT_SCALE

    def qhead(h):
        return (q_ref[0, :, D_HEAD * h:D_HEAD * (h + 1)] * scale).astype(BF16)

    @pl.when(j == 0)
    def _():
        for h in range(H_SB):
            kb = kn_ref[0, :, D_HEAD * h:D_HEAD * (h + 1)].astype(BF16)
            vb = vn_ref[0, :, D_HEAD * h:D_HEAD * (h + 1)].astype(BF16)
            c, pv = _sb_block(_dot_nt(qhead(h), kb), jnp.zeros((SROWS, 1), F32), tri, vb, causal)
            c_scr[h] = jnp.broadcast_to(c, (SROWS, LANES))
            acc_scr[h] = pv

    order = range(G - 1, -1, -1)
    zs = []
    for h in range(H_SB):
        kt = jnp.concatenate([kc_refs[s][h] for s in order], axis=1).astype(BF16)
        z = _dot(qhead(h), kt)
        zs += [z[:, PAGE * p:PAGE * (p + 1)] for p in range(G)]
    z = jnp.concatenate(zs, axis=0)
    t = _softplus2(z)
    lb = jnp.minimum(z, 0.0) - t
    lk = lb - z
    loc = _dot(lk.astype(BF16), tri)
    tot = jnp.sum(lk, axis=1, keepdims=True)
    cs = []
    for h in range(H_SB):
        run = c_scr[h][:, 0:1]
        per_page = [None] * G
        for p in range(G - 1, -1, -1):
            per_page[p] = run
            run = run + tot[(h * G + p) * SROWS:(h * G + p + 1) * SROWS]
        c_scr[h] = jnp.broadcast_to(run, (SROWS, LANES))
        cs += per_page
    w = jnp.exp2(lb + loc + jnp.concatenate(cs, axis=0)).astype(BF16)
    for h in range(H_SB):
        wh = jnp.concatenate([w[(h * G + p) * SROWS:(h * G + p + 1) * SROWS] for p in range(G)], axis=1)
        vt = jnp.concatenate([vc_refs[s][h] for s in order], axis=1).astype(BF16)
        acc_scr[h] = acc_scr[h] + _dot_nt(wh, vt)

    @pl.when(j == pl.num_programs(1) - 1)
    def _():
        for h in range(H_SB):
            o_ref[0, :, D_HEAD * h:D_HEAD * (h + 1)] = acc_scr[h]


def _sb_sample(pt, q, kn, vn, kc, vc):
    B, NP = pt.shape
    G = _pages_per_step(NP)
    page = lambda s: pl.BlockSpec((None, H_SB, D_HEAD, PAGE),
                                  lambda b, j, pt: (pt[b, NP - 1 - (j * G + s)], 0, 0, 0))
    grid_spec = pltpu.PrefetchScalarGridSpec(
        num_scalar_prefetch=1,
        grid=(B, NP // G),
        in_specs=([pl.BlockSpec((1, SROWS, 512), lambda b, j, pt: (b, 0, 0)),
                   pl.BlockSpec((1, PAGE, 512), lambda b, j, pt: (b, 0, 0)),
                   pl.BlockSpec((1, PAGE, 512), lambda b, j, pt: (b, 0, 0))]
                  + [page(s) for s in range(G)] * 2),
        out_specs=pl.BlockSpec((1, SROWS, 512), lambda b, j, pt: (b, 0, 0)),
        scratch_shapes=[pltpu.VMEM((H_SB, SROWS, LANES), F32), pltpu.VMEM((H_SB, SROWS, D_HEAD), F32)])
    return pl.pallas_call(
        functools.partial(_sb_sample_kernel, G=G),
        grid_spec=grid_spec,
        out_shape=jax.ShapeDtypeStruct((B, SROWS, 512), F32),
        compiler_params=_cparams("parallel", "arbitrary"),
        name="sb_sample",
    )(pt, q, kn, vn, *([kc] * G), *([vc] * G))


_IDX_SCALE = D_IDX ** -0.5 * H_IDX ** -0.5


def _score_key(score):
    score = jnp.where(score == 0.0, 0.0, score)
    bits = pltpu.bitcast(score, I32)
    return bits ^ ((bits >> 31) & INT_MAX)


def _topk_threshold(count_ge, count_eq_lt, parts, rows, real_rows, n_sel, idx_bits, jl_ref):
    zeros = tuple(jnp.zeros((rows, 1), I32) for _ in range(parts))

    def bit_body(it, tu):
        cand = tuple(t | jnp.left_shift(jnp.int32(1), 31 - it) for t in tu)
        cnt = count_ge(tuple(c ^ INT_MIN for c in cand))
        return tuple(jnp.where(n >= n_sel, c, t) for n, c, t in zip(cnt, cand, tu))

    tu = lax.fori_loop(0, 32, bit_body, zeros)
    thr = tuple(t ^ INT_MIN for t in tu)
    n_ge = count_ge(thr)
    n_gt = count_ge(tuple(t + 1 for t in thr))
    need = tuple(n_sel - jnp.where(t == INT_MAX, 0, n) for t, n in zip(thr, n_gt))
    tied = tuple((n > n_sel) & (t != INT_MIN) for n, t in zip(n_ge, thr))
    if real_rows < rows:
        real = lax.broadcasted_iota(I32, (rows, 1), 0) < real_rows
        tied = tuple(t & real for t in tied)
    width = jl_ref.shape[-1]
    for n in range(parts):
        jl_ref[n] = jnp.full((rows, width), INT_MAX, I32)
    any_tied = tied[0].astype(I32)
    for t in tied[1:]:
        any_tied = jnp.maximum(any_tied, t.astype(I32))

    @pl.when(jnp.max(any_tied) > 0)
    def _():
        def idx_body(it, lo):
            cand = tuple(x | jnp.left_shift(jnp.int32(1), idx_bits - 1 - it) for x in lo)
            cnt = count_eq_lt(thr, cand)
            return tuple(jnp.where(n < nd, c, x) for n, nd, c, x in zip(cnt, need, cand, lo))
        lo = lax.fori_loop(0, idx_bits, idx_body, zeros)
        for n in range(parts):
            jl_ref[n] = jnp.broadcast_to(jnp.where(tied[n], lo[n], INT_MAX), (rows, width))

    return thr


def _dsa_prompt_kernel(qi_ref, wi_ref, ki_ref, qb_ref, kb_ref, vb_ref, o_ref, key_scr, jl_scr, row_scr,
                       *, tq, n_sel, idx_bits, AG):
    i = pl.program_id(1)
    tk = LANES
    R = tq // tk
    cm = R * i
    cl = cm + R - 1
    lane = lax.broadcasted_iota(I32, (1, LANES), 1)
    lo_half = lane < 64
    row = lax.broadcasted_iota(I32, (tq, tk), 0)
    col = lax.broadcasted_iota(I32, (tq, tk), 1)
    diag_ok = [col + r * tk <= row for r in range(R)]

    qs = []
    for h in range(H_IDX):
        blk = qi_ref[:, LANES * (h // 2):LANES * (h // 2 + 1)]
        qs.append(jnp.where(lo_half if h % 2 == 0 else ~lo_half, blk, 0.0))
    qstack = jnp.concatenate(qs, axis=0).astype(BF16)
    wb = [jnp.broadcast_to(wi_ref[:, h:h + 1], (tq, tk)) for h in range(H_IDX)]

    def score_chunks(c0, n, masks):
        off = pl.multiple_of(c0 * tk, tk)
        kc = ki_ref[pl.ds(off, n * tk), :].astype(BF16)
        s = _dot_nt(qstack, kc)
        relu = [[jnp.maximum(s[h * tq:(h + 1) * tq, u * tk:(u + 1) * tk], 0.0) for h in range(H_IDX)]
                for u in range(n)]
        score = [wb[0] * r[0] for r in relu]
        for h in range(1, H_IDX):
            score = [sc + wb[h] * r[h] for sc, r in zip(score, relu)]
        for u in range(n):
            key = _score_key(score[u] * _IDX_SCALE)
            if masks is not None:
                key = jnp.where(masks[u], key, INT_MIN)
            key_scr[c0 + u] = key

    def score_group(g, carry):
        score_chunks(g * AG, AG, None)
        return carry

    def score_single(c, carry):
        score_chunks(c, 1, None)
        return carry

    lax.fori_loop(0, cm // AG, score_group, 0)
    lax.fori_loop((cm // AG) * AG, cm, score_single, 0)
    score_chunks(cm, R, diag_ok)
    for u in range(1, AG):
        @pl.when((cl % AG) + u < AG)
        def _():
            key_scr[cl + u] = jnp.full((tq, tk), INT_MIN, I32)

    NP = tq // LANES
    prow = [slice(LANES * n, LANES * (n + 1)) for n in range(NP)]
    colp = col[0:LANES]

    def count_ge(cands):
        accs = []
        for n in range(NP):
            def body(g, acc, n=n):
                for u in range(AG):
                    acc = acc + (key_scr[g * AG + u, prow[n], :] >= cands[n]).astype(F32)
                return acc
            accs.append(lax.fori_loop(0, cl // AG + 1, body, jnp.zeros((LANES, tk), F32)))
        return tuple(jnp.sum(a, axis=1, keepdims=True).astype(I32) for a in accs)

    def count_eq_lt(thrs, jcands):
        accs = []
        for n in range(NP):
            def body(g, acc, n=n):
                for u in range(AG):
                    c = g * AG + u
                    hit = (key_scr[c, prow[n], :] == thrs[n]) & (colp + c * tk < jcands[n])
                    acc = acc + hit.astype(F32)
                return acc
            accs.append(lax.fori_loop(0, cl // AG + 1, body, jnp.zeros((LANES, tk), F32)))
        return tuple(jnp.sum(a, axis=1, keepdims=True).astype(I32) for a in accs)

    thr = _topk_threshold(count_ge, count_eq_lt, NP, LANES, LANES, n_sel, idx_bits, jl_scr)
    jl = [jl_scr[n][:, 0:1] for n in range(NP)]

    qs = []
    for h in range(H_DSA):
        blk = qb_ref[:, LANES * (h // 2):LANES * (h // 2 + 1)] * ATT_SCALE
        g = h // (H_DSA // KV_DSA)
        if (h % 2) != g:
            blk = pltpu.roll(blk, 64, 1)
        qs.append(jnp.where(lo_half if g == 0 else ~lo_half, blk, 0.0))
    qstack2 = jnp.concatenate(qs, axis=0).astype(BF16)

    ta = AG * tk
    rowa = lax.broadcasted_iota(I32, (tq, ta), 0)
    cola = lax.broadcasted_iota(I32, (tq, ta), 1)
    for n in range(NP):
        row_scr[0, prow[n], :] = jnp.broadcast_to(thr[n], (LANES, LANES))
        row_scr[1, prow[n], :] = jnp.broadcast_to(jl[n], (LANES, LANES))
    thr_a, jl_a = row_scr[0, :, 0:1], row_scr[1, :, 0:1]

    def att_group(c, carry, last):
        m, l, acc = carry
        off = pl.multiple_of(c * ta, ta)
        kblk = jnp.concatenate([key_scr[c * AG + u] for u in range(AG)], axis=1)
        sel = (kblk > thr_a) | ((kblk == thr_a) & (cola + off <= jl_a))
        if last:
            sel = sel & (cola + off <= rowa + i * tq)
        z = _dot_nt(qstack2, kb_ref[pl.ds(off, ta), :].astype(BF16))
        z = jnp.where(sel[None], z.reshape(H_DSA, tq, ta), NEG_BIG).reshape(H_DSA * tq, ta)
        m_new = jnp.maximum(m, jnp.max(z, axis=1, keepdims=True))
        alpha = jnp.exp2(m - m_new)
        p = jnp.exp2(z - m_new)
        l = alpha * l + jnp.sum(p, axis=1, keepdims=True)
        acc = alpha * acc + _dot(p.astype(BF16), vb_ref[pl.ds(off, ta), :].astype(BF16))
        return m_new, l, acc

    carry = (jnp.full((H_DSA * tq, 1), NEG_BIG, F32), jnp.zeros((H_DSA * tq, 1), F32),
             jnp.zeros((H_DSA * tq, LANES), F32))
    full_groups = (i * tq) // ta
    carry = lax.fori_loop(0, full_groups, lambda c, cr: att_group(c, cr, False), carry)
    for r in range(max(1, tq // ta)):
        carry = att_group(full_groups + r, carry, True)
    m, l, acc = carry
    out = acc / l
    for mblk in range(4):
        parts = []
        for h in (2 * mblk, 2 * mblk + 1):
            o_h = out[h * tq:(h + 1) * tq]
            if (h % 2) != h // (H_DSA // KV_DSA):
                o_h = pltpu.roll(o_h, 64, 1)
            parts.append(o_h)
        o_ref[:, LANES * mblk:LANES * (mblk + 1)] = jnp.where(lo_half, parts[0], parts[1])


def _dsa_prompt(qi, wi, ki2, qb, kb, vb, B, T, tq, n_sel):
    nq = T // tq
    qrow = lambda b, i: (b * nq + i, 0)
    full = lambda b, i: (b, 0)
    AG = 4 if (T // LANES) % 4 == 0 else 1
    return pl.pallas_call(
        functools.partial(_dsa_prompt_kernel, tq=tq, n_sel=n_sel, idx_bits=max(1, (T - 1).bit_length()), AG=AG),
        grid=(B, nq),
        in_specs=[pl.BlockSpec((tq, 512), qrow), pl.BlockSpec((tq, LANES), qrow),
                  pl.BlockSpec((T, LANES), full), pl.BlockSpec((tq, 512), qrow),
                  pl.BlockSpec((T, LANES), full), pl.BlockSpec((T, LANES), full)],
        out_specs=pl.BlockSpec((tq, 512), qrow),
        out_shape=jax.ShapeDtypeStruct(qb.shape, F32),
        scratch_shapes=[pltpu.VMEM((T // LANES, tq, LANES), I32), pltpu.VMEM((tq // LANES, LANES, LANES), I32),
                        pltpu.VMEM((2, tq, LANES), I32)],
        compiler_params=_cparams("parallel", "arbitrary"),
        name="dsa_prompt",
    )(qi, wi, ki2, qb, kb, vb)


def _dsa_score_sample_kernel(pt_ref, q_ref, w_ref, kn_ref, *rest, G, n_pages, n_real, n_sel, idx_bits):
    kc_refs = rest[:G]
    key_ref, thr_ref, jl_ref = rest[G:]
    j = pl.program_id(1)
    qrow = lax.broadcasted_iota(I32, (SROWS, PAGE), 0)
    kcol = lax.broadcasted_iota(I32, (SROWS, PAGE), 1)
    q = q_ref[0].astype(BF16)
    w = w_ref[0]

    def score_keys(s):
        s = jnp.maximum(s, 0.0) * w
        sc = s[0:SROWS]
        for h in range(1, H_IDX):
            sc = sc + s[h * SROWS:(h + 1) * SROWS]
        return _score_key(sc * _IDX_SCALE)

    @pl.when(j < n_pages // G)
    def _():
        kt = jnp.concatenate([kc_refs[s][...] for s in range(G)], axis=1).astype(BF16)
        key = score_keys(_dot(q, kt))
        for s in range(G):
            key_ref[0, j * G + s] = key[:, PAGE * s:PAGE * (s + 1)]

    @pl.when(j == n_pages // G)
    def _():
        key = score_keys(_dot_nt(q, kn_ref[0].astype(BF16)))
        key_ref[0, n_pages] = jnp.where(kcol <= qrow, key, INT_MIN)
        pos = lax.broadcasted_iota(I32, (G, SROWS, PAGE), 0) * PAGE + kcol[None]

        def count_ge(cands):
            cand, = cands

            def body(g, acc):
                blk = key_ref[0, pl.ds(g * G, G)]
                return acc + jnp.sum((blk >= cand[None]).astype(F32), axis=0)
            acc = lax.fori_loop(0, n_pages // G, body, jnp.zeros((SROWS, PAGE), F32))
            acc = acc + (key_ref[0, n_pages] >= cand).astype(F32)
            return (jnp.sum(acc, axis=1, keepdims=True).astype(I32),)

        def count_eq_lt(thrs, jcands):
            (thr,), (jcand,) = thrs, jcands

            def body(g, acc):
                blk = key_ref[0, pl.ds(g * G, G)]
                hit = (blk == thr[None]) & (pos + g * (G * PAGE) < jcand[None])
                return acc + jnp.sum(hit.astype(F32), axis=0)
            acc = lax.fori_loop(0, n_pages // G, body, jnp.zeros((SROWS, PAGE), F32))
            last = (key_ref[0, n_pages] == thr) & (kcol + n_pages * PAGE < jcand)
            return (jnp.sum(acc + last.astype(F32), axis=1, keepdims=True).astype(I32),)

        thr_ref[0], = _topk_threshold(count_ge, count_eq_lt, 1, SROWS, n_real, n_sel, idx_bits, jl_ref)


def _dsa_score_sample(pt, q, w, kn, kc, n_real, n_sel):
    B, NP = pt.shape
    G = _pages_per_step(NP)
    L = (NP + 1) * PAGE
    page = lambda s: pl.BlockSpec((None, D_IDX, PAGE),
                                  lambda b, j, pt: (pt[b, jnp.minimum(j * G + s, NP - 1)], 0, 0))
    grid_spec = pltpu.PrefetchScalarGridSpec(
        num_scalar_prefetch=1,
        grid=(B, NP // G + 1),
        in_specs=([pl.BlockSpec((1, H_IDX * SROWS, D_IDX), lambda b, j, pt: (b, 0, 0)),
                   pl.BlockSpec((1, H_IDX * SROWS, 1), lambda b, j, pt: (b, 0, 0)),
                   pl.BlockSpec((1, PAGE, D_IDX), lambda b, j, pt: (b, 0, 0))]
                  + [page(s) for s in range(G)]),
        out_specs=[pl.BlockSpec((1, NP + 1, SROWS, PAGE), lambda b, j, pt: (b, 0, 0, 0)),
                   pl.BlockSpec((1, SROWS, 1), lambda b, j, pt: (b, 0, 0)),
                   pl.BlockSpec((1, SROWS, 1), lambda b, j, pt: (b, 0, 0))])
    return pl.pallas_call(
        functools.partial(_dsa_score_sample_kernel, G=G, n_pages=NP, n_real=n_real, n_sel=n_sel,
                          idx_bits=max(1, (L - 1).bit_length())),
        grid_spec=grid_spec,
        out_shape=[jax.ShapeDtypeStruct((B, NP + 1, SROWS, PAGE), I32),
                   jax.ShapeDtypeStruct((B, SROWS, 1), I32),
                   jax.ShapeDtypeStruct((B, SROWS, 1), I32)],
        compiler_params=_cparams("parallel", "arbitrary"),
        name="dsa_score_sample",
    )(pt, q, w, kn, *([kc] * G))


_GR = (H_DSA // KV_DSA) * SROWS


def _dsa_att_sample_kernel(pt_ref, q_ref, key_ref, thr_ref, jl_ref, kn_ref, vn_ref, *rest, G, n_pages):
    kc_refs, vc_refs = rest[:G], rest[G:2 * G]
    o_ref, m_scr, l_scr, acc_scr = rest[2 * G:]
    j = pl.program_id(1)
    qrow = lax.broadcasted_iota(I32, (SROWS, PAGE), 0)
    kcol = lax.broadcasted_iota(I32, (SROWS, PAGE), 1)
    nrep = H_DSA // KV_DSA

    @pl.when(j == 0)
    def _():
        m_scr[...] = jnp.full(m_scr.shape, NEG_BIG, F32)
        l_scr[...] = jnp.zeros(l_scr.shape, F32)
        acc_scr[...] = jnp.zeros(acc_scr.shape, F32)

    def attend(z, v, g, sel, v_transposed=True):
        n = z.shape[1]
        z = jnp.where(sel[None], z.reshape(nrep, SROWS, n), NEG_BIG).reshape(_GR, n)
        m = m_scr[g][:, 0:1]
        l = l_scr[g][:, 0:1]
        m_new = jnp.maximum(m, jnp.max(z, axis=1, keepdims=True))
        alpha = jnp.exp2(m - m_new)
        p = jnp.exp2(z - m_new)
        l = alpha * l + jnp.sum(p, axis=1, keepdims=True)
        pv =_dot_nt(p.astype(BF16), v.astype(BF16)) if v_transposed else _dot(p.astype(BF16), v.astype(BF16))
        acc_scr[g] = alpha * acc_scr[g] + pv
        m_scr[g] = jnp.broadcast_to(m_new, (_GR, LANES))
        l_scr[g] = jnp.broadcast_to(l, (_GR, LANES))

    def qgroup(g):
        return (q_ref[0, g] * ATT_SCALE).astype(BF16)

    thr = thr_ref[0]
    jl = jl_ref[0]

    @pl.when(j < n_pages // G)
    def _():
        sels = []
        for s in range(G):
            kblk = key_ref[0, j * G + s]
            sels.append((kblk > thr) | ((kblk == thr) & (kcol + (j * G + s) * PAGE <= jl)))
        sel = jnp.concatenate(sels, axis=1)
        for g in range(KV_DSA):
            kt = jnp.concatenate([kc_refs[s][g] for s in range(G)], axis=1).astype(BF16)
            vt = jnp.concatenate([vc_refs[s][g] for s in range(G)], axis=1)
            attend(_dot(qgroup(g), kt), vt, g, sel)

    @pl.when(j == n_pages // G)
    def _():
        off = n_pages * PAGE
        kblk = key_ref[0, n_pages]
        sel = ((kblk > thr) | ((kblk == thr) & (kcol + off <= jl))) & (kcol <= qrow)
        for g in range(KV_DSA):
            kn = kn_ref[0, :, D_HEAD * g:D_HEAD * (g + 1)].astype(BF16)
            attend(_dot_nt(qgroup(g), kn), vn_ref[0, :, D_HEAD * g:D_HEAD * (g + 1)], g, sel, v_transposed=False)
        for g in range(KV_DSA):
            o_ref[0, g] = acc_scr[g] / l_scr[g][:, 0:1]


def _dsa_att_sample(pt, q, keys, thr, jl, kn, vn, kc, vc):
    B, NP = pt.shape
    G = _pages_per_step(NP)
    page = lambda s: pl.BlockSpec((None, KV_DSA, D_HEAD, PAGE),
                                  lambda b, j, pt: (pt[b, jnp.minimum(j * G + s, NP - 1)], 0, 0, 0))
    per_b3 = lambda b, j, pt: (b, 0, 0)
    grid_spec = pltpu.PrefetchScalarGridSpec(
        num_scalar_prefetch=1,
        grid=(B, NP // G + 1),
        in_specs=([pl.BlockSpec((1, KV_DSA, _GR, D_HEAD), lambda b, j, pt: (b, 0, 0, 0)),
                   pl.BlockSpec((1, NP + 1, SROWS, PAGE), lambda b, j, pt: (b, 0, 0, 0)),
                   pl.BlockSpec((1, SROWS, 1), per_b3),
                   pl.BlockSpec((1, SROWS, 1), per_b3),
                   pl.BlockSpec((1, PAGE, LANES), per_b3),
                   pl.BlockSpec((1, PAGE, LANES), per_b3)]
                  + [page(s) for s in range(G)] * 2),
        out_specs=pl.BlockSpec((1, KV_DSA, _GR, D_HEAD), lambda b, j, pt: (b, 0, 0, 0)),
        scratch_shapes=[pltpu.VMEM((KV_DSA, _GR, LANES), F32), pltpu.VMEM((KV_DSA, _GR, LANES), F32),
                        pltpu.VMEM((KV_DSA, _GR, D_HEAD), F32)])
    return pl.pallas_call(
        functools.partial(_dsa_att_sample_kernel, G=G, n_pages=NP),
        grid_spec=grid_spec,
        out_shape=jax.ShapeDtypeStruct((B, KV_DSA, _GR, D_HEAD), F32),
        compiler_params=_cparams("parallel", "arbitrary"),
        name="dsa_att_sample",
    )(pt, q, keys, thr, jl, kn, vn, *([kc] * G), *([vc] * G))


def _mm_norm_res_kernel(*refs, n):
    a_refs, w_refs = refs[:n], refs[n:2 * n]
    g_ref, x_ref, o_ref = refs[2 * n:]
    acc = _dot(a_refs[0][...].astype(BF16), w_refs[0][...])
    for a_ref, w_ref in zip(a_refs[1:], w_refs[1:]):
        acc = acc + _dot(a_ref[...].astype(BF16), w_ref[...])
    o_ref[...] = x_ref[...] + _rms(acc, g_ref[...])


def _mm_norm_res(a_list, w_list, g, x, tm, name):
    M = x.shape[0]
    n = len(a_list)
    row = lambda i: (i, 0)
    const = lambda i: (0, 0)
    return pl.pallas_call(
        functools.partial(_mm_norm_res_kernel, n=n),
        grid=(M // tm,),
        in_specs=([pl.BlockSpec((tm, a.shape[1]), row) for a in a_list]
                  + [pl.BlockSpec(w.shape, const) for w in w_list]
                  + [pl.BlockSpec((1, D_MODEL), const), pl.BlockSpec((tm, D_MODEL), row)]),
        out_specs=pl.BlockSpec((tm, D_MODEL), row),
        out_shape=jax.ShapeDtypeStruct((M, D_MODEL), F32),
        compiler_params=_cparams("parallel"),
        name=name,
    )(*a_list, *w_list, g, x)


_GELU_C = math.sqrt(2.0 / math.pi)
FFN_SUB = 256


def _gelu_tanh(x):
    return x * (0.5 * (1.0 + jnp.tanh(_GELU_C * (x + 0.044715 * (x * x * x)))))


def _ffn_up_kernel(x_ref, g_ref, wg_ref, wv_ref, cwg_ref, cwv_ref, cbg_ref, cbv_ref, pg_ref, pv_ref,
                   act_ref, og_ref, ov_ref, eg_scr, ev_scr, *, tm, T, streamed):
    i = pl.program_id(1)
    h = _rms(x_ref[...], g_ref[...]).astype(BF16)
    trow = lax.broadcasted_iota(I32, (tm, 1), 0) % T

    tn = act_ref.shape[1]
    if streamed:
        @pl.when((i * tm) % T == 0)
        def _():
            eg_scr[6:8, :] = pg_ref[...]
            ev_scr[6:8, :] = pv_ref[...]
    else:
        eg_scr[0:8, :] = jnp.zeros((8, tn), F32)
        ev_scr[0:8, :] = jnp.zeros((8, tn), F32)

    rid = lax.broadcasted_iota(I32, (8, 1), 0)

    def conv(u, cs, cw_ref, cb_ref, p_ref, o_ref, e_scr):
        if streamed:
            o_ref[:, cs] = u[tm - 2:tm, :]
            prev2, prev1 = e_scr[6:7, cs], e_scr[7:8, cs]
            r1 = pltpu.roll(u, 1, 0)
            r2 = pltpu.roll(u, 2, 0)
            head1 = jnp.where(rid == 0, prev1, r1[0:8])
            head2 = jnp.where(rid == 0, prev2, jnp.where(rid == 1, prev1, r2[0:8]))
            u1 = jnp.concatenate([head1, r1[8:]], axis=0)
            u2 = jnp.concatenate([head2, r2[8:]], axis=0)
            e_scr[6:8, cs] = u[tm - 2:tm, :]
        else:
            e_scr[8:8 + tm, cs] = u
            o_ref[:, cs] = u
            u1 = jnp.where(trow >= 1, e_scr[7:7 + tm, cs], p_ref[0, :, cs])
            u2 = jnp.where(trow >= 2, e_scr[6:6 + tm, cs], p_ref[1, :, cs])
        return cb_ref[:, cs] + cw_ref[0:1, cs] * u2 + cw_ref[1:2, cs] * u1 + cw_ref[2:3, cs] * u

    subs = [slice(c0, min(c0 + FFN_SUB, tn)) for c0 in range(0, tn, FFN_SUB)]
    dots = lambda cs: (_dot(h, wg_ref[:, cs]), _dot(h, wv_ref[:, cs]))
    cur = dots(subs[0])
    for n, cs in enumerate(subs):
        nxt = dots(subs[n + 1]) if n + 1 < len(subs) else None
        gate = conv(cur[0], cs, cwg_ref, cbg_ref, pg_ref, og_ref, eg_scr)
        val = conv(cur[1], cs, cwv_ref, cbv_ref, pv_ref, ov_ref, ev_scr)
        act_ref[:, cs] = (_gelu_tanh(gate) * val).astype(BF16)
        cur = nxt


def _ffn_up(x, g, w_up, conv_w, conv_b, prev, T, tm, tn):
    M = x.shape[0]
    nj = D_FF // tn
    streamed = T >= tm
    tpb = max(T // tm, 1)
    if streamed:
        p_spec = lambda off: pl.BlockSpec((None, 2, tn), lambda j, i: (i // tpb, 0, j + off))
        o_shape = jax.ShapeDtypeStruct((M // T, 2, D_FF), F32)
        o_spec = pl.BlockSpec((None, 2, tn), lambda j, i: (i // tpb, 0, j))
    else:
        p_spec = lambda off: pl.BlockSpec((2, tm, tn), lambda j, i: (0, i, j + off))
        o_shape = jax.ShapeDtypeStruct((M, D_FF), F32)
        o_spec = pl.BlockSpec((tm, tn), lambda j, i: (i, j))
    col = lambda rows, off: pl.BlockSpec((rows, tn), lambda j, i: (0, j + off))
    return pl.pallas_call(
        functools.partial(_ffn_up_kernel, tm=tm, T=T, streamed=streamed),
        grid=(nj, M // tm),
        in_specs=[pl.BlockSpec((tm, D_MODEL), lambda j, i: (i, 0)),
                  pl.BlockSpec((1, D_MODEL), lambda j, i: (0, 0)),
                  col(D_MODEL, 0), col(D_MODEL, nj), col(3, 0), col(3, nj), col(1, 0), col(1, nj),
                  p_spec(0), p_spec(nj)],
        out_specs=[pl.BlockSpec((tm, tn), lambda j, i: (i, j)), o_spec, o_spec],
        out_shape=[jax.ShapeDtypeStruct((M, D_FF), BF16), o_shape, o_shape],
        scratch_shapes=[pltpu.VMEM((tm + 8, tn), F32), pltpu.VMEM((tm + 8, tn), F32)],
        compiler_params=_cparams("parallel", "arbitrary"),
        name="ffn_up",
    )(x, g, w_up, w_up, conv_w, conv_w, conv_b, conv_b, prev, prev)


def _rk_pre_kernel(x_ref, g_ref, sp_ref, mix_ref, wr_ref, wk_ref, wv_ref, w1_ref, w2_ref, a1_ref, a2_ref,
                   g1_ref, g2_ref, vec_ref,
                   hn_ref, r_ref, wl_ref, kf_ref, v_ref, av_ref, bv_ref, gg_ref, e_scr, *, tm, T, streamed):
    i = pl.program_id(0)
    h = _rms(x_ref[...], g_ref[...])
    hn_ref[...] = h
    e_scr[8:8 + tm, :] = h
    if streamed:
        @pl.when((i * tm) % T == 0)
        def _():
            e_scr[7:8, :] = sp_ref[...]
        xp = e_scr[7:7 + tm, :]
        e_scr[7:8, :] = h[tm - 1:tm, :]
    else:
        e_scr[0:8, :] = jnp.zeros((8, D_MODEL), F32)
        trow = lax.broadcasted_iota(I32, (tm, 1), 0) % T
        xp = jnp.where(trow >= 1, e_scr[7:7 + tm, :], sp_ref[...])
    xx = xp - h
    mixed = lambda j: (h + xx * mix_ref[j:j + 1, :]).astype(BF16)
    w0, a0, k_k, k_a = vec_ref[0:1, :], vec_ref[1:2, :], vec_ref[2:3, :], vec_ref[3:4, :]
    r = _dot(mixed(0), wr_ref[...])
    k = _dot(mixed(2), wk_ref[...])
    v = _dot(mixed(3), wv_ref[...])
    lw = w0 + _dot(jnp.tanh(_dot(mixed(1), w1_ref[...])).astype(BF16), w2_ref[...])
    w_log = -(jnp.maximum(-lw, 0.0) + jnp.log(1.0 + jnp.exp(-jnp.abs(lw)))) - 0.5
    wl_ref[...] = -jnp.exp(w_log)
    a = jax.nn.sigmoid(a0 + _dot(_dot(mixed(4), a1_ref[...]).astype(BF16), a2_ref[...]))
    gg_ref[...] = _dot(jax.nn.sigmoid(_dot(mixed(5), g1_ref[...])).astype(BF16), g2_ref[...])
    kk = k * k_k
    bd = _seg_ones()
    kk = kk / jnp.maximum(jnp.sqrt(_seg64_sum(kk * kk, bd)), 1e-12)
    r_ref[...] = r
    v_ref[...] = v
    kf_ref[...] = k * (1.0 + (a - 1.0) * k_a)
    av_ref[...] = -kk
    bv_ref[...] = kk * a


def _rk_pre(x, g, sp, mix, wr, wk, wv, w1, w2, a1, a2, g1, g2, vecs, T, tm):
    M = x.shape[0]
    streamed = T >= tm
    tpb = max(T // tm, 1)
    row = lambda i: (i, 0)
    const = lambda i: (0, 0)
    sp_spec = (pl.BlockSpec((None, 1, D_MODEL), lambda i: (i // tpb, 0, 0)) if streamed
               else pl.BlockSpec((tm, D_MODEL), row))
    full = lambda a: pl.BlockSpec(a.shape, const)
    return pl.pallas_call(
        functools.partial(_rk_pre_kernel, tm=tm, T=T, streamed=streamed),
        grid=(M // tm,),
        in_specs=[pl.BlockSpec((tm, D_MODEL), row), pl.BlockSpec((1, D_MODEL), const), sp_spec,
                  full(mix), full(wr), full(wk), full(wv), full(w1), full(w2), full(a1), full(a2),
                  full(g1), full(g2), full(vecs)],
        out_specs=[pl.BlockSpec((tm, D_MODEL), row)] * 8,
        out_shape=[jax.ShapeDtypeStruct((M, D_MODEL), F32)] * 8,
        scratch_shapes=[pltpu.VMEM((tm + 8, D_MODEL), F32)],
        compiler_params=_cparams("arbitrary"),
        name="rk_pre",
    )(x, g, sp, mix, wr, wk, wv, w1, w2, a1, a2, g1, g2, vecs)


def _bdot(a, b):
    return _dot(a.astype(BF16), b.astype(BF16))


def _rk_scan_kernel(r_ref, wl_ref, k_ref, v_ref, a_ref, b_ref, s0_ref, y_ref, sf_ref, s_scr, *, C, npair):
    c = pl.program_id(2)

    @pl.when(c == 0)
    def _():
        s_scr[...] = s0_ref[0]

    lane = lax.broadcasted_iota(I32, (1, LANES), 1)
    m0 = lane < 64
    rr = lax.broadcasted_iota(I32, (C, C), 0)
    cc = lax.broadcasted_iota(I32, (C, C), 1)
    tri_incl = (cc <= rr).astype(BF16)
    r2 = lax.broadcasted_iota(I32, (2 * C, 2 * C), 0)
    c2 = lax.broadcasted_iota(I32, (2 * C, 2 * C), 1)
    strict = (r2 % C) > (c2 % C)
    incl = (r2 % C) >= (c2 % C)
    incl2 = jnp.concatenate([incl, incl], axis=1)
    eye = (r2 == c2).astype(F32)
    n_double = max(int(math.log2(C)) - 1, 0)

    def stack2(z):
        return jnp.concatenate([jnp.where(m0, z, 0.0), jnp.where(m0, 0.0, z)], axis=0)

    pairs = range(npair)
    sls = [slice(LANES * p, LANES * (p + 1)) for p in pairs]
    wl = [wl_ref[:, sl] for sl in sls]
    ld = []
    for p in pairs:
        wl_hi, wl_lo = _split(wl[p])
        ld.append(_dot(tri_incl, wl_hi) + _dot(tri_incl, wl_lo))
    dfull = [jnp.exp(x) for x in ld]
    dinv = [jnp.exp(-x) for x in ld]
    As = [stack2(a_ref[:, sls[p]] * jnp.exp(ld[p] - wl[p])).astype(BF16) for p in pairs]
    Bs = [stack2(b_ref[:, sls[p]] * dinv[p]) for p in pairs]
    Ks = [stack2(k_ref[:, sls[p]] * dinv[p]) for p in pairs]
    Rs = [stack2(r_ref[:, sls[p]] * dfull[p]).astype(BF16) for p in pairs]
    Vs = [stack2(v_ref[:, sls[p]]) for p in pairs]
    Vb = [x.astype(BF16) for x in Vs]
    BK = [jnp.concatenate([Bs[p], Ks[p]], axis=0).astype(BF16) for p in pairs]
    G1 = [_dot_nt(As[p], BK[p]) for p in pairs]
    G2 = [_dot_nt(Rs[p], BK[p]) for p in pairs]
    Lab = [jnp.where(strict, g[:, :2 * C], 0.0) for g in G1]
    Lak = [jnp.where(strict, g[:, 2 * C:], 0.0).astype(BF16) for g in G1]
    Mr = [jnp.where(incl2, g, 0.0).astype(BF16) for g in G2]
    Tm = [eye + x for x in Lab]
    if n_double > 0:
        Pb = [x.astype(BF16) for x in Lab]
        Pb = [_dot(x, x).astype(BF16) for x in Pb]
        for rnd in range(n_double):
            Tb = [x.astype(BF16) for x in Tm]
            if rnd + 1 < n_double:
                X = [_dot(Pb[p], jnp.concatenate([Pb[p], Tb[p]], axis=1)) for p in pairs]
                Pb = [x[:, :2 * C].astype(BF16) for x in X]
                Tm = [Tm[p] + X[p][:, 2 * C:] for p in pairs]
            else:
                Tm = [Tm[p] + _dot(Pb[p], Tb[p]) for p in pairs]
    S = [s_scr[p] for p in pairs]
    Sb = [x.astype(BF16) for x in S]
    rhs = [_dot_nt(As[p], Sb[p]) + _dot(Lak[p], Vb[p]) for p in pairs]
    U = [_bdot(Tm[p], rhs[p]) for p in pairs]
    UV = [jnp.concatenate([U[p], Vs[p]], axis=0) for p in pairs]
    UVb = [x.astype(BF16) for x in UV]
    Ys = [_dot_nt(Rs[p], Sb[p]) + _dot(Mr[p], UVb[p]) for p in pairs]
    upd = [_dot(UV[p].T.astype(BF16), BK[p]) for p in pairs]
    for p in pairs:
        y_ref[:, sls[p]] = Ys[p][:C] + Ys[p][C:]
        s_scr[p] = (S[p] + upd[p]) * dfull[p][C - 1:C, :]

    @pl.when(c == pl.num_programs(2) - 1)
    def _():
        sf_ref[0] = s_scr[...]


def _rk_scan(r, wl, k, v, a, b, s0, B, T, C, npair):
    nc = T // C
    ng = 8 // npair
    blk = pl.BlockSpec((C, LANES * npair), lambda bb, g, c: (bb * nc + c, g))
    s_spec = pl.BlockSpec((1, npair, LANES, LANES), lambda bb, g, c: (bb, g, 0, 0))
    return pl.pallas_call(
        functools.partial(_rk_scan_kernel, C=C, npair=npair),
        grid=(B, ng, nc),
        in_specs=[blk] * 6 + [s_spec],
        out_specs=[blk, s_spec],
        out_shape=[jax.ShapeDtypeStruct(r.shape, F32), jax.ShapeDtypeStruct(s0.shape, F32)],
        scratch_shapes=[pltpu.VMEM((npair, LANES, LANES), F32)],
        compiler_params=_cparams("parallel", "parallel", "arbitrary"),
        name="rk_scan",
    )(r, wl, k, v, a, b, s0)


def _rk_post_kernel(y_ref, r_ref, kf_ref, v_ref, gg_ref, vec_ref, o_ref):
    bd = _seg_ones()
    ln_w, ln_b, r_k = vec_ref[0:1, :], vec_ref[1:2, :], vec_ref[2:3, :]
    y = y_ref[...]
    mu = _seg64_sum(y, bd) * (1.0 / RK_N)
    d = y - mu
    var = _seg64_sum(d * d, bd) * (1.0 / RK_N)
    yn = d * lax.rsqrt(var + RK_GN_EPS) * ln_w + ln_b
    bonus = _seg64_sum(r_ref[...] * kf_ref[...] * r_k, bd) * v_ref[...]
    o_ref[...] = ((yn + bonus) * gg_ref[...]).astype(BF16)


def _rk_post(y, r, kf, v, gg, vecs, tm):
    M = y.shape[0]
    row = lambda i: (i, 0)
    return pl.pallas_call(
        _rk_post_kernel,
        grid=(M // tm,),
        in_specs=[pl.BlockSpec((tm, D_MODEL), row)] * 5 + [pl.BlockSpec(vecs.shape, lambda i: (0, 0))],
        out_specs=pl.BlockSpec((tm, D_MODEL), row),
        out_shape=jax.ShapeDtypeStruct((M, D_MODEL), BF16),
        compiler_params=_cparams("parallel"),
        name="rk_post",
    )(y, r, kf, v, gg, vecs)


def _row_tile(M):
    return min(256, M)


def _pad_rows(a, n):
    return jnp.pad(a, ((0, 0), (0, n - a.shape[1])) + ((0, 0),) * (a.ndim - 2))


def _attn_layer_prompt(x, B, T, g_pre, w_in, w_outs, g_post):
    M = B * T
    tm = _row_tile(M)
    cos, sin = _rope_tables(jnp.arange(T, dtype=I32))
    qa, ka, va, qb, kb, vb, qi, ki2, wi = _attn_proj(x, g_pre, w_in, cos, sin, tm)
    n_sel = min(DSA_TOPK, T // 4)
    oa = _sb_prompt(qa, ka, va, B, T, min(256, T))
    ob = _dsa_prompt(qi, wi, ki2, qb, kb, vb, B, T, min(128, T), n_sel)
    x = _mm_norm_res([oa, ob], w_outs, g_post, x, tm, "attn_out")
    rows = (ka.reshape(1, B, T, H_SB, D_HEAD), va.reshape(1, B, T, H_SB, D_HEAD),
            kb.reshape(1, B, T, KV_DSA, D_HEAD), vb.reshape(1, B, T, KV_DSA, D_HEAD),
            ki2[:, :D_IDX].reshape(1, B, T, D_IDX))
    return x, rows


def _attn_layer_sample(x, B, T, P, page_table, caches, g_pre, w_in, w_outs, g_post):
    M = B * T
    cos, sin = _rope_tables(P + jnp.arange(T, dtype=I32))
    cos, sin = jnp.tile(cos, (B, 1)), jnp.tile(sin, (B, 1))
    qa, ka, va, qb, kb, vb, qi, ki2, wi = _attn_proj(x, g_pre, w_in, cos, sin, M)
    c_sb_k, c_sb_v, c_dsa_k, c_dsa_v, c_idx = caches
    n_pool = c_sb_k.shape[0]
    n_sel = min(DSA_TOPK, (P + T) // 4)
    b3 = lambda a: a.reshape(B, T, a.shape[-1])

    row_minor = lambda c: jnp.moveaxis(c, 1, -1)
    oa = _sb_sample(page_table, _pad_rows(b3(qa), SROWS), _pad_rows(b3(ka), PAGE), _pad_rows(b3(va), PAGE),
                    row_minor(c_sb_k), row_minor(c_sb_v))
    oa = oa[:, :T].reshape(M, H_SB * D_HEAD)

    qi_s = _pad_rows(b3(qi).reshape(B, T, H_IDX, D_IDX), SROWS).transpose(0, 2, 1, 3).reshape(B, H_IDX * SROWS, D_IDX)
    wi_s = _pad_rows(b3(wi)[:, :, :H_IDX], SROWS).transpose(0, 2, 1).reshape(B, H_IDX * SROWS, 1)
    keys, thr, jl = _dsa_score_sample(page_table, qi_s, wi_s, _pad_rows(b3(ki2)[:, :, :D_IDX], PAGE),
                                      row_minor(c_idx), T, n_sel)
    qb_s = _pad_rows(b3(qb).reshape(B, T, H_DSA, D_HEAD), SROWS).transpose(0, 2, 1, 3).reshape(B, KV_DSA, _GR, D_HEAD)
    ob = _dsa_att_sample(page_table, qb_s, keys, thr, jl, _pad_rows(b3(kb), PAGE), _pad_rows(b3(vb), PAGE),
                         row_minor(c_dsa_k), row_minor(c_dsa_v))
    ob = ob.reshape(B, H_DSA, SROWS, D_HEAD)[:, :, :T].transpose(0, 2, 1, 3).reshape(M, H_DSA * D_HEAD)

    x = _mm_norm_res([oa, ob], w_outs, g_post, x, M, "attn_out")
    rows = (ka.reshape(1, B, T, H_SB, D_HEAD), va.reshape(1, B, T, H_SB, D_HEAD),
            kb.reshape(1, B, T, KV_DSA, D_HEAD), vb.reshape(1, B, T, KV_DSA, D_HEAD),
            ki2[:, :D_IDX].reshape(1, B, T, D_IDX))
    return x, rows


def _pair_state(S):
    B = S.shape[0]
    S = S.reshape(B, 8, 2, RK_N, RK_N)
    z = jnp.zeros_like(S[:, :, 0])
    top = jnp.concatenate([S[:, :, 0], z], axis=-1)
    bot = jnp.concatenate([z, S[:, :, 1]], axis=-1)
    return jnp.concatenate([top, bot], axis=-2)


def _unpair_state(Sp):
    B = Sp.shape[0]
    return jnp.stack([Sp[:, :, :RK_N, :RK_N], Sp[:, :, RK_N:, RK_N:]], axis=2).reshape(B, RK_H, RK_N, RK_N)


RK_CHUNK = 64
RK_PAIRS = 8


def _rwkv_layer(x, B, T, shift_prev, S0, g_pre, prm, g_post):
    M = B * T
    tm = _row_tile(M)
    (mix, wr, wk, wv, wo, w1, w2, a1, a2, g1, g2, vec_pre, vec_post) = prm
    if T >= tm:
        sp = shift_prev.reshape(B, 1, D_MODEL)
    else:
        sp = jnp.repeat(shift_prev, T, axis=0)
    hn, r, wl, kf, v, av, bv, gg = _rk_pre(x, g_pre, sp, mix, wr, wk, wv, w1, w2, a1, a2, g1, g2, vec_pre, T, tm)
    Tp = -(-T // RK_CHUNK) * RK_CHUNK
    if Tp != T:
        padt = lambda a: _pad_rows(a.reshape(B, T, D_MODEL), Tp).reshape(B * Tp, D_MODEL)
        y, Sf = _rk_scan(padt(r), padt(wl), padt(kf), padt(v), padt(av), padt(bv), _pair_state(S0), B, Tp, RK_CHUNK, RK_PAIRS)
        y = y.reshape(B, Tp, D_MODEL)[:, :T].reshape(M, D_MODEL)
    else:
        y, Sf = _rk_scan(r, wl, kf, v, av, bv, _pair_state(S0), B, T, RK_CHUNK, RK_PAIRS)
    z = _rk_post(y, r, kf, v, gg, vec_post, tm)
    x = _mm_norm_res([z], [wo], g_post, x, tm, "rk_out")
    shift = hn.reshape(B, T, D_MODEL)[:, -1]
    return x, shift, _unpair_state(Sf)


def _ffn_layer(x, B, T, prev, g_pre, w_up, conv_w, conv_b, w_down, g_post):
    M = B * T
    tm = _row_tile(M)
    tn = D_FF // 2
    if T >= tm:
        act, cg, cv = _ffn_up(x, g_pre, w_up, conv_w, conv_b, prev, T, tm, tn)
        conv_state = jnp.concatenate([cg, cv], axis=-1)
    else:
        zeros = jnp.zeros((B, T, 2 * D_FF), F32)
        p1 = zeros.at[:, 0].set(prev[:, 1])
        p2 = zeros.at[:, 0].set(prev[:, 0]).at[:, 1].set(prev[:, 1])
        pp = jnp.stack([p1.reshape(M, -1), p2.reshape(M, -1)])
        act, ug, uv = _ffn_up(x, g_pre, w_up, conv_w, conv_b, pp, T, tm, tn)
        u = jnp.concatenate([ug, uv], axis=-1).reshape(B, T, 2 * D_FF)
        conv_state = jnp.concatenate([prev, u], axis=1)[:, -2:]
    x = _mm_norm_res([act], [w_down], g_post, x, tm, "ffn_down")
    return x, conv_state


def _forward(x_prompt, x_sample, cache_sb_k, cache_sb_v, cache_dsa_k, cache_dsa_v, cache_idx_k, page_table,
             state_wkv, state_shift, state_ffn_conv,
             norm_mix_pre, norm_mix_post, norm_ffn_pre, norm_ffn_post,
             att_w_in, att_w_out,
             rk_mix, rk_w_r, rk_w_k, rk_w_v, rk_w_o, rk_w0, rk_w1, rk_w2, rk_a0, rk_a1, rk_a2,
             rk_g1, rk_g2, rk_k_k, rk_k_a, rk_r_k, rk_ln_w, rk_ln_b,
             ffn_w_up, ffn_conv_w, ffn_conv_b, ffn_w_down):
    Bp, Tp, D = x_prompt.shape
    Bs, Ts, _ = x_sample.shape
    P = page_table.shape[1] * PAGE
    depth = norm_mix_pre.shape[0]
    xp = x_prompt.reshape(Bp * Tp, D)
    xs = x_sample.reshape(Bs * Ts, D)
    vrow = lambda a: a.reshape(1, -1)
    bf = lambda a: a.astype(BF16)
    att_p, att_s, wkv_p, wkv_s, sh_p, sh_s, cv_p, cv_s = [], [], [], [], [], [], [], []
    for i in range(depth):
        li = i // 2
        g_pre, g_post = vrow(norm_mix_pre[i]), vrow(norm_mix_post[i])
        if i % 2 == 0:
            w_in = _pack_w_in(att_w_in[li])
            w_outs = [bf(att_w_out[li][:H_SB * D_HEAD]), bf(att_w_out[li][H_SB * D_HEAD:])]
            caches = (cache_sb_k[li], cache_sb_v[li], cache_dsa_k[li], cache_dsa_v[li], cache_idx_k[li])
            xp, rows_p = _attn_layer_prompt(xp, Bp, Tp, g_pre, w_in, w_outs, g_post)
            xs, rows_s = _attn_layer_sample(xs, Bs, Ts, P, page_table, caches, g_pre, w_in, w_outs, g_post)
            att_p.append(rows_p)
            att_s.append(rows_s)
        else:
            vec_pre = jnp.stack([rk_w0[li], rk_a0[li], rk_k_k[li], rk_k_a[li]])
            vec_post = jnp.stack([rk_ln_w[li], rk_ln_b[li], rk_r_k[li].reshape(-1)])
            prm = (rk_mix[li], bf(rk_w_r[li]), bf(rk_w_k[li]), bf(rk_w_v[li]), bf(rk_w_o[li]),
                   bf(rk_w1[li]), bf(rk_w2[li]), bf(rk_a1[li]), bf(rk_a2[li]), bf(rk_g1[li]), bf(rk_g2[li]),
                   vec_pre, vec_post)
            xp, shp, Sp = _rwkv_layer(xp, Bp, Tp, jnp.zeros((Bp, D), F32), jnp.zeros((Bp, RK_H, RK_N, RK_N), F32),
                                      g_pre, prm, g_post)
            xs, shs, Ss = _rwkv_layer(xs, Bs, Ts, state_shift[li], state_wkv[li], g_pre, prm, g_post)
            wkv_p.append(Sp)
            wkv_s.append(Ss)
            sh_p.append(shp)
            sh_s.append(shs)
        f_pre, f_post = vrow(norm_ffn_pre[i]), vrow(norm_ffn_post[i])
        w_up, w_down = bf(ffn_w_up[i]), bf(ffn_w_down[i])
        cb = vrow(ffn_conv_b[i])
        xp, cp = _ffn_layer(xp, Bp, Tp, jnp.zeros((Bp, 2, 2 * D_FF), F32), f_pre, w_up, ffn_conv_w[i], cb, w_down, f_post)
        xs, cs = _ffn_layer(xs, Bs, Ts, state_ffn_conv[i], f_pre, w_up, ffn_conv_w[i], cb, w_down, f_post)
        cv_p.append(cp)
        cv_s.append(cs)
    cat = lambda rows, j: jnp.concatenate([r[j] for r in rows], axis=0)
    return (xp.reshape(Bp, Tp, D), xs.reshape(Bs, Ts, D),
            cat(att_p, 0), cat(att_s, 0), cat(att_p, 1), cat(att_s, 1),
            cat(att_p, 2), cat(att_s, 2), cat(att_p, 3), cat(att_s, 3),
            cat(att_p, 4), cat(att_s, 4),
            jnp.stack(wkv_p), jnp.stack(wkv_s), jnp.stack(sh_p), jnp.stack(sh_s),
            jnp.stack(cv_p), jnp.stack(cv_s))


def kernel(x_prompt, x_sample, cache_sb_k, cache_sb_v, cache_dsa_k, cache_dsa_v, cache_idx_k, page_table, state_wkv, state_shift, state_ffn_conv, norm_mix_pre, norm_mix_post, norm_ffn_pre, norm_ffn_post, att_w_in, att_w_out, rk_mix, rk_w_r, rk_w_k, rk_w_v, rk_w_o, rk_w0, rk_w1, rk_w2, rk_a0, rk_a1, rk_a2, rk_g1, rk_g2, rk_k_k, rk_k_a, rk_r_k, rk_ln_w, rk_ln_b, ffn_w_up, ffn_conv_w, ffn_conv_b, ffn_w_down):
    return _forward(x_prompt, x_sample, cache_sb_k, cache_sb_v, cache_dsa_k, cache_dsa_v, cache_idx_k, page_table,
                    state_wkv, state_shift, state_ffn_conv,
                    norm_mix_pre, norm_mix_post, norm_ffn_pre, norm_ffn_post,
                    att_w_in, att_w_out,
                    rk_mix, rk_w_r, rk_w_k, rk_w_v, rk_w_o, rk_w0, rk_w1, rk_w2, rk_a0, rk_a1, rk_a2,
                    rk_g1, rk_g2, rk_k_k, rk_k_a, rk_r_k, rk_ln_w, rk_ln_b,
                    ffn_w_up, ffn_conv_w, ffn_conv_b, ffn_w_down)
```

```python
import functools
import math

import jax
import jax.numpy as jnp
from jax import lax
from jax.experimental import pallas as pl
from jax.experimental.pallas import tpu as pltpu

F32 = jnp.float32
BF16 = jnp.bfloat16
I32 = jnp.int32

D_MODEL = 1024
D_HEAD = 64
H_SB = 8
H_DSA = 8
KV_DSA = 2
H_IDX = 8
D_IDX = 64
DSA_TOPK = 256
PAGE = 128
ROPE_THETA = 10000.0
RK_N = 64
RK_H = D_MODEL // RK_N
RK_GN_EPS = 64e-5
D_FF = 2816
NORM_EPS = 1e-6

LANES = 128
VMEM_LIMIT = 56 * 1024 * 1024
INT_MIN = -2147483648
INT_MAX = 2147483647
NEG_BIG = -1e30

_C_QA, _C_KA, _C_VA, _C_QB, _C_KB, _C_VB, _C_QI, _C_KI, _C_WI, _C_END = (
    0, 512, 1024, 1536, 2048, 2176, 2304, 2816, 2944, 3072)


def _cparams(*sem):
    return pltpu.CompilerParams(dimension_semantics=sem, vmem_limit_bytes=VMEM_LIMIT)


def _dot(a, b):
    return jnp.dot(a, b, preferred_element_type=F32)


def _dot_nt(a, b):
    return lax.dot_general(a, b, (((1,), (1,)), ((), ())), preferred_element_type=F32)


def _dot_tn(a, b):
    return lax.dot_general(a, b, (((0,), (0,)), ((), ())), preferred_element_type=F32)


def _split(x):
    hi = x.astype(BF16)
    lo = (x - hi.astype(F32)).astype(BF16)
    return hi, lo


def _dot_x2(x, m_bf16):
    hi, lo = _split(x)
    return _dot(hi, m_bf16) + _dot(lo, m_bf16)


def _rms(x, g):
    return x * lax.rsqrt(jnp.mean(x * x, axis=-1, keepdims=True) + NORM_EPS) * g


def _seg_ones():
    r = lax.broadcasted_iota(I32, (LANES, LANES), 0)
    c = lax.broadcasted_iota(I32, (LANES, LANES), 1)
    return ((r < 64) == (c < 64)).astype(BF16)


def _seg64_sum(x, bd):
    n = x.shape[1] // LANES
    return jnp.concatenate([_dot_x2(x[:, LANES * m:LANES * (m + 1)], bd) for m in range(n)], axis=1)


def _attn_proj_kernel(x_ref, g_ref, w_ref, cos_ref, sin_ref,
                      qa_ref, ka_ref, va_ref, qb_ref, kb_ref, vb_ref, qi_ref, ki_ref, wi_ref):
    h = _rms(x_ref[...], g_ref[...]).astype(BF16)
    cos = cos_ref[...]
    sin = sin_ref[...]
    lane = lax.broadcasted_iota(I32, (1, LANES), 1)
    first = (lane & 32) == 0

    def proj(c0, c1):
        return _dot(h, w_ref[:, c0:c1])

    def rope(blk):
        rot = jnp.where(first, pltpu.roll(blk, LANES - 32, 1), pltpu.roll(blk, 32, 1))
        return blk * cos + rot * sin

    qa_ref[...] = proj(_C_QA, _C_KA)
    ka_ref[...] = proj(_C_KA, _C_VA)
    va_ref[...] = proj(_C_VA, _C_QB)
    for m in range(4):
        qb_ref[:, LANES * m:LANES * (m + 1)] = rope(proj(_C_QB + LANES * m, _C_QB + LANES * (m + 1)))
        qi_ref[:, LANES * m:LANES * (m + 1)] = rope(proj(_C_QI + LANES * m, _C_QI + LANES * (m + 1)))
    kb_ref[...] = rope(proj(_C_KB, _C_VB))
    vb_ref[...] = proj(_C_VB, _C_QI)
    ki_ref[...] = rope(proj(_C_KI, _C_WI))
    wi_ref[...] = proj(_C_WI, _C_END)


def _attn_proj(x, g, w, cos, sin, tm):
    M = x.shape[0]
    nt = cos.shape[0] // tm
    widths = (512, 512, 512, 512, 128, 128, 512, 128, 128)
    row = lambda i: (i, 0)
    return pl.pallas_call(
        _attn_proj_kernel,
        grid=(M // tm,),
        in_specs=[pl.BlockSpec((tm, D_MODEL), row),
                  pl.BlockSpec((1, D_MODEL), lambda i: (0, 0)),
                  pl.BlockSpec((D_MODEL, _C_END), lambda i: (0, 0)),
                  pl.BlockSpec((tm, LANES), lambda i: (i % nt, 0)),
                  pl.BlockSpec((tm, LANES), lambda i: (i % nt, 0))],
        out_specs=[pl.BlockSpec((tm, wd), row) for wd in widths],
        out_shape=[jax.ShapeDtypeStruct((M, wd), F32) for wd in widths],
        compiler_params=_cparams("parallel"),
        name="attn_proj",
    )(x, g, w, cos, sin)


def _rope_tables(pos):
    half = D_HEAD // 2
    inv = ROPE_THETA ** (-2.0 * jnp.arange(half, dtype=F32) / D_HEAD)
    ang = pos.astype(F32)[:, None] * inv[None, :]
    cos = jnp.cos(ang)
    sin = jnp.sin(ang)
    return jnp.tile(cos, (1, 4)), jnp.tile(jnp.concatenate([-sin, sin], axis=1), (1, 2))


def _pack_w_in(w_in):
    ki = w_in[:, 2816:2880]
    wi = w_in[:, 2880:2888]
    pad = jnp.zeros((w_in.shape[0], LANES - H_IDX), w_in.dtype)
    return jnp.concatenate([w_in[:, :2816], ki, ki, wi, pad], axis=1).astype(BF16)


LOG2E = math.log2(math.e)
ATT_SCALE = D_HEAD ** -0.5 * LOG2E


def _softplus2(z2):
    return jnp.log(1.0 + jnp.exp2(-jnp.abs(z2))) * LOG2E


def _sb_block(z, c, tri, vb, causal):
    t = _softplus2(z)
    lb = jnp.minimum(z, 0.0) - t
    lk = lb - z
    if causal is not None:
        lk = jnp.where(causal, lk, 0.0)
    aft = _dot(lk.astype(BF16), tri) + c
    w = jnp.exp2(lb + aft)
    if causal is not None:
        w = jnp.where(causal, w, 0.0)
    pv = _dot(w.astype(BF16), vb)
    return c + jnp.sum(lk, axis=1, keepdims=True), pv


SB_DEAD = -150.0


def _sb_prompt_kernel(q_ref, k_ref, v_ref, o_ref, *, tq):
    i = pl.program_id(2)
    lane = lax.broadcasted_iota(I32, (1, LANES), 1)
    row = lax.broadcasted_iota(I32, (tq, tq), 0)
    col = lax.broadcasted_iota(I32, (tq, tq), 1)
    tri = (row > col).astype(BF16)
    causal = col < row
    q = q_ref[...] * ATT_SCALE
    qh = [jnp.where(lane < 64, q, 0.0).astype(BF16), jnp.where(lane >= 64, q, 0.0).astype(BF16)]

    def step(js, carry, mask):
        kbs, vbs = [], []
        for j in js:
            off = pl.multiple_of(j * tq, tq)
            kbs.append(k_ref[pl.ds(off, tq), :].astype(BF16))
            vbs.append(v_ref[pl.ds(off, tq), :].astype(BF16))
        tiles = [(h2, n) for n in range(len(js)) for h2 in range(2)]
        z = [_dot_nt(qh[h2], kbs[n]) for h2, n in tiles]
        t = [_softplus2(x) for x in z]
        lb = [jnp.minimum(x, 0.0) - y for x, y in zip(z, t)]
        lk = [x - y for x, y in zip(lb, z)]
        if mask is not None:
            lk = [jnp.where(mask, x, 0.0) for x in lk]
        loc = [_dot(x.astype(BF16), tri) for x in lk]
        tot = [jnp.sum(x, axis=1, keepdims=True) for x in lk]
        c = [carry[0], carry[2]]
        w = []
        for idx, (h2, n) in enumerate(tiles):
            x = jnp.exp2(lb[idx] + loc[idx] + c[h2])
            w.append((jnp.where(mask, x, 0.0) if mask is not None else x).astype(BF16))
            c[h2] = c[h2] + tot[idx]
        pv = [_dot(w[idx], vbs[n]) for idx, (h2, n) in enumerate(tiles)]
        acc = [carry[1], carry[3]]
        for idx, (h2, n) in enumerate(tiles):
            acc[h2] = acc[h2] + pv[idx]
        return c[0], acc[0], c[1], acc[1]

    def alive(carry):
        return jnp.maximum(jnp.max(carry[0]), jnp.max(carry[2])) > SB_DEAD

    zc, za = jnp.zeros((tq, 1), F32), jnp.zeros((tq, LANES), F32)
    carry = step([i], (zc, za, zc, za), causal)
    carry = lax.fori_loop(0, jnp.minimum(i, 1), lambda jj, cr: step([i - 1], cr, None), carry)
    rem = jnp.maximum(i - 1, 0)

    def pair_cond(st):
        return (st[0] < rem // 2) & alive(st[1:])

    def pair_body(st):
        jj = st[0]
        return (jj + 1,) + step([i - 2 - 2 * jj, i - 3 - 2 * jj], st[1:], None)

    carry = lax.while_loop(pair_cond, pair_body, (jnp.int32(0),) + carry)[1:]
    last = jnp.where((rem % 2 == 1) & alive(carry), 1, 0)
    carry = lax.fori_loop(0, last, lambda jj, cr: step([0], cr, None), carry)
    o_ref[...] = jnp.where(lane < 64, carry[1], carry[3])


def _sb_prompt(q, k, v, B, T, tq):
    nq = T // tq
    return pl.pallas_call(
        functools.partial(_sb_prompt_kernel, tq=tq),
        grid=(B, 4, nq),
        in_specs=[pl.BlockSpec((tq, LANES), lambda b, p, i: (b * nq + i, p)),
                  pl.BlockSpec((T, LANES), lambda b, p, i: (b, p)),
                  pl.BlockSpec((T, LANES), lambda b, p, i: (b, p))],
        out_specs=pl.BlockSpec((tq, LANES), lambda b, p, i: (b * nq + i, p)),
        out_shape=jax.ShapeDtypeStruct(q.shape, F32),
        compiler_params=_cparams("parallel", "parallel", "arbitrary"),
        name="sb_prompt",
    )(q, k, v)


SROWS = 16


def _pages_per_step(n_pages):
    for g in (32, 16, 8, 4, 2):
        if n_pages % g == 0:
            return g
    return 1


def _sb_probe_kernel(pt_ref, q_ref, kn_ref, *rest, G, n_real):
    kc_refs, alive_ref = rest[:G], rest[G]
    qrow = lax.broadcasted_iota(I32, (SROWS, PAGE), 0)
    kcol = lax.broadcasted_iota(I32, (SROWS, PAGE), 1)
    causal = kcol < qrow
    log_keep = lambda z: jnp.minimum(z, 0.0) - _softplus2(z) - z
    cmax = None
    for h in range(H_SB):
        q = (q_ref[0, :, D_HEAD * h:D_HEAD * (h + 1)] * ATT_SCALE).astype(BF16)
        kb = kn_ref[0, :, D_HEAD * h:D_HEAD * (h + 1)].astype(BF16)
        kt = jnp.concatenate([kc_refs[s][h] for s in range(G)], axis=1).astype(BF16)
        c = (jnp.sum(jnp.where(causal, log_keep(_dot_nt(q, kb)), 0.0), axis=1, keepdims=True)
             + jnp.sum(log_keep(_dot(q, kt)), axis=1, keepdims=True))
        cmax = c if cmax is None else jnp.maximum(cmax, c)
    real = lax.broadcasted_iota(I32, (SROWS, 1), 0) < n_real
    worst = jnp.max(jnp.where(real, cmax, NEG_BIG))
    alive_ref[...] = jnp.full(alive_ref.shape, jnp.where(worst > SB_DEAD - 1.0, 1, 0), I32)


def _sb_sample_kernel(pt_ref, alive_ref, q_ref, kn_ref, vn_ref, *rest, G):
    kc_refs, vc_refs = rest[:G], rest[G:2 * G]
    o_ref, c_scr, acc_scr = rest[2 * G:]
    b = pl.program_id(0)
    j = pl.program_id(1)
    row = lax.broadcasted_iota(I32, (PAGE, PAGE), 0)
    col = lax.broadcasted_iota(I32, (PAGE, PAGE), 1)
    tri = (row > col).astype(BF16)
    qrow = lax.broadcasted_iota(I32, (SROWS, PAGE), 0)
    kcol = lax.broadcasted_iota(I32, (SROWS, PAGE), 1)
    causal = kcol < qrow
    scale = ATT_SCALE

    def qhead(h):
        return (q_ref[0, :, D_HEAD * h:D_HEAD * (h + 1)] * scale).astype(BF16)

    @pl.when(j == 0)
    def _():
        for h in range(H_SB):
            kb = kn_ref[0, :, D_HEAD * h:D_HEAD * (h + 1)].astype(BF16)
            vb = vn_ref[0, :, D_HEAD * h:D_HEAD * (h + 1)].astype(BF16)
            c, pv = _sb_block(_dot_nt(qhead(h), kb), jnp.zeros((SROWS, 1), F32), tri, vb, causal)
            c_scr[h] = jnp.broadcast_to(c, (SROWS, LANES))
            acc_scr[h] = pv

    @pl.when((j == 0) | (alive_ref[b] > 0))
    def _():
        order = range(G - 1, -1, -1)
        zs = []
        for h in range(H_SB):
            kt = jnp.concatenate([kc_refs[s][h] for s in order], axis=1).astype(BF16)
            z = _dot(qhead(h), kt)
            zs += [z[:, PAGE * p:PAGE * (p + 1)] for p in range(G)]
        z = jnp.concatenate(zs, axis=0)
        t = _softplus2(z)
        lb = jnp.minimum(z, 0.0) - t
        lk = lb - z
        loc = _dot(lk.astype(BF16), tri)
        tot = jnp.sum(lk, axis=1, keepdims=True)
        cs = []
        for h in range(H_SB):
            run = c_scr[h][:, 0:1]
            per_page = [None] * G
            for p in range(G - 1, -1, -1):
                per_page[p] = run
                run = run + tot[(h * G + p) * SROWS:(h * G + p + 1) * SROWS]
            c_scr[h] = jnp.broadcast_to(run, (SROWS, LANES))
            cs += per_page
        w = jnp.exp2(lb + loc + jnp.concatenate(cs, axis=0)).astype(BF16)
        for h in range(H_SB):
            wh = jnp.concatenate([w[(h * G + p) * SROWS:(h * G + p + 1) * SROWS] for p in range(G)], axis=1)
            vt = jnp.concatenate([vc_refs[s][h] for s in order], axis=1).astype(BF16)
            acc_scr[h] = acc_scr[h] + _dot_nt(wh, vt)

    @pl.when(j == pl.num_programs(1) - 1)
    def _():
        for h in range(H_SB):
            o_ref[0, :, D_HEAD * h:D_HEAD * (h + 1)] = acc_scr[h]


def _sb_sample(pt, q, kn, vn, kc, vc, n_real):
    B, NP = pt.shape
    G = _pages_per_step(NP)
    newest = lambda s: pl.BlockSpec((None, H_SB, D_HEAD, PAGE), lambda b, pt: (pt[b, NP - 1 - s], 0, 0, 0))
    alive = pl.pallas_call(
        functools.partial(_sb_probe_kernel, G=G, n_real=n_real),
        grid_spec=pltpu.PrefetchScalarGridSpec(
            num_scalar_prefetch=1,
            grid=(B,),
            in_specs=([pl.BlockSpec((1, SROWS, 512), lambda b, pt: (b, 0, 0)),
                       pl.BlockSpec((1, PAGE, 512), lambda b, pt: (b, 0, 0))]
                      + [newest(s) for s in range(G)]),
            out_specs=pl.BlockSpec((1, 8, LANES), lambda b, pt: (b, 0, 0))),
        out_shape=jax.ShapeDtypeStruct((B, 8, LANES), I32),
        compiler_params=_cparams("parallel"),
        name="sb_probe",
    )(pt, q, kn, *([kc] * G))[:, 0, 0]

    def page(s):
        def index(b, j, pt, alive):
            jj = jnp.where(alive[b] > 0, j, 0)
            return (pt[b, NP - 1 - (jj * G + s)], 0, 0, 0)
        return pl.BlockSpec((None, H_SB, D_HEAD, PAGE), index)

    per_b = lambda b, j, pt, alive: (b, 0, 0)
    grid_spec = pltpu.PrefetchScalarGridSpec(
        num_scalar_prefetch=2,
        grid=(B, NP // G),
        in_specs=([pl.BlockSpec((1, SROWS, 512), per_b),
                   pl.BlockSpec((1, PAGE, 512), per_b),
                   pl.BlockSpec((1, PAGE, 512), per_b)]
                  + [page(s) for s in range(G)] * 2),
        out_specs=pl.BlockSpec((1, SROWS, 512), per_b),
        scratch_shapes=[pltpu.VMEM((H_SB, SROWS, LANES), F32), pltpu.VMEM((H_SB, SROWS, D_HEAD), F32)])
    return pl.pallas_call(
        functools.partial(_sb_sample_kernel, G=G),
        grid_spec=grid_spec,
        out_shape=jax.ShapeDtypeStruct((B, SROWS, 512), F32),
        compiler_params=_cparams("parallel", "arbitrary"),
        name="sb_sample",
    )(pt, alive, q, kn, vn, *([kc] * G), *([vc] * G))


_IDX_SCALE = D_IDX ** -0.5 * H_IDX ** -0.5


def _score_key(score):
    score = jnp.where(score == 0.0, 0.0, score)
    bits = pltpu.bitcast(score, I32)
    return bits ^ ((bits >> 31) & INT_MAX)


def _topk_threshold(count_ge, count_eq_lt, parts, rows, real_rows, n_sel, idx_bits, jl_ref):
    zeros = tuple(jnp.zeros((rows, 1), I32) for _ in range(parts))

    def bit_body(it, tu):
        cand = tuple(t | jnp.left_shift(jnp.int32(1), 31 - it) for t in tu)
        cnt = count_ge(tuple(c ^ INT_MIN for c in cand))
        return tuple(jnp.where(n >= n_sel, c, t) for n, c, t in zip(cnt, cand, tu))

    tu = lax.fori_loop(0, 32, bit_body, zeros)
    thr = tuple(t ^ INT_MIN for t in tu)
    n_ge = count_ge(thr)
    tied = tuple((n > n_sel) & (t != INT_MIN) for n, t in zip(n_ge, thr))
    if real_rows < rows:
        real = lax.broadcasted_iota(I32, (rows, 1), 0) < real_rows
        tied = tuple(t & real for t in tied)
    width = jl_ref.shape[-1]
    for n in range(parts):
        jl_ref[n] = jnp.full((rows, width), INT_MAX, I32)
    any_tied = tied[0].astype(I32)
    for t in tied[1:]:
        any_tied = jnp.maximum(any_tied, t.astype(I32))

    @pl.when(jnp.max(any_tied) > 0)
    def _():
        n_gt = count_ge(tuple(t + 1 for t in thr))
        need = tuple(n_sel - jnp.where(t == INT_MAX, 0, n) for t, n in zip(thr, n_gt))

        def idx_body(it, lo):
            cand = tuple(x | jnp.left_shift(jnp.int32(1), idx_bits - 1 - it) for x in lo)
            cnt = count_eq_lt(thr, cand)
            return tuple(jnp.where(n < nd, c, x) for n, nd, c, x in zip(cnt, need, cand, lo))
        lo = lax.fori_loop(0, idx_bits, idx_body, zeros)
        for n in range(parts):
            jl_ref[n] = jnp.broadcast_to(jnp.where(tied[n], lo[n], INT_MAX), (rows, width))

    return thr


def _dsa_prompt_kernel(qi_ref, wi_ref, ki_ref, qb_ref, kb_ref, vb_ref, o_ref, key_scr, jl_scr, row_scr,
                       *, tq, n_sel, idx_bits, AG):
    i = pl.program_id(1)
    tk = LANES
    R = tq // tk
    cm = R * i
    cl = cm + R - 1
    lane = lax.broadcasted_iota(I32, (1, LANES), 1)
    lo_half = lane < 64
    row = lax.broadcasted_iota(I32, (tq, tk), 0)
    col = lax.broadcasted_iota(I32, (tq, tk), 1)
    diag_ok = [col + r * tk <= row for r in range(R)]

    qs = []
    for h in range(H_IDX):
        blk = qi_ref[:, LANES * (h // 2):LANES * (h // 2 + 1)]
        qs.append(jnp.where(lo_half if h % 2 == 0 else ~lo_half, blk, 0.0))
    qstack = jnp.concatenate(qs, axis=0).astype(BF16)
    wb = [jnp.broadcast_to(wi_ref[:, h:h + 1], (tq, tk)) for h in range(H_IDX)]

    def score_chunks(c0, n, masks):
        off = pl.multiple_of(c0 * tk, tk)
        kc = ki_ref[pl.ds(off, n * tk), :].astype(BF16)
        s = _dot_nt(qstack, kc)
        relu = [[jnp.maximum(s[h * tq:(h + 1) * tq, u * tk:(u + 1) * tk], 0.0) for h in range(H_IDX)]
                for u in range(n)]
        score = [wb[0] * r[0] for r in relu]
        for h in range(1, H_IDX):
            score = [sc + wb[h] * r[h] for sc, r in zip(score, relu)]
        for u in range(n):
            key = _score_key(score[u] * _IDX_SCALE)
            if masks is not None:
                key = jnp.where(masks[u], key, INT_MIN)
            key_scr[c0 + u] = key

    def score_group(g, carry):
        score_chunks(g * AG, AG, None)
        return carry

    def score_single(c, carry):
        score_chunks(c, 1, None)
        return carry

    lax.fori_loop(0, cm // AG, score_group, 0)
    lax.fori_loop((cm // AG) * AG, cm, score_single, 0)
    score_chunks(cm, R, diag_ok)
    for u in range(1, AG):
        @pl.when((cl % AG) + u < AG)
        def _():
            key_scr[cl + u] = jnp.full((tq, tk), INT_MIN, I32)

    NP = tq // LANES
    prow = [slice(LANES * n, LANES * (n + 1)) for n in range(NP)]
    colp = col[0:LANES]

    def count_ge(cands):
        accs = []
        for n in range(NP):
            def body(g, acc, n=n):
                for u in range(AG):
                    acc = acc + (key_scr[g * AG + u, prow[n], :] >= cands[n]).astype(F32)
                return acc
            accs.append(lax.fori_loop(0, cl // AG + 1, body, jnp.zeros((LANES, tk), F32)))
        return tuple(jnp.sum(a, axis=1, keepdims=True).astype(I32) for a in accs)

    def count_eq_lt(thrs, jcands):
        accs = []
        for n in range(NP):
            def body(g, acc, n=n):
                for u in range(AG):
                    c = g * AG + u
                    hit = (key_scr[c, prow[n], :] == thrs[n]) & (colp + c * tk < jcands[n])
                    acc = acc + hit.astype(F32)
                return acc
            accs.append(lax.fori_loop(0, cl // AG + 1, body, jnp.zeros((LANES, tk), F32)))
        return tuple(jnp.sum(a, axis=1, keepdims=True).astype(I32) for a in accs)

    thr = _topk_threshold(count_ge, count_eq_lt, NP, LANES, LANES, n_sel, idx_bits, jl_scr)
    jl = [jl_scr[n][:, 0:1] for n in range(NP)]

    qs = []
    for h in range(H_DSA):
        blk = qb_ref[:, LANES * (h // 2):LANES * (h // 2 + 1)] * ATT_SCALE
        g = h // (H_DSA // KV_DSA)
        if (h % 2) != g:
            blk = pltpu.roll(blk, 64, 1)
        qs.append(jnp.where(lo_half if g == 0 else ~lo_half, blk, 0.0))
    qstack2 = jnp.concatenate(qs, axis=0).astype(BF16)

    ta = AG * tk
    rowa = lax.broadcasted_iota(I32, (tq, ta), 0)
    cola = lax.broadcasted_iota(I32, (tq, ta), 1)
    for n in range(NP):
        row_scr[0, prow[n], :] = jnp.broadcast_to(thr[n], (LANES, LANES))
        row_scr[1, prow[n], :] = jnp.broadcast_to(jl[n], (LANES, LANES))
    thr_a, jl_a = row_scr[0, :, 0:1], row_scr[1, :, 0:1]

    def att_group(c, carry, last):
        m, l, acc = carry
        off = pl.multiple_of(c * ta, ta)
        kblk = jnp.concatenate([key_scr[c * AG + u] for u in range(AG)], axis=1)
        sel = (kblk > thr_a) | ((kblk == thr_a) & (cola + off <= jl_a))
        if last:
            sel = sel & (cola + off <= rowa + i * tq)
        z = _dot_nt(qstack2, kb_ref[pl.ds(off, ta), :].astype(BF16))
        z = jnp.where(sel[None], z.reshape(H_DSA, tq, ta), NEG_BIG).reshape(H_DSA * tq, ta)
        m_new = jnp.maximum(m, jnp.max(z, axis=1, keepdims=True))
        alpha = jnp.exp2(m - m_new)
        p = jnp.exp2(z - m_new)
        l = alpha * l + jnp.sum(p, axis=1, keepdims=True)
        acc = alpha * acc + _dot(p.astype(BF16), vb_ref[pl.ds(off, ta), :].astype(BF16))
        return m_new, l, acc

    carry = (jnp.full((H_DSA * tq, 1), NEG_BIG, F32), jnp.zeros((H_DSA * tq, 1), F32),
             jnp.zeros((H_DSA * tq, LANES), F32))
    full_groups = (i * tq) // ta
    carry = lax.fori_loop(0, full_groups, lambda c, cr: att_group(c, cr, False), carry)
    for r in range(max(1, tq // ta)):
        carry = att_group(full_groups + r, carry, True)
    m, l, acc = carry
    out = acc / l
    for mblk in range(4):
        parts = []
        for h in (2 * mblk, 2 * mblk + 1):
            o_h = out[h * tq:(h + 1) * tq]
            if (h % 2) != h // (H_DSA // KV_DSA):
                o_h = pltpu.roll(o_h, 64, 1)
            parts.append(o_h)
        o_ref[:, LANES * mblk:LANES * (mblk + 1)] = jnp.where(lo_half, parts[0], parts[1])


def _dsa_prompt(qi, wi, ki2, qb, kb, vb, B, T, tq, n_sel):
    nq = T // tq
    qrow = lambda b, i: (b * nq + i, 0)
    full = lambda b, i: (b, 0)
    AG = 4 if (T // LANES) % 4 == 0 else 1
    return pl.pallas_call(
        functools.partial(_dsa_prompt_kernel, tq=tq, n_sel=n_sel, idx_bits=max(1, (T - 1).bit_length()), AG=AG),
        grid=(B, nq),
        in_specs=[pl.BlockSpec((tq, 512), qrow), pl.BlockSpec((tq, LANES), qrow),
                  pl.BlockSpec((T, LANES), full), pl.BlockSpec((tq, 512), qrow),
                  pl.BlockSpec((T, LANES), full), pl.BlockSpec((T, LANES), full)],
        out_specs=pl.BlockSpec((tq, 512), qrow),
        out_shape=jax.ShapeDtypeStruct(qb.shape, F32),
        scratch_shapes=[pltpu.VMEM((T // LANES, tq, LANES), I32), pltpu.VMEM((tq // LANES, LANES, LANES), I32),
                        pltpu.VMEM((2, tq, LANES), I32)],
        compiler_params=_cparams("parallel", "arbitrary"),
        name="dsa_prompt",
    )(qi, wi, ki2, qb, kb, vb)


def _dsa_score_sample_kernel(pt_ref, q_ref, w_ref, kn_ref, *rest, G, n_pages, n_real, n_sel, idx_bits):
    kc_refs = rest[:G]
    key_ref, thr_ref, jl_ref = rest[G:]
    j = pl.program_id(1)
    qrow = lax.broadcasted_iota(I32, (SROWS, PAGE), 0)
    kcol = lax.broadcasted_iota(I32, (SROWS, PAGE), 1)
    q = q_ref[0].astype(BF16)
    w = w_ref[0]

    def score_keys(s):
        s = jnp.maximum(s, 0.0) * w
        sc = s[0:SROWS]
        for h in range(1, H_IDX):
            sc = sc + s[h * SROWS:(h + 1) * SROWS]
        return _score_key(sc * _IDX_SCALE)

    @pl.when(j < n_pages // G)
    def _():
        kt = jnp.concatenate([kc_refs[s][...] for s in range(G)], axis=1).astype(BF16)
        key = score_keys(_dot(q, kt))
        for s in range(G):
            key_ref[0, j * G + s] = key[:, PAGE * s:PAGE * (s + 1)]

    @pl.when(j == n_pages // G)
    def _():
        key = score_keys(_dot_nt(q, kn_ref[0].astype(BF16)))
        key_ref[0, n_pages] = jnp.where(kcol <= qrow, key, INT_MIN)
        pos = lax.broadcasted_iota(I32, (G, SROWS, PAGE), 0) * PAGE + kcol[None]

        def count_ge(cands):
            cand, = cands

            def body(g, acc):
                blk = key_ref[0, pl.ds(g * G, G)]
                return acc + jnp.sum((blk >= cand[None]).astype(F32), axis=0)
            acc = lax.fori_loop(0, n_pages // G, body, jnp.zeros((SROWS, PAGE), F32))
            acc = acc + (key_ref[0, n_pages] >= cand).astype(F32)
            return (jnp.sum(acc, axis=1, keepdims=True).astype(I32),)

        def count_eq_lt(thrs, jcands):
            (thr,), (jcand,) = thrs, jcands

            def body(g, acc):
                blk = key_ref[0, pl.ds(g * G, G)]
                hit = (blk == thr[None]) & (pos + g * (G * PAGE) < jcand[None])
                return acc + jnp.sum(hit.astype(F32), axis=0)
            acc = lax.fori_loop(0, n_pages // G, body, jnp.zeros((SROWS, PAGE), F32))
            last = (key_ref[0, n_pages] == thr) & (kcol + n_pages * PAGE < jcand)
            return (jnp.sum(acc + last.astype(F32), axis=1, keepdims=True).astype(I32),)

        thr_ref[0], = _topk_threshold(count_ge, count_eq_lt, 1, SROWS, n_real, n_sel, idx_bits, jl_ref)


def _dsa_score_sample(pt, q, w, kn, kc, n_real, n_sel):
    B, NP = pt.shape
    G = _pages_per_step(NP)
    L = (NP + 1) * PAGE
    page = lambda s: pl.BlockSpec((None, D_IDX, PAGE),
                                  lambda b, j, pt: (pt[b, jnp.minimum(j * G + s, NP - 1)], 0, 0))
    grid_spec = pltpu.PrefetchScalarGridSpec(
        num_scalar_prefetch=1,
        grid=(B, NP // G + 1),
        in_specs=([pl.BlockSpec((1, H_IDX * SROWS, D_IDX), lambda b, j, pt: (b, 0, 0)),
                   pl.BlockSpec((1, H_IDX * SROWS, 1), lambda b, j, pt: (b, 0, 0)),
                   pl.BlockSpec((1, PAGE, D_IDX), lambda b, j, pt: (b, 0, 0))]
                  + [page(s) for s in range(G)]),
        out_specs=[pl.BlockSpec((1, NP + 1, SROWS, PAGE), lambda b, j, pt: (b, 0, 0, 0)),
                   pl.BlockSpec((1, SROWS, 1), lambda b, j, pt: (b, 0, 0)),
                   pl.BlockSpec((1, SROWS, 1), lambda b, j, pt: (b, 0, 0))])
    return pl.pallas_call(
        functools.partial(_dsa_score_sample_kernel, G=G, n_pages=NP, n_real=n_real, n_sel=n_sel,
                          idx_bits=max(1, (L - 1).bit_length())),
        grid_spec=grid_spec,
        out_shape=[jax.ShapeDtypeStruct((B, NP + 1, SROWS, PAGE), I32),
                   jax.ShapeDtypeStruct((B, SROWS, 1), I32),
                   jax.ShapeDtypeStruct((B, SROWS, 1), I32)],
        compiler_params=_cparams("parallel", "arbitrary"),
        name="dsa_score_sample",
    )(pt, q, w, kn, *([kc] * G))


_GR = (H_DSA // KV_DSA) * SROWS


def _dsa_att_sample_kernel(pt_ref, q_ref, key_ref, thr_ref, jl_ref, kn_ref, vn_ref, *rest, G, n_pages):
    kc_refs, vc_refs = rest[:G], rest[G:2 * G]
    o_ref, m_scr, l_scr, acc_scr = rest[2 * G:]
    j = pl.program_id(1)
    qrow = lax.broadcasted_iota(I32, (SROWS, PAGE), 0)
    kcol = lax.broadcasted_iota(I32, (SROWS, PAGE), 1)
    nrep = H_DSA // KV_DSA

    @pl.when(j == 0)
    def _():
        m_scr[...] = jnp.full(m_scr.shape, NEG_BIG, F32)
        l_scr[...] = jnp.zeros(l_scr.shape, F32)
        acc_scr[...] = jnp.zeros(acc_scr.shape, F32)

    def attend(zs, vs, sel, v_transposed=True):
        groups = range(KV_DSA)
        n = zs[0].shape[1]
        zs = [jnp.where(sel[None], z.reshape(nrep, SROWS, n), NEG_BIG).reshape(_GR, n) for z in zs]
        m = [m_scr[g][:, 0:1] for g in groups]
        l = [l_scr[g][:, 0:1] for g in groups]
        m_new = [jnp.maximum(m[g], jnp.max(zs[g], axis=1, keepdims=True)) for g in groups]
        alpha = [jnp.exp2(m[g] - m_new[g]) for g in groups]
        p = [jnp.exp2(zs[g] - m_new[g]) for g in groups]
        l = [alpha[g] * l[g] + jnp.sum(p[g], axis=1, keepdims=True) for g in groups]
        pb = [x.astype(BF16) for x in p]
        vb = [x.astype(BF16) for x in vs]
        pv = [_dot_nt(pb[g], vb[g]) if v_transposed else _dot(pb[g], vb[g]) for g in groups]
        for g in groups:
            acc_scr[g] = alpha[g] * acc_scr[g] + pv[g]
            m_scr[g] = jnp.broadcast_to(m_new[g], (_GR, LANES))
            l_scr[g] = jnp.broadcast_to(l[g], (_GR, LANES))

    def qgroup(g):
        return (q_ref[0, g] * ATT_SCALE).astype(BF16)

    thr = thr_ref[0]
    jl = jl_ref[0]

    @pl.when(j < n_pages // G)
    def _():
        sels = []
        for s in range(G):
            kblk = key_ref[0, j * G + s]
            sels.append((kblk > thr) | ((kblk == thr) & (kcol + (j * G + s) * PAGE <= jl)))
        sel = jnp.concatenate(sels, axis=1)
        kts = [jnp.concatenate([kc_refs[s][g] for s in range(G)], axis=1).astype(BF16) for g in range(KV_DSA)]
        vts = [jnp.concatenate([vc_refs[s][g] for s in range(G)], axis=1) for g in range(KV_DSA)]
        attend([_dot(qgroup(g), kts[g]) for g in range(KV_DSA)], vts, sel)

    @pl.when(j == n_pages // G)
    def _():
        off = n_pages * PAGE
        kblk = key_ref[0, n_pages]
        sel = ((kblk > thr) | ((kblk == thr) & (kcol + off <= jl))) & (kcol <= qrow)
        kns = [kn_ref[0, :, D_HEAD * g:D_HEAD * (g + 1)].astype(BF16) for g in range(KV_DSA)]
        vns = [vn_ref[0, :, D_HEAD * g:D_HEAD * (g + 1)] for g in range(KV_DSA)]
        attend([_dot_nt(qgroup(g), kns[g]) for g in range(KV_DSA)], vns, sel, v_transposed=False)
        for g in range(KV_DSA):
            o_ref[0, g] = acc_scr[g] / l_scr[g][:, 0:1]


def _dsa_att_sample(pt, q, keys, thr, jl, kn, vn, kc, vc):
    B, NP = pt.shape
    G = _pages_per_step(NP)
    page = lambda s: pl.BlockSpec((None, KV_DSA, D_HEAD, PAGE),
                                  lambda b, j, pt: (pt[b, jnp.minimum(j * G + s, NP - 1)], 0, 0, 0))
    per_b3 = lambda b, j, pt: (b, 0, 0)
    grid_spec = pltpu.PrefetchScalarGridSpec(
        num_scalar_prefetch=1,
        grid=(B, NP // G + 1),
        in_specs=([pl.BlockSpec((1, KV_DSA, _GR, D_HEAD), lambda b, j, pt: (b, 0, 0, 0)),
                   pl.BlockSpec((1, NP + 1, SROWS, PAGE), lambda b, j, pt: (b, 0, 0, 0)),
                   pl.BlockSpec((1, SROWS, 1), per_b3),
                   pl.BlockSpec((1, SROWS, 1), per_b3),
                   pl.BlockSpec((1, PAGE, LANES), per_b3),
                   pl.BlockSpec((1, PAGE, LANES), per_b3)]
                  + [page(s) for s in range(G)] * 2),
        out_specs=pl.BlockSpec((1, KV_DSA, _GR, D_HEAD), lambda b, j, pt: (b, 0, 0, 0)),
        scratch_shapes=[pltpu.VMEM((KV_DSA, _GR, LANES), F32), pltpu.VMEM((KV_DSA, _GR, LANES), F32),
                        pltpu.VMEM((KV_DSA, _GR, D_HEAD), F32)])
    return pl.pallas_call(
        functools.partial(_dsa_att_sample_kernel, G=G, n_pages=NP),
        grid_spec=grid_spec,
        out_shape=jax.ShapeDtypeStruct((B, KV_DSA, _GR, D_HEAD), F32),
        compiler_params=_cparams("parallel", "arbitrary"),
        name="dsa_att_sample",
    )(pt, q, keys, thr, jl, kn, vn, *([kc] * G), *([vc] * G))


def _mm_norm_res_kernel(*refs, n):
    a_refs, w_refs = refs[:n], refs[n:2 * n]
    g_ref, x_ref, o_ref = refs[2 * n:]
    acc = _dot(a_refs[0][...].astype(BF16), w_refs[0][...])
    for a_ref, w_ref in zip(a_refs[1:], w_refs[1:]):
        acc = acc + _dot(a_ref[...].astype(BF16), w_ref[...])
    o_ref[...] = x_ref[...] + _rms(acc, g_ref[...])


def _mm_norm_res(a_list, w_list, g, x, tm, name):
    M = x.shape[0]
    n = len(a_list)
    row = lambda i: (i, 0)
    const = lambda i: (0, 0)
    return pl.pallas_call(
        functools.partial(_mm_norm_res_kernel, n=n),
        grid=(M // tm,),
        in_specs=([pl.BlockSpec((tm, a.shape[1]), row) for a in a_list]
                  + [pl.BlockSpec(w.shape, const) for w in w_list]
                  + [pl.BlockSpec((1, D_MODEL), const), pl.BlockSpec((tm, D_MODEL), row)]),
        out_specs=pl.BlockSpec((tm, D_MODEL), row),
        out_shape=jax.ShapeDtypeStruct((M, D_MODEL), F32),
        compiler_params=_cparams("parallel"),
        name=name,
    )(*a_list, *w_list, g, x)


_GELU_C = math.sqrt(2.0 / math.pi)
FFN_SUB = 256


def _gelu_tanh(x):
    return x * (0.5 * (1.0 + jnp.tanh(_GELU_C * (x + 0.044715 * (x * x * x)))))


def _ffn_up_kernel(x_ref, g_ref, wg_ref, wv_ref, cwg_ref, cwv_ref, cbg_ref, cbv_ref, pg_ref, pv_ref,
                   act_ref, og_ref, ov_ref, eg_scr, ev_scr, *, tm, T, streamed):
    i = pl.program_id(1)
    h = _rms(x_ref[...], g_ref[...]).astype(BF16)
    trow = lax.broadcasted_iota(I32, (tm, 1), 0) % T

    tn = act_ref.shape[1]
    if streamed:
        @pl.when((i * tm) % T == 0)
        def _():
            eg_scr[6:8, :] = pg_ref[...]
            ev_scr[6:8, :] = pv_ref[...]
    else:
        eg_scr[0:8, :] = jnp.zeros((8, tn), F32)
        ev_scr[0:8, :] = jnp.zeros((8, tn), F32)

    rid = lax.broadcasted_iota(I32, (8, 1), 0)

    def conv(u, cs, cw_ref, cb_ref, p_ref, o_ref, e_scr):
        if streamed:
            o_ref[:, cs] = u[tm - 2:tm, :]
            prev2, prev1 = e_scr[6:7, cs], e_scr[7:8, cs]
            r1 = pltpu.roll(u, 1, 0)
            r2 = pltpu.roll(u, 2, 0)
            head1 = jnp.where(rid == 0, prev1, r1[0:8])
            head2 = jnp.where(rid == 0, prev2, jnp.where(rid == 1, prev1, r2[0:8]))
            u1 = jnp.concatenate([head1, r1[8:]], axis=0)
            u2 = jnp.concatenate([head2, r2[8:]], axis=0)
            e_scr[6:8, cs] = u[tm - 2:tm, :]
        else:
            e_scr[8:8 + tm, cs] = u
            o_ref[:, cs] = u
            u1 = jnp.where(trow >= 1, e_scr[7:7 + tm, cs], p_ref[0, :, cs])
            u2 = jnp.where(trow >= 2, e_scr[6:6 + tm, cs], p_ref[1, :, cs])
        return cb_ref[:, cs] + cw_ref[0:1, cs] * u2 + cw_ref[1:2, cs] * u1 + cw_ref[2:3, cs] * u

    subs = [slice(c0, min(c0 + FFN_SUB, tn)) for c0 in range(0, tn, FFN_SUB)]
    dots = lambda cs: (_dot(h, wg_ref[:, cs]), _dot(h, wv_ref[:, cs]))
    cur = dots(subs[0])
    for n, cs in enumerate(subs):
        nxt = dots(subs[n + 1]) if n + 1 < len(subs) else None
        gate = conv(cur[0], cs, cwg_ref, cbg_ref, pg_ref, og_ref, eg_scr)
        val = conv(cur[1], cs, cwv_ref, cbv_ref, pv_ref, ov_ref, ev_scr)
        act_ref[:, cs] = (_gelu_tanh(gate) * val).astype(BF16)
        cur = nxt


def _ffn_up(x, g, w_up, conv_w, conv_b, prev, T, tm, tn):
    M = x.shape[0]
    nj = D_FF // tn
    streamed = T >= tm
    tpb = max(T // tm, 1)
    if streamed:
        p_spec = lambda off: pl.BlockSpec((None, 2, tn), lambda j, i: (i // tpb, 0, j + off))
        o_shape = jax.ShapeDtypeStruct((M // T, 2, D_FF), F32)
        o_spec = pl.BlockSpec((None, 2, tn), lambda j, i: (i // tpb, 0, j))
    else:
        p_spec = lambda off: pl.BlockSpec((2, tm, tn), lambda j, i: (0, i, j + off))
        o_shape = jax.ShapeDtypeStruct((M, D_FF), F32)
        o_spec = pl.BlockSpec((tm, tn), lambda j, i: (i, j))
    col = lambda rows, off: pl.BlockSpec((rows, tn), lambda j, i: (0, j + off))
    return pl.pallas_call(
        functools.partial(_ffn_up_kernel, tm=tm, T=T, streamed=streamed),
        grid=(nj, M // tm),
        in_specs=[pl.BlockSpec((tm, D_MODEL), lambda j, i: (i, 0)),
                  pl.BlockSpec((1, D_MODEL), lambda j, i: (0, 0)),
                  col(D_MODEL, 0), col(D_MODEL, nj), col(3, 0), col(3, nj), col(1, 0), col(1, nj),
                  p_spec(0), p_spec(nj)],
        out_specs=[pl.BlockSpec((tm, tn), lambda j, i: (i, j)), o_spec, o_spec],
        out_shape=[jax.ShapeDtypeStruct((M, D_FF), BF16), o_shape, o_shape],
        scratch_shapes=[pltpu.VMEM((tm + 8, tn), F32), pltpu.VMEM((tm + 8, tn), F32)],
        compiler_params=_cparams("parallel", "arbitrary"),
        name="ffn_up",
    )(x, g, w_up, w_up, conv_w, conv_w, conv_b, conv_b, prev, prev)


def _rk_pre_kernel(x_ref, g_ref, sp_ref, mix_ref, wr_ref, wk_ref, wv_ref, w1_ref, w2_ref, a1_ref, a2_ref,
                   g1_ref, g2_ref, vec_ref,
                   hn_ref, r_ref, wl_ref, kf_ref, v_ref, av_ref, bv_ref, gg_ref, e_scr, *, tm, T, streamed):
    i = pl.program_id(0)
    h = _rms(x_ref[...], g_ref[...])
    hn_ref[...] = h
    e_scr[8:8 + tm, :] = h
    if streamed:
        @pl.when((i * tm) % T == 0)
        def _():
            e_scr[7:8, :] = sp_ref[...]
        xp = e_scr[7:7 + tm, :]
        e_scr[7:8, :] = h[tm - 1:tm, :]
    else:
        e_scr[0:8, :] = jnp.zeros((8, D_MODEL), F32)
        trow = lax.broadcasted_iota(I32, (tm, 1), 0) % T
        xp = jnp.where(trow >= 1, e_scr[7:7 + tm, :], sp_ref[...])
    xx = xp - h
    mixed = lambda j: (h + xx * mix_ref[j:j + 1, :]).astype(BF16)
    w0, a0, k_k, k_a = vec_ref[0:1, :], vec_ref[1:2, :], vec_ref[2:3, :], vec_ref[3:4, :]
    r = _dot(mixed(0), wr_ref[...])
    k = _dot(mixed(2), wk_ref[...])
    v = _dot(mixed(3), wv_ref[...])
    lw = w0 + _dot(jnp.tanh(_dot(mixed(1), w1_ref[...])).astype(BF16), w2_ref[...])
    w_log = -(jnp.maximum(-lw, 0.0) + jnp.log(1.0 + jnp.exp(-jnp.abs(lw)))) - 0.5
    wl_ref[...] = -jnp.exp(w_log)
    a = jax.nn.sigmoid(a0 + _dot(_dot(mixed(4), a1_ref[...]).astype(BF16), a2_ref[...]))
    gg_ref[...] = _dot(jax.nn.sigmoid(_dot(mixed(5), g1_ref[...])).astype(BF16), g2_ref[...])
    kk = k * k_k
    bd = _seg_ones()
    kk = kk / jnp.maximum(jnp.sqrt(_seg64_sum(kk * kk, bd)), 1e-12)
    r_ref[...] = r
    v_ref[...] = v
    kf_ref[...] = k * (1.0 + (a - 1.0) * k_a)
    av_ref[...] = -kk
    bv_ref[...] = kk * a


def _rk_pre(x, g, sp, mix, wr, wk, wv, w1, w2, a1, a2, g1, g2, vecs, T, tm):
    M = x.shape[0]
    streamed = T >= tm
    tpb = max(T // tm, 1)
    row = lambda i: (i, 0)
    const = lambda i: (0, 0)
    sp_spec = (pl.BlockSpec((None, 1, D_MODEL), lambda i: (i // tpb, 0, 0)) if streamed
               else pl.BlockSpec((tm, D_MODEL), row))
    full = lambda a: pl.BlockSpec(a.shape, const)
    return pl.pallas_call(
        functools.partial(_rk_pre_kernel, tm=tm, T=T, streamed=streamed),
        grid=(M // tm,),
        in_specs=[pl.BlockSpec((tm, D_MODEL), row), pl.BlockSpec((1, D_MODEL), const), sp_spec,
                  full(mix), full(wr), full(wk), full(wv), full(w1), full(w2), full(a1), full(a2),
                  full(g1), full(g2), full(vecs)],
        out_specs=[pl.BlockSpec((tm, D_MODEL), row)] * 8,
        out_shape=[jax.ShapeDtypeStruct((M, D_MODEL), F32)] * 8,
        scratch_shapes=[pltpu.VMEM((tm + 8, D_MODEL), F32)],
        compiler_params=_cparams("arbitrary"),
        name="rk_pre",
    )(x, g, sp, mix, wr, wk, wv, w1, w2, a1, a2, g1, g2, vecs)


def _bdot(a, b):
    return _dot(a.astype(BF16), b.astype(BF16))


def _rk_scan_kernel(r_ref, wl_ref, k_ref, v_ref, a_ref, b_ref, s0_ref, y_ref, sf_ref, s_scr, *, C, npair):
    c = pl.program_id(2)

    @pl.when(c == 0)
    def _():
        s_scr[...] = s0_ref[0]

    lane = lax.broadcasted_iota(I32, (1, LANES), 1)
    m0 = lane < 64
    rr = lax.broadcasted_iota(I32, (C, C), 0)
    cc = lax.broadcasted_iota(I32, (C, C), 1)
    tri_incl = (cc <= rr).astype(BF16)
    r2 = lax.broadcasted_iota(I32, (2 * C, 2 * C), 0)
    c2 = lax.broadcasted_iota(I32, (2 * C, 2 * C), 1)
    strict = (r2 % C) > (c2 % C)
    incl = (r2 % C) >= (c2 % C)
    incl2 = jnp.concatenate([incl, incl], axis=1)
    eye = (r2 == c2).astype(F32)
    n_double = max(int(math.log2(C)) - 1, 0)

    def stack2(z):
        return jnp.concatenate([jnp.where(m0, z, 0.0), jnp.where(m0, 0.0, z)], axis=0)

    pairs = range(npair)
    sls = [slice(LANES * p, LANES * (p + 1)) for p in pairs]
    wl = [wl_ref[:, sl] for sl in sls]
    ld = []
    for p in pairs:
        wl_hi, wl_lo = _split(wl[p])
        ld.append(_dot(tri_incl, wl_hi) + _dot(tri_incl, wl_lo))
    dfull = [jnp.exp(x) for x in ld]
    dinv = [jnp.exp(-x) for x in ld]
    As = [stack2(a_ref[:, sls[p]] * jnp.exp(ld[p] - wl[p])).astype(BF16) for p in pairs]
    Bs = [stack2(b_ref[:, sls[p]] * dinv[p]) for p in pairs]
    Ks = [stack2(k_ref[:, sls[p]] * dinv[p]) for p in pairs]
    Rs = [stack2(r_ref[:, sls[p]] * dfull[p]).astype(BF16) for p in pairs]
    Vs = [stack2(v_ref[:, sls[p]]) for p in pairs]
    Vb = [x.astype(BF16) for x in Vs]
    BK = [jnp.concatenate([Bs[p], Ks[p]], axis=0).astype(BF16) for p in pairs]
    G1 = [_dot_nt(As[p], BK[p]) for p in pairs]
    G2 = [_dot_nt(Rs[p], BK[p]) for p in pairs]
    Lab = [jnp.where(strict, g[:, :2 * C], 0.0) for g in G1]
    Lak = [jnp.where(strict, g[:, 2 * C:], 0.0).astype(BF16) for g in G1]
    Mr = [jnp.where(incl2, g, 0.0).astype(BF16) for g in G2]
    Tm = [eye + x for x in Lab]
    if n_double > 0:
        Pb = [x.astype(BF16) for x in Lab]
        Pb = [_dot(x, x).astype(BF16) for x in Pb]
        for rnd in range(n_double):
            Tb = [x.astype(BF16) for x in Tm]
            if rnd + 1 < n_double:
                X = [_dot(Pb[p], jnp.concatenate([Pb[p], Tb[p]], axis=1)) for p in pairs]
                Pb = [x[:, :2 * C].astype(BF16) for x in X]
                Tm = [Tm[p] + X[p][:, 2 * C:] for p in pairs]
            else:
                Tm = [Tm[p] + _dot(Pb[p], Tb[p]) for p in pairs]
    S = [s_scr[p] for p in pairs]
    Sb = [x.astype(BF16) for x in S]
    rhs = [_dot_nt(As[p], Sb[p]) + _dot(Lak[p], Vb[p]) for p in pairs]
    U = [_bdot(Tm[p], rhs[p]) for p in pairs]
    UV = [jnp.concatenate([U[p], Vs[p]], axis=0) for p in pairs]
    UVb = [x.astype(BF16) for x in UV]
    Ys = [_dot_nt(Rs[p], Sb[p]) + _dot(Mr[p], UVb[p]) for p in pairs]
    upd = [_dot(UV[p].T.astype(BF16), BK[p]) for p in pairs]
    for p in pairs:
        y_ref[:, sls[p]] = Ys[p][:C] + Ys[p][C:]
        s_scr[p] = (S[p] + upd[p]) * dfull[p][C - 1:C, :]

    @pl.when(c == pl.num_programs(2) - 1)
    def _():
        sf_ref[0] = s_scr[...]


def _rk_scan(r, wl, k, v, a, b, s0, B, T, C, npair):
    nc = T // C
    ng = 8 // npair
    blk = pl.BlockSpec((C, LANES * npair), lambda bb, g, c: (bb * nc + c, g))
    s_spec = pl.BlockSpec((1, npair, LANES, LANES), lambda bb, g, c: (bb, g, 0, 0))
    return pl.pallas_call(
        functools.partial(_rk_scan_kernel, C=C, npair=npair),
        grid=(B, ng, nc),
        in_specs=[blk] * 6 + [s_spec],
        out_specs=[blk, s_spec],
        out_shape=[jax.ShapeDtypeStruct(r.shape, F32), jax.ShapeDtypeStruct(s0.shape, F32)],
        scratch_shapes=[pltpu.VMEM((npair, LANES, LANES), F32)],
        compiler_params=_cparams("parallel", "parallel", "arbitrary"),
        name="rk_scan",
    )(r, wl, k, v, a, b, s0)


def _rk_post_kernel(y_ref, r_ref, kf_ref, v_ref, gg_ref, vec_ref, o_ref):
    bd = _seg_ones()
    ln_w, ln_b, r_k = vec_ref[0:1, :], vec_ref[1:2, :], vec_ref[2:3, :]
    y = y_ref[...]
    mu = _seg64_sum(y, bd) * (1.0 / RK_N)
    d = y - mu
    var = _seg64_sum(d * d, bd) * (1.0 / RK_N)
    yn = d * lax.rsqrt(var + RK_GN_EPS) * ln_w + ln_b
    bonus = _seg64_sum(r_ref[...] * kf_ref[...] * r_k, bd) * v_ref[...]
    o_ref[...] = ((yn + bonus) * gg_ref[...]).astype(BF16)


def _rk_post(y, r, kf, v, gg, vecs, tm):
    M = y.shape[0]
    row = lambda i: (i, 0)
    return pl.pallas_call(
        _rk_post_kernel,
        grid=(M // tm,),
        in_specs=[pl.BlockSpec((tm, D_MODEL), row)] * 5 + [pl.BlockSpec(vecs.shape, lambda i: (0, 0))],
        out_specs=pl.BlockSpec((tm, D_MODEL), row),
        out_shape=jax.ShapeDtypeStruct((M, D_MODEL), BF16),
        compiler_params=_cparams("parallel"),
        name="rk_post",
    )(y, r, kf, v, gg, vecs)


def _row_tile(M):
    return min(256, M)


def _pad_rows(a, n):
    return jnp.pad(a, ((0, 0), (0, n - a.shape[1])) + ((0, 0),) * (a.ndim - 2))


def _attn_layer_prompt(x, B, T, g_pre, w_in, w_outs, g_post):
    M = B * T
    tm = _row_tile(M)
    cos, sin = _rope_tables(jnp.arange(T, dtype=I32))
    qa, ka, va, qb, kb, vb, qi, ki2, wi = _attn_proj(x, g_pre, w_in, cos, sin, tm)
    n_sel = min(DSA_TOPK, T // 4)
    oa = _sb_prompt(qa, ka, va, B, T, min(256, T))
    ob = _dsa_prompt(qi, wi, ki2, qb, kb, vb, B, T, min(128, T), n_sel)
    x = _mm_norm_res([oa, ob], w_outs, g_post, x, tm, "attn_out")
    rows = (ka.reshape(1, B, T, H_SB, D_HEAD), va.reshape(1, B, T, H_SB, D_HEAD),
            kb.reshape(1, B, T, KV_DSA, D_HEAD), vb.reshape(1, B, T, KV_DSA, D_HEAD),
            ki2[:, :D_IDX].reshape(1, B, T, D_IDX))
    return x, rows


def _attn_layer_sample(x, B, T, P, page_table, caches, g_pre, w_in, w_outs, g_post):
    M = B * T
    cos, sin = _rope_tables(P + jnp.arange(T, dtype=I32))
    cos, sin = jnp.tile(cos, (B, 1)), jnp.tile(sin, (B, 1))
    qa, ka, va, qb, kb, vb, qi, ki2, wi = _attn_proj(x, g_pre, w_in, cos, sin, M)
    c_sb_k, c_sb_v, c_dsa_k, c_dsa_v, c_idx = caches
    n_pool = c_sb_k.shape[0]
    n_sel = min(DSA_TOPK, (P + T) // 4)
    b3 = lambda a: a.reshape(B, T, a.shape[-1])

    row_minor = lambda c: jnp.moveaxis(c, 1, -1)
    oa = _sb_sample(page_table, _pad_rows(b3(qa), SROWS), _pad_rows(b3(ka), PAGE), _pad_rows(b3(va), PAGE),
                    row_minor(c_sb_k), row_minor(c_sb_v), T)
    oa = oa[:, :T].reshape(M, H_SB * D_HEAD)

    qi_s = _pad_rows(b3(qi).reshape(B, T, H_IDX, D_IDX), SROWS).transpose(0, 2, 1, 3).reshape(B, H_IDX * SROWS, D_IDX)
    wi_s = _pad_rows(b3(wi)[:, :, :H_IDX], SROWS).transpose(0, 2, 1).reshape(B, H_IDX * SROWS, 1)
    keys, thr, jl = _dsa_score_sample(page_table, qi_s, wi_s, _pad_rows(b3(ki2)[:, :, :D_IDX], PAGE),
                                      row_minor(c_idx), T, n_sel)
    qb_s = _pad_rows(b3(qb).reshape(B, T, H_DSA, D_HEAD), SROWS).transpose(0, 2, 1, 3).reshape(B, KV_DSA, _GR, D_HEAD)
    ob = _dsa_att_sample(page_table, qb_s, keys, thr, jl, _pad_rows(b3(kb), PAGE), _pad_rows(b3(vb), PAGE),
                         row_minor(c_dsa_k), row_minor(c_dsa_v))
    ob = ob.reshape(B, H_DSA, SROWS, D_HEAD)[:, :, :T].transpose(0, 2, 1, 3).reshape(M, H_DSA * D_HEAD)

    x = _mm_norm_res([oa, ob], w_outs, g_post, x, M, "attn_out")
    rows = (ka.reshape(1, B, T, H_SB, D_HEAD), va.reshape(1, B, T, H_SB, D_HEAD),
            kb.reshape(1, B, T, KV_DSA, D_HEAD), vb.reshape(1, B, T, KV_DSA, D_HEAD),
            ki2[:, :D_IDX].reshape(1, B, T, D_IDX))
    return x, rows


def _pair_state(S):
    B = S.shape[0]
    S = S.reshape(B, 8, 2, RK_N, RK_N)
    z = jnp.zeros_like(S[:, :, 0])
    top = jnp.concatenate([S[:, :, 0], z], axis=-1)
    bot = jnp.concatenate([z, S[:, :, 1]], axis=-1)
    return jnp.concatenate([top, bot], axis=-2)


def _unpair_state(Sp):
    B = Sp.shape[0]
    return jnp.stack([Sp[:, :, :RK_N, :RK_N], Sp[:, :, RK_N:, RK_N:]], axis=2).reshape(B, RK_H, RK_N, RK_N)


RK_CHUNK = 64
RK_PAIRS = 8


def _rwkv_layer(x, B, T, shift_prev, S0, g_pre, prm, g_post):
    M = B * T
    tm = _row_tile(M)
    (mix, wr, wk, wv, wo, w1, w2, a1, a2, g1, g2, vec_pre, vec_post) = prm
    if T >= tm:
        sp = shift_prev.reshape(B, 1, D_MODEL)
    else:
        sp = jnp.repeat(shift_prev, T, axis=0)
    hn, r, wl, kf, v, av, bv, gg = _rk_pre(x, g_pre, sp, mix, wr, wk, wv, w1, w2, a1, a2, g1, g2, vec_pre, T, tm)
    Tp = -(-T // RK_CHUNK) * RK_CHUNK
    if Tp != T:
        padt = lambda a: _pad_rows(a.reshape(B, T, D_MODEL), Tp).reshape(B * Tp, D_MODEL)
        y, Sf = _rk_scan(padt(r), padt(wl), padt(kf), padt(v), padt(av), padt(bv), _pair_state(S0), B, Tp, RK_CHUNK, RK_PAIRS)
        y = y.reshape(B, Tp, D_MODEL)[:, :T].reshape(M, D_MODEL)
    else:
        y, Sf = _rk_scan(r, wl, kf, v, av, bv, _pair_state(S0), B, T, RK_CHUNK, RK_PAIRS)
    z = _rk_post(y, r, kf, v, gg, vec_post, tm)
    x = _mm_norm_res([z], [wo], g_post, x, tm, "rk_out")
    shift = hn.reshape(B, T, D_MODEL)[:, -1]
    return x, shift, _unpair_state(Sf)


def _ffn_layer(x, B, T, prev, g_pre, w_up, conv_w, conv_b, w_down, g_post):
    M = B * T
    tm = _row_tile(M)
    tn = D_FF // 2
    if T >= tm:
        act, cg, cv = _ffn_up(x, g_pre, w_up, conv_w, conv_b, prev, T, tm, tn)
        conv_state = jnp.concatenate([cg, cv], axis=-1)
    else:
        zeros = jnp.zeros((B, T, 2 * D_FF), F32)
        p1 = zeros.at[:, 0].set(prev[:, 1])
        p2 = zeros.at[:, 0].set(prev[:, 0]).at[:, 1].set(prev[:, 1])
        pp = jnp.stack([p1.reshape(M, -1), p2.reshape(M, -1)])
        act, ug, uv = _ffn_up(x, g_pre, w_up, conv_w, conv_b, pp, T, tm, tn)
        u = jnp.concatenate([ug, uv], axis=-1).reshape(B, T, 2 * D_FF)
        conv_state = jnp.concatenate([prev, u], axis=1)[:, -2:]
    x = _mm_norm_res([act], [w_down], g_post, x, tm, "ffn_down")
    return x, conv_state


def _forward(x_prompt, x_sample, cache_sb_k, cache_sb_v, cache_dsa_k, cache_dsa_v, cache_idx_k, page_table,
             state_wkv, state_shift, state_ffn_conv,
             norm_mix_pre, norm_mix_post, norm_ffn_pre, norm_ffn_post,
             att_w_in, att_w_out,
             rk_mix, rk_w_r, rk_w_k, rk_w_v, rk_w_o, rk_w0, rk_w1, rk_w2, rk_a0, rk_a1, rk_a2,
             rk_g1, rk_g2, rk_k_k, rk_k_a, rk_r_k, rk_ln_w, rk_ln_b,
             ffn_w_up, ffn_conv_w, ffn_conv_b, ffn_w_down):
    Bp, Tp, D = x_prompt.shape
    Bs, Ts, _ = x_sample.shape
    P = page_table.shape[1] * PAGE
    depth = norm_mix_pre.shape[0]
    xp = x_prompt.reshape(Bp * Tp, D)
    xs = x_sample.reshape(Bs * Ts, D)
    vrow = lambda a: a.reshape(1, -1)
    bf = lambda a: a.astype(BF16)
    att_p, att_s, wkv_p, wkv_s, sh_p, sh_s, cv_p, cv_s = [], [], [], [], [], [], [], []
    for i in range(depth):
        li = i // 2
        g_pre, g_post = vrow(norm_mix_pre[i]), vrow(norm_mix_post[i])
        if i % 2 == 0:
            w_in = _pack_w_in(att_w_in[li])
            w_outs = [bf(att_w_out[li][:H_SB * D_HEAD]), bf(att_w_out[li][H_SB * D_HEAD:])]
            caches = (cache_sb_k[li], cache_sb_v[li], cache_dsa_k[li], cache_dsa_v[li], cache_idx_k[li])
            xp, rows_p = _attn_layer_prompt(xp, Bp, Tp, g_pre, w_in, w_outs, g_post)
            xs, rows_s = _attn_layer_sample(xs, Bs, Ts, P, page_table, caches, g_pre, w_in, w_outs, g_post)
            att_p.append(rows_p)
            att_s.append(rows_s)
        else:
            vec_pre = jnp.stack([rk_w0[li], rk_a0[li], rk_k_k[li], rk_k_a[li]])
            vec_post = jnp.stack([rk_ln_w[li], rk_ln_b[li], rk_r_k[li].reshape(-1)])
            prm = (rk_mix[li], bf(rk_w_r[li]), bf(rk_w_k[li]), bf(rk_w_v[li]), bf(rk_w_o[li]),
                   bf(rk_w1[li]), bf(rk_w2[li]), bf(rk_a1[li]), bf(rk_a2[li]), bf(rk_g1[li]), bf(rk_g2[li]),
                   vec_pre, vec_post)
            xp, shp, Sp = _rwkv_layer(xp, Bp, Tp, jnp.zeros((Bp, D), F32), jnp.zeros((Bp, RK_H, RK_N, RK_N), F32),
                                      g_pre, prm, g_post)
            xs, shs, Ss = _rwkv_layer(xs, Bs, Ts, state_shift[li], state_wkv[li], g_pre, prm, g_post)
            wkv_p.append(Sp)
            wkv_s.append(Ss)
            sh_p.append(shp)
            sh_s.append(shs)
        f_pre, f_post = vrow(norm_ffn_pre[i]), vrow(norm_ffn_post[i])
        w_up, w_down = bf(ffn_w_up[i]), bf(ffn_w_down[i])
        cb = vrow(ffn_conv_b[i])
        xp, cp = _ffn_layer(xp, Bp, Tp, jnp.zeros((Bp, 2, 2 * D_FF), F32), f_pre, w_up, ffn_conv_w[i], cb, w_down, f_post)
        xs, cs = _ffn_layer(xs, Bs, Ts, state_ffn_conv[i], f_pre, w_up, ffn_conv_w[i], cb, w_down, f_post)
        cv_p.append(cp)
        cv_s.append(cs)
    cat = lambda rows, j: jnp.concatenate([r[j] for r in rows], axis=0)
    return (xp.reshape(Bp, Tp, D), xs.reshape(Bs, Ts, D),
            cat(att_p, 0), cat(att_s, 0), cat(att_p, 1), cat(att_s, 1),
            cat(att_p, 2), cat(att_s, 2), cat(att_p, 3), cat(att_s, 3),
            cat(att_p, 4), cat(att_s, 4),
            jnp.stack(wkv_p), jnp.stack(wkv_s), jnp.stack(sh_p), jnp.stack(sh_s),
            jnp.stack(cv_p), jnp.stack(cv_s))


def kernel(x_prompt, x_sample, cache_sb_k, cache_sb_v, cache_dsa_k, cache_dsa_v, cache_idx_k, page_table, state_wkv, state_shift, state_ffn_conv, norm_mix_pre, norm_mix_post, norm_ffn_pre, norm_ffn_post, att_w_in, att_w_out, rk_mix, rk_w_r, rk_w_k, rk_w_v, rk_w_o, rk_w0, rk_w1, rk_w2, rk_a0, rk_a1, rk_a2, rk_g1, rk_g2, rk_k_k, rk_k_a, rk_r_k, rk_ln_w, rk_ln_b, ffn_w_up, ffn_conv_w, ffn_conv_b, ffn_w_down):
    return _forward(x_prompt, x_sample, cache_sb_k, cache_sb_v, cache_dsa_k, cache_dsa_v, cache_idx_k, page_table,
                    state_wkv, state_shift, state_ffn_conv,
                    norm_mix_pre, norm_mix_post, norm_ffn_pre, norm_ffn_post,
                    att_w_in, att_w_out,
                    rk_mix, rk_w_r, rk_w_k, rk_w_v, rk_w_o, rk_w0, rk_w1, rk_w2, rk_a0, rk_a1, rk_a2,
                    rk_g1, rk_g2, rk_k_k, rk_k_a, rk_r_k, rk_ln_w, rk_ln_b,
                    ffn_w_up, ffn_conv_w, ffn_conv_b, ffn_w_down)
```

```python
import functools
import math

import jax
import jax.numpy as jnp
from jax import lax
from jax.experimental import pallas as pl
from jax.experimental.pallas import tpu as pltpu

F32 = jnp.float32
BF16 = jnp.bfloat16
I32 = jnp.int32

D_MODEL = 1024
D_HEAD = 64
H_SB = 8
H_DSA = 8
KV_DSA = 2
H_IDX = 8
D_IDX = 64
DSA_TOPK = 256
PAGE = 128
ROPE_THETA = 10000.0
RK_N = 64
RK_H = D_MODEL // RK_N
RK_GN_EPS = 64e-5
D_FF = 2816
NORM_EPS = 1e-6

LANES = 128
VMEM_LIMIT = 56 * 1024 * 1024
INT_MIN = -2147483648
INT_MAX = 2147483647
NEG_BIG = -1e30

_C_QA, _C_KA, _C_VA, _C_QB, _C_KB, _C_VB, _C_QI, _C_KI, _C_WI, _C_END = (
    0, 512, 1024, 1536, 2048, 2176, 2304, 2816, 2944, 3072)


def _cparams(*sem):
    return pltpu.CompilerParams(dimension_semantics=sem, vmem_limit_bytes=VMEM_LIMIT)


def _dot(a, b):
    return jnp.dot(a, b, preferred_element_type=F32)


def _dot_nt(a, b):
    return lax.dot_general(a, b, (((1,), (1,)), ((), ())), preferred_element_type=F32)


def _dot_tn(a, b):
    return lax.dot_general(a, b, (((0,), (0,)), ((), ())), preferred_element_type=F32)


def _split(x):
    hi = x.astype(BF16)
    lo = (x - hi.astype(F32)).astype(BF16)
    return hi, lo


def _dot_x2(x, m_bf16):
    hi, lo = _split(x)
    return _dot(hi, m_bf16) + _dot(lo, m_bf16)


def _rms(x, g):
    return x * lax.rsqrt(jnp.mean(x * x, axis=-1, keepdims=True) + NORM_EPS) * g


def _seg_ones():
    r = lax.broadcasted_iota(I32, (LANES, LANES), 0)
    c = lax.broadcasted_iota(I32, (LANES, LANES), 1)
    return ((r < 64) == (c < 64)).astype(BF16)


def _seg64_sum(x, bd):
    n = x.shape[1] // LANES
    return jnp.concatenate([_dot_x2(x[:, LANES * m:LANES * (m + 1)], bd) for m in range(n)], axis=1)


def _attn_proj_kernel(x_ref, g_ref, w_ref, cos_ref, sin_ref,
                      qa_ref, ka_ref, va_ref, qb_ref, kb_ref, vb_ref, qi_ref, ki_ref, wi_ref):
    h = _rms(x_ref[...], g_ref[...]).astype(BF16)
    cos = cos_ref[...]
    sin = sin_ref[...]
    lane = lax.broadcasted_iota(I32, (1, LANES), 1)
    first = (lane & 32) == 0

    def proj(c0, c1):
        return _dot(h, w_ref[:, c0:c1])

    def rope(blk):
        rot = jnp.where(first, pltpu.roll(blk, LANES - 32, 1), pltpu.roll(blk, 32, 1))
        return blk * cos + rot * sin

    qa_ref[...] = proj(_C_QA, _C_KA)
    ka_ref[...] = proj(_C_KA, _C_VA)
    va_ref[...] = proj(_C_VA, _C_QB)
    for m in range(4):
        qb_ref[:, LANES * m:LANES * (m + 1)] = rope(proj(_C_QB + LANES * m, _C_QB + LANES * (m + 1)))
        qi_ref[:, LANES * m:LANES * (m + 1)] = rope(proj(_C_QI + LANES * m, _C_QI + LANES * (m + 1)))
    kb_ref[...] = rope(proj(_C_KB, _C_VB))
    vb_ref[...] = proj(_C_VB, _C_QI)
    ki_ref[...] = rope(proj(_C_KI, _C_WI))
    wi_ref[...] = proj(_C_WI, _C_END)


def _attn_proj(x, g, w, cos, sin, tm):
    M = x.shape[0]
    nt = cos.shape[0] // tm
    widths = (512, 512, 512, 512, 128, 128, 512, 128, 128)
    row = lambda i: (i, 0)
    return pl.pallas_call(
        _attn_proj_kernel,
        grid=(M // tm,),
        in_specs=[pl.BlockSpec((tm, D_MODEL), row),
                  pl.BlockSpec((1, D_MODEL), lambda i: (0, 0)),
                  pl.BlockSpec((D_MODEL, _C_END), lambda i: (0, 0)),
                  pl.BlockSpec((tm, LANES), lambda i: (i % nt, 0)),
                  pl.BlockSpec((tm, LANES), lambda i: (i % nt, 0))],
        out_specs=[pl.BlockSpec((tm, wd), row) for wd in widths],
        out_shape=[jax.ShapeDtypeStruct((M, wd), F32) for wd in widths],
        compiler_params=_cparams("parallel"),
        name="attn_proj",
    )(x, g, w, cos, sin)


def _rope_tables(pos):
    half = D_HEAD // 2
    inv = ROPE_THETA ** (-2.0 * jnp.arange(half, dtype=F32) / D_HEAD)
    ang = pos.astype(F32)[:, None] * inv[None, :]
    cos = jnp.cos(ang)
    sin = jnp.sin(ang)
    return jnp.tile(cos, (1, 4)), jnp.tile(jnp.concatenate([-sin, sin], axis=1), (1, 2))


def _pack_w_in(w_in):
    ki = w_in[:, 2816:2880]
    wi = w_in[:, 2880:2888]
    pad = jnp.zeros((w_in.shape[0], LANES - H_IDX), w_in.dtype)
    return jnp.concatenate([w_in[:, :2816], ki, ki, wi, pad], axis=1).astype(BF16)


LOG2E = math.log2(math.e)
ATT_SCALE = D_HEAD ** -0.5 * LOG2E


def _softplus2(z2):
    return jnp.log(1.0 + jnp.exp2(-jnp.abs(z2))) * LOG2E


def _sb_block(z, c, tri, vb, causal):
    t = _softplus2(z)
    lb = jnp.minimum(z, 0.0) - t
    lk = lb - z
    if causal is not None:
        lk = jnp.where(causal, lk, 0.0)
    aft = _dot(lk.astype(BF16), tri) + c
    w = jnp.exp2(lb + aft)
    if causal is not None:
        w = jnp.where(causal, w, 0.0)
    pv = _dot(w.astype(BF16), vb)
    return c + jnp.sum(lk, axis=1, keepdims=True), pv


SB_DEAD = -150.0


def _sb_prompt_kernel(q_ref, k_ref, v_ref, o_ref, *, tq):
    i = pl.program_id(2)
    lane = lax.broadcasted_iota(I32, (1, LANES), 1)
    row = lax.broadcasted_iota(I32, (tq, tq), 0)
    col = lax.broadcasted_iota(I32, (tq, tq), 1)
    tri = (row > col).astype(BF16)
    causal = col < row
    q = q_ref[...] * ATT_SCALE
    qh = [jnp.where(lane < 64, q, 0.0).astype(BF16), jnp.where(lane >= 64, q, 0.0).astype(BF16)]

    def step(js, carry, mask):
        kbs, vbs = [], []
        for j in js:
            off = pl.multiple_of(j * tq, tq)
            kbs.append(k_ref[pl.ds(off, tq), :].astype(BF16))
            vbs.append(v_ref[pl.ds(off, tq), :].astype(BF16))
        tiles = [(h2, n) for n in range(len(js)) for h2 in range(2)]
        z = [_dot_nt(qh[h2], kbs[n]) for h2, n in tiles]
        t = [_softplus2(x) for x in z]
        lb = [jnp.minimum(x, 0.0) - y for x, y in zip(z, t)]
        lk = [x - y for x, y in zip(lb, z)]
        if mask is not None:
            lk = [jnp.where(mask, x, 0.0) for x in lk]
        loc = [_dot(x.astype(BF16), tri) for x in lk]
        tot = [jnp.sum(x, axis=1, keepdims=True) for x in lk]
        c = [carry[0], carry[2]]
        w = []
        for idx, (h2, n) in enumerate(tiles):
            x = jnp.exp2(lb[idx] + loc[idx] + c[h2])
            w.append((jnp.where(mask, x, 0.0) if mask is not None else x).astype(BF16))
            c[h2] = c[h2] + tot[idx]
        pv = [_dot(w[idx], vbs[n]) for idx, (h2, n) in enumerate(tiles)]
        acc = [carry[1], carry[3]]
        for idx, (h2, n) in enumerate(tiles):
            acc[h2] = acc[h2] + pv[idx]
        return c[0], acc[0], c[1], acc[1]

    def alive(carry):
        return jnp.maximum(jnp.max(carry[0]), jnp.max(carry[2])) > SB_DEAD

    zc, za = jnp.zeros((tq, 1), F32), jnp.zeros((tq, LANES), F32)
    carry = step([i], (zc, za, zc, za), causal)
    carry = lax.fori_loop(0, jnp.minimum(i, 1), lambda jj, cr: step([i - 1], cr, None), carry)
    rem = jnp.maximum(i - 1, 0)

    def pair_cond(st):
        return (st[0] < rem // 2) & alive(st[1:])

    def pair_body(st):
        jj = st[0]
        return (jj + 1,) + step([i - 2 - 2 * jj, i - 3 - 2 * jj], st[1:], None)

    carry = lax.while_loop(pair_cond, pair_body, (jnp.int32(0),) + carry)[1:]
    last = jnp.where((rem % 2 == 1) & alive(carry), 1, 0)
    carry = lax.fori_loop(0, last, lambda jj, cr: step([0], cr, None), carry)
    o_ref[...] = jnp.where(lane < 64, carry[1], carry[3])


def _sb_prompt(q, k, v, B, T, tq):
    nq = T // tq
    return pl.pallas_call(
        functools.partial(_sb_prompt_kernel, tq=tq),
        grid=(B, 4, nq),
        in_specs=[pl.BlockSpec((tq, LANES), lambda b, p, i: (b * nq + i, p)),
                  pl.BlockSpec((T, LANES), lambda b, p, i: (b, p)),
                  pl.BlockSpec((T, LANES), lambda b, p, i: (b, p))],
        out_specs=pl.BlockSpec((tq, LANES), lambda b, p, i: (b * nq + i, p)),
        out_shape=jax.ShapeDtypeStruct(q.shape, F32),
        compiler_params=_cparams("parallel", "parallel", "arbitrary"),
        name="sb_prompt",
    )(q, k, v)


SROWS = 16


def _pages_per_step(n_pages):
    for g in (32, 16, 8, 4, 2):
        if n_pages % g == 0:
            return g
    return 1


def _sb_probe_kernel(pt_ref, q_ref, kn_ref, *rest, G, n_real):
    kc_refs, alive_ref = rest[:G], rest[G]
    qrow = lax.broadcasted_iota(I32, (SROWS, PAGE), 0)
    kcol = lax.broadcasted_iota(I32, (SROWS, PAGE), 1)
    causal = kcol < qrow
    log_keep = lambda z: jnp.minimum(z, 0.0) - _softplus2(z) - z
    cmax = None
    for h in range(H_SB):
        q = (q_ref[0, :, D_HEAD * h:D_HEAD * (h + 1)] * ATT_SCALE).astype(BF16)
        kb = kn_ref[0, :, D_HEAD * h:D_HEAD * (h + 1)].astype(BF16)
        kt = jnp.concatenate([kc_refs[s][h] for s in range(G)], axis=1).astype(BF16)
        c = (jnp.sum(jnp.where(causal, log_keep(_dot_nt(q, kb)), 0.0), axis=1, keepdims=True)
             + jnp.sum(log_keep(_dot(q, kt)), axis=1, keepdims=True))
        cmax = c if cmax is None else jnp.maximum(cmax, c)
    real = lax.broadcasted_iota(I32, (SROWS, 1), 0) < n_real
    worst = jnp.max(jnp.where(real, cmax, NEG_BIG))
    alive_ref[...] = jnp.full(alive_ref.shape, jnp.where(worst > SB_DEAD - 1.0, 1, 0), I32)


def _sb_sample_kernel(pt_ref, alive_ref, q_ref, kn_ref, vn_ref, *rest, G):
    kc_refs, vc_refs = rest[:G], rest[G:2 * G]
    o_ref, c_scr, acc_scr = rest[2 * G:]
    b = pl.program_id(0)
    j = pl.program_id(1)
    row = lax.broadcasted_iota(I32, (PAGE, PAGE), 0)
    col = lax.broadcasted_iota(I32, (PAGE, PAGE), 1)
    tri = (row > col).astype(BF16)
    qrow = lax.broadcasted_iota(I32, (SROWS, PAGE), 0)
    kcol = lax.broadcasted_iota(I32, (SROWS, PAGE), 1)
    causal = kcol < qrow
    scale = ATT_SCALE

    def qhead(h):
        return (q_ref[0, :, D_HEAD * h:D_HEAD * (h + 1)] * scale).astype(BF16)

    @pl.when(j == 0)
    def _():
        for h in range(H_SB):
            kb = kn_ref[0, :, D_HEAD * h:D_HEAD * (h + 1)].astype(BF16)
            vb = vn_ref[0, :, D_HEAD * h:D_HEAD * (h + 1)].astype(BF16)
            c, pv = _sb_block(_dot_nt(qhead(h), kb), jnp.zeros((SROWS, 1), F32), tri, vb, causal)
            c_scr[h] = jnp.broadcast_to(c, (SROWS, LANES))
            acc_scr[h] = pv

    @pl.when((j == 0) | (alive_ref[b] > 0))
    def _():
        order = range(G - 1, -1, -1)
        zs = []
        for h in range(H_SB):
            kt = jnp.concatenate([kc_refs[s][h] for s in order], axis=1).astype(BF16)
            z = _dot(qhead(h), kt)
            zs += [z[:, PAGE * p:PAGE * (p + 1)] for p in range(G)]
        z = jnp.concatenate(zs, axis=0)
        t = _softplus2(z)
        lb = jnp.minimum(z, 0.0) - t
        lk = lb - z
        loc = _dot(lk.astype(BF16), tri)
        tot = jnp.sum(lk, axis=1, keepdims=True)
        cs = []
        for h in range(H_SB):
            run = c_scr[h][:, 0:1]
            per_page = [None] * G
            for p in range(G - 1, -1, -1):
                per_page[p] = run
                run = run + tot[(h * G + p) * SROWS:(h * G + p + 1) * SROWS]
            c_scr[h] = jnp.broadcast_to(run, (SROWS, LANES))
            cs += per_page
        w = jnp.exp2(lb + loc + jnp.concatenate(cs, axis=0)).astype(BF16)
        for h in range(H_SB):
            wh = jnp.concatenate([w[(h * G + p) * SROWS:(h * G + p + 1) * SROWS] for p in range(G)], axis=1)
            vt = jnp.concatenate([vc_refs[s][h] for s in order], axis=1).astype(BF16)
            acc_scr[h] = acc_scr[h] + _dot_nt(wh, vt)

    @pl.when(j == pl.num_programs(1) - 1)
    def _():
        for h in range(H_SB):
            o_ref[0, :, D_HEAD * h:D_HEAD * (h + 1)] = acc_scr[h]


def _sb_sample(pt, q, kn, vn, kc, vc, n_real):
    B, NP = pt.shape
    G = _pages_per_step(NP)
    newest = lambda s: pl.BlockSpec((None, H_SB, D_HEAD, PAGE), lambda b, pt: (pt[b, NP - 1 - s], 0, 0, 0))
    alive = pl.pallas_call(
        functools.partial(_sb_probe_kernel, G=G, n_real=n_real),
        grid_spec=pltpu.PrefetchScalarGridSpec(
            num_scalar_prefetch=1,
            grid=(B,),
            in_specs=([pl.BlockSpec((1, SROWS, 512), lambda b, pt: (b, 0, 0)),
                       pl.BlockSpec((1, PAGE, 512), lambda b, pt: (b, 0, 0))]
                      + [newest(s) for s in range(G)]),
            out_specs=pl.BlockSpec((1, 8, LANES), lambda b, pt: (b, 0, 0))),
        out_shape=jax.ShapeDtypeStruct((B, 8, LANES), I32),
        compiler_params=_cparams("parallel"),
        name="sb_probe",
    )(pt, q, kn, *([kc] * G))[:, 0, 0]

    def page(s):
        def index(b, j, pt, alive):
            jj = jnp.where(alive[b] > 0, j, 0)
            return (pt[b, NP - 1 - (jj * G + s)], 0, 0, 0)
        return pl.BlockSpec((None, H_SB, D_HEAD, PAGE), index)

    per_b = lambda b, j, pt, alive: (b, 0, 0)
    grid_spec = pltpu.PrefetchScalarGridSpec(
        num_scalar_prefetch=2,
        grid=(B, NP // G),
        in_specs=([pl.BlockSpec((1, SROWS, 512), per_b),
                   pl.BlockSpec((1, PAGE, 512), per_b),
                   pl.BlockSpec((1, PAGE, 512), per_b)]
                  + [page(s) for s in range(G)] * 2),
        out_specs=pl.BlockSpec((1, SROWS, 512), per_b),
        scratch_shapes=[pltpu.VMEM((H_SB, SROWS, LANES), F32), pltpu.VMEM((H_SB, SROWS, D_HEAD), F32)])
    return pl.pallas_call(
        functools.partial(_sb_sample_kernel, G=G),
        grid_spec=grid_spec,
        out_shape=jax.ShapeDtypeStruct((B, SROWS, 512), F32),
        compiler_params=_cparams("parallel", "arbitrary"),
        name="sb_sample",
    )(pt, alive, q, kn, vn, *([kc] * G), *([vc] * G))


_IDX_SCALE = D_IDX ** -0.5 * H_IDX ** -0.5


def _score_key(score):
    score = jnp.where(score == 0.0, 0.0, score)
    bits = pltpu.bitcast(score, I32)
    return bits ^ ((bits >> 31) & INT_MAX)


def _topk_threshold(count_ge, count_eq_lt, parts, rows, real_rows, n_sel, idx_bits, jl_ref):
    zeros = tuple(jnp.zeros((rows, 1), I32) for _ in range(parts))

    def bit_body(it, tu):
        cand = tuple(t | jnp.left_shift(jnp.int32(1), 31 - it) for t in tu)
        cnt = count_ge(tuple(c ^ INT_MIN for c in cand))
        return tuple(jnp.where(n >= n_sel, c, t) for n, c, t in zip(cnt, cand, tu))

    tu = lax.fori_loop(0, 32, bit_body, zeros)
    thr = tuple(t ^ INT_MIN for t in tu)
    n_ge = count_ge(thr)
    tied = tuple((n > n_sel) & (t != INT_MIN) for n, t in zip(n_ge, thr))
    if real_rows < rows:
        real = lax.broadcasted_iota(I32, (rows, 1), 0) < real_rows
        tied = tuple(t & real for t in tied)
    width = jl_ref.shape[-1]
    for n in range(parts):
        jl_ref[n] = jnp.full((rows, width), INT_MAX, I32)
    any_tied = tied[0].astype(I32)
    for t in tied[1:]:
        any_tied = jnp.maximum(any_tied, t.astype(I32))

    @pl.when(jnp.max(any_tied) > 0)
    def _():
        n_gt = count_ge(tuple(t + 1 for t in thr))
        need = tuple(n_sel - jnp.where(t == INT_MAX, 0, n) for t, n in zip(thr, n_gt))

        def idx_body(it, lo):
            cand = tuple(x | jnp.left_shift(jnp.int32(1), idx_bits - 1 - it) for x in lo)
            cnt = count_eq_lt(thr, cand)
            return tuple(jnp.where(n < nd, c, x) for n, nd, c, x in zip(cnt, need, cand, lo))
        lo = lax.fori_loop(0, idx_bits, idx_body, zeros)
        for n in range(parts):
            jl_ref[n] = jnp.broadcast_to(jnp.where(tied[n], lo[n], INT_MAX), (rows, width))

    return thr


def _dsa_prompt_kernel(qi_ref, wi_ref, ki_ref, qb_ref, kb_ref, vb_ref, o_ref, key_scr, jl_scr, row_scr,
                       *, tq, n_sel, idx_bits, AG):
    i = pl.program_id(1)
    tk = LANES
    R = tq // tk
    cm = R * i
    cl = cm + R - 1
    lane = lax.broadcasted_iota(I32, (1, LANES), 1)
    lo_half = lane < 64
    row = lax.broadcasted_iota(I32, (tq, tk), 0)
    col = lax.broadcasted_iota(I32, (tq, tk), 1)
    diag_ok = [col + r * tk <= row for r in range(R)]

    qs = []
    for h in range(H_IDX):
        blk = qi_ref[:, LANES * (h // 2):LANES * (h // 2 + 1)]
        qs.append(jnp.where(lo_half if h % 2 == 0 else ~lo_half, blk, 0.0))
    qstack = jnp.concatenate(qs, axis=0).astype(BF16)
    wb = [jnp.broadcast_to(wi_ref[:, h:h + 1], (tq, tk)) for h in range(H_IDX)]

    def score_chunks(c0, n, masks):
        off = pl.multiple_of(c0 * tk, tk)
        kc = ki_ref[pl.ds(off, n * tk), :].astype(BF16)
        s = _dot_nt(qstack, kc)
        relu = [[jnp.maximum(s[h * tq:(h + 1) * tq, u * tk:(u + 1) * tk], 0.0) for h in range(H_IDX)]
                for u in range(n)]
        score = [wb[0] * r[0] for r in relu]
        for h in range(1, H_IDX):
            score = [sc + wb[h] * r[h] for sc, r in zip(score, relu)]
        for u in range(n):
            key = _score_key(score[u] * _IDX_SCALE)
            if masks is not None:
                key = jnp.where(masks[u], key, INT_MIN)
            key_scr[c0 + u] = key

    def score_group(g, carry):
        score_chunks(g * AG, AG, None)
        return carry

    def score_single(c, carry):
        score_chunks(c, 1, None)
        return carry

    lax.fori_loop(0, cm // AG, score_group, 0)
    lax.fori_loop((cm // AG) * AG, cm, score_single, 0)
    score_chunks(cm, R, diag_ok)
    for u in range(1, AG):
        @pl.when((cl % AG) + u < AG)
        def _():
            key_scr[cl + u] = jnp.full((tq, tk), INT_MIN, I32)

    NP = tq // LANES
    prow = [slice(LANES * n, LANES * (n + 1)) for n in range(NP)]
    colp = lax.broadcasted_iota(I32, (LANES, tk), 1)

    def count_ge(cands):
        accs = []
        for n in range(NP):
            def body(g, acc, n=n):
                for u in range(AG):
                    acc = acc + (key_scr[g * AG + u][prow[n]] >= cands[n]).astype(F32)
                return acc
            accs.append(lax.fori_loop(0, cl // AG + 1, body, jnp.zeros((LANES, tk), F32)))
        return tuple(jnp.sum(a, axis=1, keepdims=True).astype(I32) for a in accs)

    def count_eq_lt(thrs, jcands):
        accs = []
        for n in range(NP):
            def body(g, acc, n=n):
                for u in range(AG):
                    c = g * AG + u
                    hit = (key_scr[c][prow[n]] == thrs[n]) & (colp + c * tk < jcands[n])
                    acc = acc + hit.astype(F32)
                return acc
            accs.append(lax.fori_loop(0, cl // AG + 1, body, jnp.zeros((LANES, tk), F32)))
        return tuple(jnp.sum(a, axis=1, keepdims=True).astype(I32) for a in accs)

    thr = _topk_threshold(count_ge, count_eq_lt, NP, LANES, LANES, n_sel, idx_bits, jl_scr)
    jl = [jl_scr[n][:, 0:1] for n in range(NP)]

    qs = []
    for h in range(H_DSA):
        blk = qb_ref[:, LANES * (h // 2):LANES * (h // 2 + 1)] * ATT_SCALE
        g = h // (H_DSA // KV_DSA)
        if (h % 2) != g:
            blk = pltpu.roll(blk, 64, 1)
        qs.append(jnp.where(lo_half if g == 0 else ~lo_half, blk, 0.0))
    qstack2 = jnp.concatenate(qs, axis=0).astype(BF16)

    ta = AG * tk
    rowa = lax.broadcasted_iota(I32, (tq, ta), 0)
    cola = lax.broadcasted_iota(I32, (tq, ta), 1)
    for n in range(NP):
        row_scr[0, prow[n], :] = jnp.broadcast_to(thr[n], (LANES, LANES))
        row_scr[1, prow[n], :] = jnp.broadcast_to(jl[n], (LANES, LANES))
    thr_a = jnp.concatenate([row_scr[0]] * AG, axis=1)
    jl_a = jnp.concatenate([row_scr[1]] * AG, axis=1)

    def att_group(c, carry, last):
        m, l, acc = carry
        off = pl.multiple_of(c * ta, ta)
        kblk = jnp.concatenate([key_scr[c * AG + u] for u in range(AG)], axis=1)
        sel = (kblk > thr_a) | ((kblk == thr_a) & (cola + off <= jl_a))
        if last:
            sel = sel & (cola + off <= rowa + i * tq)
        z = _dot_nt(qstack2, kb_ref[pl.ds(off, ta), :].astype(BF16))
        z = jnp.where(sel[None], z.reshape(H_DSA, tq, ta), NEG_BIG).reshape(H_DSA * tq, ta)
        m_new = jnp.maximum(m, jnp.max(z, axis=1, keepdims=True))
        alpha = jnp.exp2(m - m_new)
        p = jnp.exp2(z - m_new)
        l = alpha * l + jnp.sum(p, axis=1, keepdims=True)
        acc = alpha * acc + _dot(p.astype(BF16), vb_ref[pl.ds(off, ta), :].astype(BF16))
        return m_new, l, acc

    carry = (jnp.full((H_DSA * tq, 1), NEG_BIG, F32), jnp.zeros((H_DSA * tq, 1), F32),
             jnp.zeros((H_DSA * tq, LANES), F32))
    full_groups = (i * tq) // ta
    carry = lax.fori_loop(0, full_groups, lambda c, cr: att_group(c, cr, False), carry)
    for r in range(max(1, tq // ta)):
        carry = att_group(full_groups + r, carry, True)
    m, l, acc = carry
    out = acc / l
    for mblk in range(4):
        parts = []
        for h in (2 * mblk, 2 * mblk + 1):
            o_h = out[h * tq:(h + 1) * tq]
            if (h % 2) != h // (H_DSA // KV_DSA):
                o_h = pltpu.roll(o_h, 64, 1)
            parts.append(o_h)
        o_ref[:, LANES * mblk:LANES * (mblk + 1)] = jnp.where(lo_half, parts[0], parts[1])


def _dsa_prompt(qi, wi, ki2, qb, kb, vb, B, T, tq, n_sel):
    nq = T // tq
    qrow = lambda b, i: (b * nq + i, 0)
    full = lambda b, i: (b, 0)
    AG = 4 if (T // LANES) % 4 == 0 else 1
    return pl.pallas_call(
        functools.partial(_dsa_prompt_kernel, tq=tq, n_sel=n_sel, idx_bits=max(1, (T - 1).bit_length()), AG=AG),
        grid=(B, nq),
        in_specs=[pl.BlockSpec((tq, 512), qrow), pl.BlockSpec((tq, LANES), qrow),
                  pl.BlockSpec((T, LANES), full), pl.BlockSpec((tq, 512), qrow),
                  pl.BlockSpec((T, LANES), full), pl.BlockSpec((T, LANES), full)],
        out_specs=pl.BlockSpec((tq, 512), qrow),
        out_shape=jax.ShapeDtypeStruct(qb.shape, F32),
        scratch_shapes=[pltpu.VMEM((T // LANES, tq, LANES), I32), pltpu.VMEM((tq // LANES, LANES, LANES), I32),
                        pltpu.VMEM((2, tq, LANES), I32)],
        compiler_params=_cparams("parallel", "arbitrary"),
        name="dsa_prompt",
    )(qi, wi, ki2, qb, kb, vb)


def _dsa_score_sample_kernel(pt_ref, q_ref, w_ref, kn_ref, *rest, G, n_pages, n_real, n_sel, idx_bits):
    kc_refs = rest[:G]
    key_ref, thr_ref, jl_ref = rest[G:]
    j = pl.program_id(1)
    qrow = lax.broadcasted_iota(I32, (SROWS, PAGE), 0)
    kcol = lax.broadcasted_iota(I32, (SROWS, PAGE), 1)
    q = q_ref[0].astype(BF16)
    w = w_ref[0]

    def score_keys(s):
        s = jnp.maximum(s, 0.0) * w
        sc = s[0:SROWS]
        for h in range(1, H_IDX):
            sc = sc + s[h * SROWS:(h + 1) * SROWS]
        return _score_key(sc * _IDX_SCALE)

    @pl.when(j < n_pages // G)
    def _():
        kt = jnp.concatenate([kc_refs[s][...] for s in range(G)], axis=1).astype(BF16)
        key = score_keys(_dot(q, kt))
        for s in range(G):
            key_ref[0, j * G + s] = key[:, PAGE * s:PAGE * (s + 1)]

    @pl.when(j == n_pages // G)
    def _():
        key = score_keys(_dot_nt(q, kn_ref[0].astype(BF16)))
        key_ref[0, n_pages] = jnp.where(kcol <= qrow, key, INT_MIN)
        pos = lax.broadcasted_iota(I32, (G, SROWS, PAGE), 0) * PAGE + kcol[None]

        def count_ge(cands):
            cand, = cands

            def body(g, acc):
                blk = key_ref[0, pl.ds(g * G, G)]
                return acc + jnp.sum((blk >= cand[None]).astype(F32), axis=0)
            acc = lax.fori_loop(0, n_pages // G, body, jnp.zeros((SROWS, PAGE), F32))
            acc = acc + (key_ref[0, n_pages] >= cand).astype(F32)
            return (jnp.sum(acc, axis=1, keepdims=True).astype(I32),)

        def count_eq_lt(thrs, jcands):
            (thr,), (jcand,) = thrs, jcands

            def body(g, acc):
                blk = key_ref[0, pl.ds(g * G, G)]
                hit = (blk == thr[None]) & (pos + g * (G * PAGE) < jcand[None])
                return acc + jnp.sum(hit.astype(F32), axis=0)
            acc = lax.fori_loop(0, n_pages // G, body, jnp.zeros((SROWS, PAGE), F32))
            last = (key_ref[0, n_pages] == thr) & (kcol + n_pages * PAGE < jcand)
            return (jnp.sum(acc + last.astype(F32), axis=1, keepdims=True).astype(I32),)

        thr_ref[0], = _topk_threshold(count_ge, count_eq_lt, 1, SROWS, n_real, n_sel, idx_bits, jl_ref)


def _dsa_score_sample(pt, q, w, kn, kc, n_real, n_sel):
    B, NP = pt.shape
    G = _pages_per_step(NP)
    L = (NP + 1) * PAGE
    page = lambda s: pl.BlockSpec((None, D_IDX, PAGE),
                                  lambda b, j, pt: (pt[b, jnp.minimum(j * G + s, NP - 1)], 0, 0))
    grid_spec = pltpu.PrefetchScalarGridSpec(
        num_scalar_prefetch=1,
        grid=(B, NP // G + 1),
        in_specs=([pl.BlockSpec((1, H_IDX * SROWS, D_IDX), lambda b, j, pt: (b, 0, 0)),
                   pl.BlockSpec((1, H_IDX * SROWS, 1), lambda b, j, pt: (b, 0, 0)),
                   pl.BlockSpec((1, PAGE, D_IDX), lambda b, j, pt: (b, 0, 0))]
                  + [page(s) for s in range(G)]),
        out_specs=[pl.BlockSpec((1, NP + 1, SROWS, PAGE), lambda b, j, pt: (b, 0, 0, 0)),
                   pl.BlockSpec((1, SROWS, 1), lambda b, j, pt: (b, 0, 0)),
                   pl.BlockSpec((1, SROWS, 1), lambda b, j, pt: (b, 0, 0))])
    return pl.pallas_call(
        functools.partial(_dsa_score_sample_kernel, G=G, n_pages=NP, n_real=n_real, n_sel=n_sel,
                          idx_bits=max(1, (L - 1).bit_length())),
        grid_spec=grid_spec,
        out_shape=[jax.ShapeDtypeStruct((B, NP + 1, SROWS, PAGE), I32),
                   jax.ShapeDtypeStruct((B, SROWS, 1), I32),
                   jax.ShapeDtypeStruct((B, SROWS, 1), I32)],
        compiler_params=_cparams("parallel", "arbitrary"),
        name="dsa_score_sample",
    )(pt, q, w, kn, *([kc] * G))


_GR = (H_DSA // KV_DSA) * SROWS


def _dsa_att_sample_kernel(pt_ref, q_ref, key_ref, thr_ref, jl_ref, kn_ref, vn_ref, *rest, G, n_pages):
    kc_refs, vc_refs = rest[:G], rest[G:2 * G]
    o_ref, m_scr, l_scr, acc_scr = rest[2 * G:]
    j = pl.program_id(1)
    qrow = lax.broadcasted_iota(I32, (SROWS, PAGE), 0)
    kcol = lax.broadcasted_iota(I32, (SROWS, PAGE), 1)
    nrep = H_DSA // KV_DSA

    @pl.when(j == 0)
    def _():
        m_scr[...] = jnp.full(m_scr.shape, NEG_BIG, F32)
        l_scr[...] = jnp.zeros(l_scr.shape, F32)
        acc_scr[...] = jnp.zeros(acc_scr.shape, F32)

    def attend(zs, vs, sel, v_transposed=True):
        groups = range(KV_DSA)
        n = zs[0].shape[1]
        zs = [jnp.where(sel[None], z.reshape(nrep, SROWS, n), NEG_BIG).reshape(_GR, n) for z in zs]
        m = [m_scr[g][:, 0:1] for g in groups]
        l = [l_scr[g][:, 0:1] for g in groups]
        m_new = [jnp.maximum(m[g], jnp.max(zs[g], axis=1, keepdims=True)) for g in groups]
        alpha = [jnp.exp2(m[g] - m_new[g]) for g in groups]
        p = [jnp.exp2(zs[g] - m_new[g]) for g in groups]
        l = [alpha[g] * l[g] + jnp.sum(p[g], axis=1, keepdims=True) for g in groups]
        pb = [x.astype(BF16) for x in p]
        vb = [x.astype(BF16) for x in vs]
        pv = [_dot_nt(pb[g], vb[g]) if v_transposed else _dot(pb[g], vb[g]) for g in groups]
        for g in groups:
            acc_scr[g] = alpha[g] * acc_scr[g] + pv[g]
            m_scr[g] = jnp.broadcast_to(m_new[g], (_GR, LANES))
            l_scr[g] = jnp.broadcast_to(l[g], (_GR, LANES))

    def qgroup(g):
        return (q_ref[0, g] * ATT_SCALE).astype(BF16)

    thr = thr_ref[0]
    jl = jl_ref[0]

    @pl.when(j < n_pages // G)
    def _():
        sels = []
        for s in range(G):
            kblk = key_ref[0, j * G + s]
            sels.append((kblk > thr) | ((kblk == thr) & (kcol + (j * G + s) * PAGE <= jl)))
        sel = jnp.concatenate(sels, axis=1)
        kts = [jnp.concatenate([kc_refs[s][g] for s in range(G)], axis=1).astype(BF16) for g in range(KV_DSA)]
        vts = [jnp.concatenate([vc_refs[s][g] for s in range(G)], axis=1) for g in range(KV_DSA)]
        attend([_dot(qgroup(g), kts[g]) for g in range(KV_DSA)], vts, sel)

    @pl.when(j == n_pages // G)
    def _():
        off = n_pages * PAGE
        kblk = key_ref[0, n_pages]
        sel = ((kblk > thr) | ((kblk == thr) & (kcol + off <= jl))) & (kcol <= qrow)
        kns = [kn_ref[0, :, D_HEAD * g:D_HEAD * (g + 1)].astype(BF16) for g in range(KV_DSA)]
        vns = [vn_ref[0, :, D_HEAD * g:D_HEAD * (g + 1)] for g in range(KV_DSA)]
        attend([_dot_nt(qgroup(g), kns[g]) for g in range(KV_DSA)], vns, sel, v_transposed=False)
        for g in range(KV_DSA):
            o_ref[0, g] = acc_scr[g] / l_scr[g][:, 0:1]


def _dsa_att_sample(pt, q, keys, thr, jl, kn, vn, kc, vc):
    B, NP = pt.shape
    G = _pages_per_step(NP)
    page = lambda s: pl.BlockSpec((None, KV_DSA, D_HEAD, PAGE),
                                  lambda b, j, pt: (pt[b, jnp.minimum(j * G + s, NP - 1)], 0, 0, 0))
    per_b3 = lambda b, j, pt: (b, 0, 0)
    grid_spec = pltpu.PrefetchScalarGridSpec(
        num_scalar_prefetch=1,
        grid=(B, NP // G + 1),
        in_specs=([pl.BlockSpec((1, KV_DSA, _GR, D_HEAD), lambda b, j, pt: (b, 0, 0, 0)),
                   pl.BlockSpec((1, NP + 1, SROWS, PAGE), lambda b, j, pt: (b, 0, 0, 0)),
                   pl.BlockSpec((1, SROWS, 1), per_b3),
                   pl.BlockSpec((1, SROWS, 1), per_b3),
                   pl.BlockSpec((1, PAGE, LANES), per_b3),
                   pl.BlockSpec((1, PAGE, LANES), per_b3)]
                  + [page(s) for s in range(G)] * 2),
        out_specs=pl.BlockSpec((1, KV_DSA, _GR, D_HEAD), lambda b, j, pt: (b, 0, 0, 0)),
        scratch_shapes=[pltpu.VMEM((KV_DSA, _GR, LANES), F32), pltpu.VMEM((KV_DSA, _GR, LANES), F32),
                        pltpu.VMEM((KV_DSA, _GR, D_HEAD), F32)])
    return pl.pallas_call(
        functools.partial(_dsa_att_sample_kernel, G=G, n_pages=NP),
        grid_spec=grid_spec,
        out_shape=jax.ShapeDtypeStruct((B, KV_DSA, _GR, D_HEAD), F32),
        compiler_params=_cparams("parallel", "arbitrary"),
        name="dsa_att_sample",
    )(pt, q, keys, thr, jl, kn, vn, *([kc] * G), *([vc] * G))


def _mm_norm_res_kernel(*refs, n):
    a_refs, w_refs = refs[:n], refs[n:2 * n]
    g_ref, x_ref, o_ref = refs[2 * n:]
    acc = _dot(a_refs[0][...].astype(BF16), w_refs[0][...])
    for a_ref, w_ref in zip(a_refs[1:], w_refs[1:]):
        acc = acc + _dot(a_ref[...].astype(BF16), w_ref[...])
    o_ref[...] = x_ref[...] + _rms(acc, g_ref[...])


def _mm_norm_res(a_list, w_list, g, x, tm, name):
    M = x.shape[0]
    n = len(a_list)
    row = lambda i: (i, 0)
    const = lambda i: (0, 0)
    return pl.pallas_call(
        functools.partial(_mm_norm_res_kernel, n=n),
        grid=(M // tm,),
        in_specs=([pl.BlockSpec((tm, a.shape[1]), row) for a in a_list]
                  + [pl.BlockSpec(w.shape, const) for w in w_list]
                  + [pl.BlockSpec((1, D_MODEL), const), pl.BlockSpec((tm, D_MODEL), row)]),
        out_specs=pl.BlockSpec((tm, D_MODEL), row),
        out_shape=jax.ShapeDtypeStruct((M, D_MODEL), F32),
        compiler_params=_cparams("parallel"),
        name=name,
    )(*a_list, *w_list, g, x)


_GELU_C = math.sqrt(2.0 / math.pi)
FFN_SUB = 256


def _gelu_tanh(x):
    return x * (0.5 * (1.0 + jnp.tanh(_GELU_C * (x + 0.044715 * (x * x * x)))))


def _ffn_up_kernel(x_ref, g_ref, wg_ref, wv_ref, cwg_ref, cwv_ref, cbg_ref, cbv_ref, pg_ref, pv_ref,
                   act_ref, og_ref, ov_ref, eg_scr, ev_scr, *, tm, T, streamed):
    i = pl.program_id(1)
    h = _rms(x_ref[...], g_ref[...]).astype(BF16)
    trow = lax.broadcasted_iota(I32, (tm, 1), 0) % T

    tn = act_ref.shape[1]
    if streamed:
        @pl.when((i * tm) % T == 0)
        def _():
            eg_scr[6:8, :] = pg_ref[...]
            ev_scr[6:8, :] = pv_ref[...]
    else:
        eg_scr[0:8, :] = jnp.zeros((8, tn), F32)
        ev_scr[0:8, :] = jnp.zeros((8, tn), F32)

    rid = lax.broadcasted_iota(I32, (8, 1), 0)

    def conv(u, cs, cw_ref, cb_ref, p_ref, o_ref, e_scr):
        if streamed:
            o_ref[:, cs] = u[tm - 2:tm, :]
            prev2, prev1 = e_scr[6:7, cs], e_scr[7:8, cs]
            r1 = pltpu.roll(u, 1, 0)
            r2 = pltpu.roll(u, 2, 0)
            head1 = jnp.where(rid == 0, prev1, r1[0:8])
            head2 = jnp.where(rid == 0, prev2, jnp.where(rid == 1, prev1, r2[0:8]))
            u1 = jnp.concatenate([head1, r1[8:]], axis=0)
            u2 = jnp.concatenate([head2, r2[8:]], axis=0)
            e_scr[6:8, cs] = u[tm - 2:tm, :]
        else:
            e_scr[8:8 + tm, cs] = u
            o_ref[:, cs] = u
            u1 = jnp.where(trow >= 1, e_scr[7:7 + tm, cs], p_ref[0, :, cs])
            u2 = jnp.where(trow >= 2, e_scr[6:6 + tm, cs], p_ref[1, :, cs])
        return cb_ref[:, cs] + cw_ref[0:1, cs] * u2 + cw_ref[1:2, cs] * u1 + cw_ref[2:3, cs] * u

    subs = [slice(c0, min(c0 + FFN_SUB, tn)) for c0 in range(0, tn, FFN_SUB)]
    dots = lambda cs: (_dot(h, wg_ref[:, cs]), _dot(h, wv_ref[:, cs]))
    cur = dots(subs[0])
    for n, cs in enumerate(subs):
        nxt = dots(subs[n + 1]) if n + 1 < len(subs) else None
        gate = conv(cur[0], cs, cwg_ref, cbg_ref, pg_ref, og_ref, eg_scr)
        val = conv(cur[1], cs, cwv_ref, cbv_ref, pv_ref, ov_ref, ev_scr)
        act_ref[:, cs] = (_gelu_tanh(gate) * val).astype(BF16)
        cur = nxt


def _ffn_up(x, g, w_up, conv_w, conv_b, prev, T, tm, tn):
    M = x.shape[0]
    nj = D_FF // tn
    streamed = T >= tm
    tpb = max(T // tm, 1)
    if streamed:
        p_spec = lambda off: pl.BlockSpec((None, 2, tn), lambda j, i: (i // tpb, 0, j + off))
        o_shape = jax.ShapeDtypeStruct((M // T, 2, D_FF), F32)
        o_spec = pl.BlockSpec((None, 2, tn), lambda j, i: (i // tpb, 0, j))
    else:
        p_spec = lambda off: pl.BlockSpec((2, tm, tn), lambda j, i: (0, i, j + off))
        o_shape = jax.ShapeDtypeStruct((M, D_FF), F32)
        o_spec = pl.BlockSpec((tm, tn), lambda j, i: (i, j))
    col = lambda rows, off: pl.BlockSpec((rows, tn), lambda j, i: (0, j + off))
    return pl.pallas_call(
        functools.partial(_ffn_up_kernel, tm=tm, T=T, streamed=streamed),
        grid=(nj, M // tm),
        in_specs=[pl.BlockSpec((tm, D_MODEL), lambda j, i: (i, 0)),
                  pl.BlockSpec((1, D_MODEL), lambda j, i: (0, 0)),
                  col(D_MODEL, 0), col(D_MODEL, nj), col(3, 0), col(3, nj), col(1, 0), col(1, nj),
                  p_spec(0), p_spec(nj)],
        out_specs=[pl.BlockSpec((tm, tn), lambda j, i: (i, j)), o_spec, o_spec],
        out_shape=[jax.ShapeDtypeStruct((M, D_FF), BF16), o_shape, o_shape],
        scratch_shapes=[pltpu.VMEM((tm + 8, tn), F32), pltpu.VMEM((tm + 8, tn), F32)],
        compiler_params=_cparams("parallel", "arbitrary"),
        name="ffn_up",
    )(x, g, w_up, w_up, conv_w, conv_w, conv_b, conv_b, prev, prev)


def _rk_pre_kernel(x_ref, g_ref, sp_ref, mix_ref, wr_ref, wk_ref, wv_ref, w1_ref, w2_ref, a1_ref, a2_ref,
                   g1_ref, g2_ref, vec_ref,
                   hn_ref, r_ref, wl_ref, kf_ref, v_ref, av_ref, bv_ref, gg_ref, e_scr, *, tm, T, streamed):
    i = pl.program_id(0)
    h = _rms(x_ref[...], g_ref[...])
    hn_ref[...] = h
    e_scr[8:8 + tm, :] = h
    if streamed:
        @pl.when((i * tm) % T == 0)
        def _():
            e_scr[7:8, :] = sp_ref[...]
        xp = e_scr[7:7 + tm, :]
        e_scr[7:8, :] = h[tm - 1:tm, :]
    else:
        e_scr[0:8, :] = jnp.zeros((8, D_MODEL), F32)
        trow = lax.broadcasted_iota(I32, (tm, 1), 0) % T
        xp = jnp.where(trow >= 1, e_scr[7:7 + tm, :], sp_ref[...])
    xx = xp - h
    mixed = lambda j: (h + xx * mix_ref[j:j + 1, :]).astype(BF16)
    w0, a0, k_k, k_a = vec_ref[0:1, :], vec_ref[1:2, :], vec_ref[2:3, :], vec_ref[3:4, :]
    r = _dot(mixed(0), wr_ref[...])
    k = _dot(mixed(2), wk_ref[...])
    v = _dot(mixed(3), wv_ref[...])
    lw = w0 + _dot(jnp.tanh(_dot(mixed(1), w1_ref[...])).astype(BF16), w2_ref[...])
    w_log = -(jnp.maximum(-lw, 0.0) + jnp.log(1.0 + jnp.exp(-jnp.abs(lw)))) - 0.5
    wl_ref[...] = -jnp.exp(w_log)
    a = jax.nn.sigmoid(a0 + _dot(_dot(mixed(4), a1_ref[...]).astype(BF16), a2_ref[...]))
    gg_ref[...] = _dot(jax.nn.sigmoid(_dot(mixed(5), g1_ref[...])).astype(BF16), g2_ref[...])
    kk = k * k_k
    bd = _seg_ones()
    kk = kk / jnp.maximum(jnp.sqrt(_seg64_sum(kk * kk, bd)), 1e-12)
    r_ref[...] = r
    v_ref[...] = v
    kf_ref[...] = k * (1.0 + (a - 1.0) * k_a)
    av_ref[...] = -kk
    bv_ref[...] = kk * a


def _rk_pre(x, g, sp, mix, wr, wk, wv, w1, w2, a1, a2, g1, g2, vecs, T, tm):
    M = x.shape[0]
    streamed = T >= tm
    tpb = max(T // tm, 1)
    row = lambda i: (i, 0)
    const = lambda i: (0, 0)
    sp_spec = (pl.BlockSpec((None, 1, D_MODEL), lambda i: (i // tpb, 0, 0)) if streamed
               else pl.BlockSpec((tm, D_MODEL), row))
    full = lambda a: pl.BlockSpec(a.shape, const)
    return pl.pallas_call(
        functools.partial(_rk_pre_kernel, tm=tm, T=T, streamed=streamed),
        grid=(M // tm,),
        in_specs=[pl.BlockSpec((tm, D_MODEL), row), pl.BlockSpec((1, D_MODEL), const), sp_spec,
                  full(mix), full(wr), full(wk), full(wv), full(w1), full(w2), full(a1), full(a2),
                  full(g1), full(g2), full(vecs)],
        out_specs=[pl.BlockSpec((tm, D_MODEL), row)] * 8,
        out_shape=[jax.ShapeDtypeStruct((M, D_MODEL), F32)] * 8,
        scratch_shapes=[pltpu.VMEM((tm + 8, D_MODEL), F32)],
        compiler_params=_cparams("arbitrary"),
        name="rk_pre",
    )(x, g, sp, mix, wr, wk, wv, w1, w2, a1, a2, g1, g2, vecs)


def _bdot(a, b):
    return _dot(a.astype(BF16), b.astype(BF16))


def _rk_scan_kernel(r_ref, wl_ref, k_ref, v_ref, a_ref, b_ref, s0_ref, y_ref, sf_ref, s_scr, *, C, npair):
    c = pl.program_id(2)

    @pl.when(c == 0)
    def _():
        s_scr[...] = s0_ref[0]

    lane = lax.broadcasted_iota(I32, (1, LANES), 1)
    m0 = lane < 64
    rr = lax.broadcasted_iota(I32, (C, C), 0)
    cc = lax.broadcasted_iota(I32, (C, C), 1)
    tri_incl = (cc <= rr).astype(BF16)
    r2 = lax.broadcasted_iota(I32, (2 * C, 2 * C), 0)
    c2 = lax.broadcasted_iota(I32, (2 * C, 2 * C), 1)
    strict = (r2 % C) > (c2 % C)
    incl = (r2 % C) >= (c2 % C)
    incl2 = jnp.concatenate([incl, incl], axis=1)
    eye = (r2 == c2).astype(F32)
    n_double = max(int(math.log2(C)) - 1, 0)

    def stack2(z):
        return jnp.concatenate([jnp.where(m0, z, 0.0), jnp.where(m0, 0.0, z)], axis=0)

    pairs = range(npair)
    sls = [slice(LANES * p, LANES * (p + 1)) for p in pairs]
    wl = [wl_ref[:, sl] for sl in sls]
    ld = []
    for p in pairs:
        wl_hi, wl_lo = _split(wl[p])
        ld.append(_dot(tri_incl, wl_hi) + _dot(tri_incl, wl_lo))
    dfull = [jnp.exp(x) for x in ld]
    dinv = [jnp.exp(-x) for x in ld]
    As = [stack2(a_ref[:, sls[p]] * jnp.exp(ld[p] - wl[p])).astype(BF16) for p in pairs]
    Bs = [stack2(b_ref[:, sls[p]] * dinv[p]) for p in pairs]
    Ks = [stack2(k_ref[:, sls[p]] * dinv[p]) for p in pairs]
    Rs = [stack2(r_ref[:, sls[p]] * dfull[p]).astype(BF16) for p in pairs]
    Vs = [stack2(v_ref[:, sls[p]]) for p in pairs]
    Vb = [x.astype(BF16) for x in Vs]
    BK = [jnp.concatenate([Bs[p], Ks[p]], axis=0).astype(BF16) for p in pairs]
    AR = [jnp.concatenate([As[p], Rs[p]], axis=0) for p in pairs]
    GG = [_dot_nt(AR[p], BK[p]) for p in pairs]
    Lab = [jnp.where(strict, g[:2 * C, :2 * C], 0.0) for g in GG]
    Lak = [jnp.where(strict, g[:2 * C, 2 * C:], 0.0).astype(BF16) for g in GG]
    Mr = [jnp.where(incl2, g[2 * C:], 0.0).astype(BF16) for g in GG]
    Tm = [eye + x for x in Lab]
    if n_double > 0:
        Pb = [x.astype(BF16) for x in Lab]
        Pb = [_dot(x, x).astype(BF16) for x in Pb]
        for rnd in range(n_double):
            Tb = [x.astype(BF16) for x in Tm]
            if rnd + 1 < n_double:
                X = [_dot(Pb[p], jnp.concatenate([Pb[p], Tb[p]], axis=1)) for p in pairs]
                Pb = [x[:, :2 * C].astype(BF16) for x in X]
                Tm = [Tm[p] + X[p][:, 2 * C:] for p in pairs]
            else:
                Tm = [Tm[p] + _dot(Pb[p], Tb[p]) for p in pairs]
    S = [s_scr[p] for p in pairs]
    Sb = [x.astype(BF16) for x in S]
    ARS = [_dot_nt(AR[p], Sb[p]) for p in pairs]
    rhs = [ARS[p][:2 * C] + _dot(Lak[p], Vb[p]) for p in pairs]
    U = [_bdot(Tm[p], rhs[p]) for p in pairs]
    UV = [jnp.concatenate([U[p], Vs[p]], axis=0) for p in pairs]
    UVb = [x.astype(BF16) for x in UV]
    Ys = [ARS[p][2 * C:] + _dot(Mr[p], UVb[p]) for p in pairs]
    upd = [_dot(UV[p].T.astype(BF16), BK[p]) for p in pairs]
    for p in pairs:
        y_ref[:, sls[p]] = Ys[p][:C] + Ys[p][C:]
        s_scr[p] = (S[p] + upd[p]) * dfull[p][C - 1:C, :]

    @pl.when(c == pl.num_programs(2) - 1)
    def _():
        sf_ref[0] = s_scr[...]


def _rk_scan(r, wl, k, v, a, b, s0, B, T, C, npair):
    nc = T // C
    ng = 8 // npair
    blk = pl.BlockSpec((C, LANES * npair), lambda bb, g, c: (bb * nc + c, g))
    s_spec = pl.BlockSpec((1, npair, LANES, LANES), lambda bb, g, c: (bb, g, 0, 0))
    return pl.pallas_call(
        functools.partial(_rk_scan_kernel, C=C, npair=npair),
        grid=(B, ng, nc),
        in_specs=[blk] * 6 + [s_spec],
        out_specs=[blk, s_spec],
        out_shape=[jax.ShapeDtypeStruct(r.shape, F32), jax.ShapeDtypeStruct(s0.shape, F32)],
        scratch_shapes=[pltpu.VMEM((npair, LANES, LANES), F32)],
        compiler_params=_cparams("parallel", "parallel", "arbitrary"),
        name="rk_scan",
    )(r, wl, k, v, a, b, s0)


def _rk_post_kernel(y_ref, r_ref, kf_ref, v_ref, gg_ref, vec_ref, o_ref):
    bd = _seg_ones()
    ln_w, ln_b, r_k = vec_ref[0:1, :], vec_ref[1:2, :], vec_ref[2:3, :]
    y = y_ref[...]
    mu = _seg64_sum(y, bd) * (1.0 / RK_N)
    d = y - mu
    var = _seg64_sum(d * d, bd) * (1.0 / RK_N)
    yn = d * lax.rsqrt(var + RK_GN_EPS) * ln_w + ln_b
    bonus = _seg64_sum(r_ref[...] * kf_ref[...] * r_k, bd) * v_ref[...]
    o_ref[...] = ((yn + bonus) * gg_ref[...]).astype(BF16)


def _rk_post(y, r, kf, v, gg, vecs, tm):
    M = y.shape[0]
    row = lambda i: (i, 0)
    return pl.pallas_call(
        _rk_post_kernel,
        grid=(M // tm,),
        in_specs=[pl.BlockSpec((tm, D_MODEL), row)] * 5 + [pl.BlockSpec(vecs.shape, lambda i: (0, 0))],
        out_specs=pl.BlockSpec((tm, D_MODEL), row),
        out_shape=jax.ShapeDtypeStruct((M, D_MODEL), BF16),
        compiler_params=_cparams("parallel"),
        name="rk_post",
    )(y, r, kf, v, gg, vecs)


def _row_tile(M):
    return min(256, M)


def _pad_rows(a, n):
    return jnp.pad(a, ((0, 0), (0, n - a.shape[1])) + ((0, 0),) * (a.ndim - 2))


def _attn_layer_prompt(x, B, T, g_pre, w_in, w_outs, g_post):
    M = B * T
    tm = _row_tile(M)
    cos, sin = _rope_tables(jnp.arange(T, dtype=I32))
    qa, ka, va, qb, kb, vb, qi, ki2, wi = _attn_proj(x, g_pre, w_in, cos, sin, tm)
    n_sel = min(DSA_TOPK, T // 4)
    oa = _sb_prompt(qa, ka, va, B, T, min(256, T))
    ob = _dsa_prompt(qi, wi, ki2, qb, kb, vb, B, T, min(256, T), n_sel)
    x = _mm_norm_res([oa, ob], w_outs, g_post, x, tm, "attn_out")
    rows = (ka.reshape(1, B, T, H_SB, D_HEAD), va.reshape(1, B, T, H_SB, D_HEAD),
            kb.reshape(1, B, T, KV_DSA, D_HEAD), vb.reshape(1, B, T, KV_DSA, D_HEAD),
            ki2[:, :D_IDX].reshape(1, B, T, D_IDX))
    return x, rows


def _attn_layer_sample(x, B, T, P, page_table, caches, g_pre, w_in, w_outs, g_post):
    M = B * T
    cos, sin = _rope_tables(P + jnp.arange(T, dtype=I32))
    cos, sin = jnp.tile(cos, (B, 1)), jnp.tile(sin, (B, 1))
    qa, ka, va, qb, kb, vb, qi, ki2, wi = _attn_proj(x, g_pre, w_in, cos, sin, M)
    c_sb_k, c_sb_v, c_dsa_k, c_dsa_v, c_idx = caches
    n_pool = c_sb_k.shape[0]
    n_sel = min(DSA_TOPK, (P + T) // 4)
    b3 = lambda a: a.reshape(B, T, a.shape[-1])

    row_minor = lambda c: jnp.moveaxis(c, 1, -1)
    oa = _sb_sample(page_table, _pad_rows(b3(qa), SROWS), _pad_rows(b3(ka), PAGE), _pad_rows(b3(va), PAGE),
                    row_minor(c_sb_k), row_minor(c_sb_v), T)
    oa = oa[:, :T].reshape(M, H_SB * D_HEAD)

    qi_s = _pad_rows(b3(qi).reshape(B, T, H_IDX, D_IDX), SROWS).transpose(0, 2, 1, 3).reshape(B, H_IDX * SROWS, D_IDX)
    wi_s = _pad_rows(b3(wi)[:, :, :H_IDX], SROWS).transpose(0, 2, 1).reshape(B, H_IDX * SROWS, 1)
    keys, thr, jl = _dsa_score_sample(page_table, qi_s, wi_s, _pad_rows(b3(ki2)[:, :, :D_IDX], PAGE),
                                      row_minor(c_idx), T, n_sel)
    qb_s = _pad_rows(b3(qb).reshape(B, T, H_DSA, D_HEAD), SROWS).transpose(0, 2, 1, 3).reshape(B, KV_DSA, _GR, D_HEAD)
    ob = _dsa_att_sample(page_table, qb_s, keys, thr, jl, _pad_rows(b3(kb), PAGE), _pad_rows(b3(vb), PAGE),
                         row_minor(c_dsa_k), row_minor(c_dsa_v))
    ob = ob.reshape(B, H_DSA, SROWS, D_HEAD)[:, :, :T].transpose(0, 2, 1, 3).reshape(M, H_DSA * D_HEAD)

    x = _mm_norm_res([oa, ob], w_outs, g_post, x, M, "attn_out")
    rows = (ka.reshape(1, B, T, H_SB, D_HEAD), va.reshape(1, B, T, H_SB, D_HEAD),
            kb.reshape(1, B, T, KV_DSA, D_HEAD), vb.reshape(1, B, T, KV_DSA, D_HEAD),
            ki2[:, :D_IDX].reshape(1, B, T, D_IDX))
    return x, rows


def _pair_state(S):
    B = S.shape[0]
    S = S.reshape(B, 8, 2, RK_N, RK_N)
    z = jnp.zeros_like(S[:, :, 0])
    top = jnp.concatenate([S[:, :, 0], z], axis=-1)
    bot = jnp.concatenate([z, S[:, :, 1]], axis=-1)
    return jnp.concatenate([top, bot], axis=-2)


def _unpair_state(Sp):
    B = Sp.shape[0]
    return jnp.stack([Sp[:, :, :RK_N, :RK_N], Sp[:, :, RK_N:, RK_N:]], axis=2).reshape(B, RK_H, RK_N, RK_N)


RK_CHUNK = 64
RK_PAIRS = 8


def _rwkv_layer(x, B, T, shift_prev, S0, g_pre, prm, g_post):
    M = B * T
    tm = _row_tile(M)
    (mix, wr, wk, wv, wo, w1, w2, a1, a2, g1, g2, vec_pre, vec_post) = prm
    if T >= tm:
        sp = shift_prev.reshape(B, 1, D_MODEL)
    else:
        sp = jnp.repeat(shift_prev, T, axis=0)
    hn, r, wl, kf, v, av, bv, gg = _rk_pre(x, g_pre, sp, mix, wr, wk, wv, w1, w2, a1, a2, g1, g2, vec_pre, T, tm)
    Tp = -(-T // RK_CHUNK) * RK_CHUNK
    if Tp != T:
        padt = lambda a: _pad_rows(a.reshape(B, T, D_MODEL), Tp).reshape(B * Tp, D_MODEL)
        y, Sf = _rk_scan(padt(r), padt(wl), padt(kf), padt(v), padt(av), padt(bv), _pair_state(S0), B, Tp, RK_CHUNK, RK_PAIRS)
        y = y.reshape(B, Tp, D_MODEL)[:, :T].reshape(M, D_MODEL)
    else:
        y, Sf = _rk_scan(r, wl, kf, v, av, bv, _pair_state(S0), B, T, RK_CHUNK, RK_PAIRS)
    z = _rk_post(y, r, kf, v, gg, vec_post, tm)
    x = _mm_norm_res([z], [wo], g_post, x, tm, "rk_out")
    shift = hn.reshape(B, T, D_MODEL)[:, -1]
    return x, shift, _unpair_state(Sf)


def _ffn_layer(x, B, T, prev, g_pre, w_up, conv_w, conv_b, w_down, g_post):
    M = B * T
    tm = _row_tile(M)
    tn = D_FF // 2
    if T >= tm:
        act, cg, cv = _ffn_up(x, g_pre, w_up, conv_w, conv_b, prev, T, tm, tn)
        conv_state = jnp.concatenate([cg, cv], axis=-1)
    else:
        zeros = jnp.zeros((B, T, 2 * D_FF), F32)
        p1 = zeros.at[:, 0].set(prev[:, 1])
        p2 = zeros.at[:, 0].set(prev[:, 0]).at[:, 1].set(prev[:, 1])
        pp = jnp.stack([p1.reshape(M, -1), p2.reshape(M, -1)])
        act, ug, uv = _ffn_up(x, g_pre, w_up, conv_w, conv_b, pp, T, tm, tn)
        u = jnp.concatenate([ug, uv], axis=-1).reshape(B, T, 2 * D_FF)
        conv_state = jnp.concatenate([prev, u], axis=1)[:, -2:]
    x = _mm_norm_res([act], [w_down], g_post, x, tm, "ffn_down")
    return x, conv_state


def _forward(x_prompt, x_sample, cache_sb_k, cache_sb_v, cache_dsa_k, cache_dsa_v, cache_idx_k, page_table,
             state_wkv, state_shift, state_ffn_conv,
             norm_mix_pre, norm_mix_post, norm_ffn_pre, norm_ffn_post,
             att_w_in, att_w_out,
             rk_mix, rk_w_r, rk_w_k, rk_w_v, rk_w_o, rk_w0, rk_w1, rk_w2, rk_a0, rk_a1, rk_a2,
             rk_g1, rk_g2, rk_k_k, rk_k_a, rk_r_k, rk_ln_w, rk_ln_b,
             ffn_w_up, ffn_conv_w, ffn_conv_b, ffn_w_down):
    Bp, Tp, D = x_prompt.shape
    Bs, Ts, _ = x_sample.shape
    P = page_table.shape[1] * PAGE
    depth = norm_mix_pre.shape[0]
    xp = x_prompt.reshape(Bp * Tp, D)
    xs = x_sample.reshape(Bs * Ts, D)
    vrow = lambda a: a.reshape(1, -1)
    bf = lambda a: a.astype(BF16)
    att_p, att_s, wkv_p, wkv_s, sh_p, sh_s, cv_p, cv_s = [], [], [], [], [], [], [], []
    for i in range(depth):
        li = i // 2
        g_pre, g_post = vrow(norm_mix_pre[i]), vrow(norm_mix_post[i])
        if i % 2 == 0:
            w_in = _pack_w_in(att_w_in[li])
            w_outs = [bf(att_w_out[li][:H_SB * D_HEAD]), bf(att_w_out[li][H_SB * D_HEAD:])]
            caches = (cache_sb_k[li], cache_sb_v[li], cache_dsa_k[li], cache_dsa_v[li], cache_idx_k[li])
            xp, rows_p = _attn_layer_prompt(xp, Bp, Tp, g_pre, w_in, w_outs, g_post)
            xs, rows_s = _attn_layer_sample(xs, Bs, Ts, P, page_table, caches, g_pre, w_in, w_outs, g_post)
            att_p.append(rows_p)
            att_s.append(rows_s)
        else:
            vec_pre = jnp.stack([rk_w0[li], rk_a0[li], rk_k_k[li], rk_k_a[li]])
            vec_post = jnp.stack([rk_ln_w[li], rk_ln_b[li], rk_r_k[li].reshape(-1)])
            prm = (rk_mix[li], bf(rk_w_r[li]), bf(rk_w_k[li]), bf(rk_w_v[li]), bf(rk_w_o[li]),
                   bf(rk_w1[li]), bf(rk_w2[li]), bf(rk_a1[li]), bf(rk_a2[li]), bf(rk_g1[li]), bf(rk_g2[li]),
                   vec_pre, vec_post)
            xp, shp, Sp = _rwkv_layer(xp, Bp, Tp, jnp.zeros((Bp, D), F32), jnp.zeros((Bp, RK_H, RK_N, RK_N), F32),
                                      g_pre, prm, g_post)
            xs, shs, Ss = _rwkv_layer(xs, Bs, Ts, state_shift[li], state_wkv[li], g_pre, prm, g_post)
            wkv_p.append(Sp)
            wkv_s.append(Ss)
            sh_p.append(shp)
            sh_s.append(shs)
        f_pre, f_post = vrow(norm_ffn_pre[i]), vrow(norm_ffn_post[i])
        w_up, w_down = bf(ffn_w_up[i]), bf(ffn_w_down[i])
        cb = vrow(ffn_conv_b[i])
        xp, cp = _ffn_layer(xp, Bp, Tp, jnp.zeros((Bp, 2, 2 * D_FF), F32), f_pre, w_up, ffn_conv_w[i], cb, w_down, f_post)
        xs, cs = _ffn_layer(xs, Bs, Ts, state_ffn_conv[i], f_pre, w_up, ffn_conv_w[i], cb, w_down, f_post)
        cv_p.append(cp)
        cv_s.append(cs)
    cat = lambda rows, j: jnp.concatenate([r[j] for r in rows], axis=0)
    return (xp.reshape(Bp, Tp, D), xs.reshape(Bs, Ts, D),
            cat(att_p, 0), cat(att_s, 0), cat(att_p, 1), cat(att_s, 1),
            cat(att_p, 2), cat(att_s, 2), cat(att_p, 3), cat(att_s, 3),
            cat(att_p, 4), cat(att_s, 4),
            jnp.stack(wkv_p), jnp.stack(wkv_s), jnp.stack(sh_p), jnp.stack(sh_s),
            jnp.stack(cv_p), jnp.stack(cv_s))


def kernel(x_prompt, x_sample, cache_sb_k, cache_sb_v, cache_dsa_k, cache_dsa_v, cache_idx_k, page_table, state_wkv, state_shift, state_ffn_conv, norm_mix_pre, norm_mix_post, norm_ffn_pre, norm_ffn_post, att_w_in, att_w_out, rk_mix, rk_w_r, rk_w_k, rk_w_v, rk_w_o, rk_w0, rk_w1, rk_w2, rk_a0, rk_a1, rk_a2, rk_g1, rk_g2, rk_k_k, rk_k_a, rk_r_k, rk_ln_w, rk_ln_b, ffn_w_up, ffn_conv_w, ffn_conv_b, ffn_w_down):
    return _forward(x_prompt, x_sample, cache_sb_k, cache_sb_v, cache_dsa_k, cache_dsa_v, cache_idx_k, page_table,
                    state_wkv, state_shift, state_ffn_conv,
                    norm_mix_pre, norm_mix_post, norm_ffn_pre, norm_ffn_post,
                    att_w_in, att_w_out,
                    rk_mix, rk_w_r, rk_w_k, rk_w_v, rk_w_o, rk_w0, rk_w1, rk_w2, rk_a0, rk_a1, rk_a2,
                    rk_g1, rk_g2, rk_k_k, rk_k_a, rk_r_k, rk_ln_w, rk_ln_b,
                    ffn_w_up, ffn_conv_w, ffn_conv_b, ffn_w_down)
```

```python
import functools
import math

import jax
import jax.numpy as jnp
from jax import lax
from jax.experimental import pallas as pl
from jax.experimental.pallas import tpu as pltpu

F32 = jnp.float32
BF16 = jnp.bfloat16
I32 = jnp.int32

D_MODEL = 1024
D_HEAD = 64
H_SB = 8
H_DSA = 8
KV_DSA = 2
H_IDX = 8
D_IDX = 64
DSA_TOPK = 256
PAGE = 128
ROPE_THETA = 10000.0
RK_N = 64
RK_H = D_MODEL // RK_N
RK_GN_EPS = 64e-5
D_FF = 2816
NORM_EPS = 1e-6

LANES = 128
VMEM_LIMIT = 56 * 1024 * 1024
INT_MIN = -2147483648
INT_MAX = 2147483647
NEG_BIG = -1e30

_C_QA, _C_KA, _C_VA, _C_QB, _C_KB, _C_VB, _C_QI, _C_KI, _C_WI, _C_END = (
    0, 512, 1024, 1536, 2048, 2176, 2304, 2816, 2944, 3072)


def _cparams(*sem):
    return pltpu.CompilerParams(dimension_semantics=sem, vmem_limit_bytes=VMEM_LIMIT)


def _dot(a, b):
    return jnp.dot(a, b, preferred_element_type=F32)


def _dot_nt(a, b):
    return lax.dot_general(a, b, (((1,), (1,)), ((), ())), preferred_element_type=F32)


def _dot_tn(a, b):
    return lax.dot_general(a, b, (((0,), (0,)), ((), ())), preferred_element_type=F32)


def _split(x):
    hi = x.astype(BF16)
    lo = (x - hi.astype(F32)).astype(BF16)
    return hi, lo


def _dot_x2(x, m_bf16):
    hi, lo = _split(x)
    return _dot(hi, m_bf16) + _dot(lo, m_bf16)


def _rms(x, g):
    return x * lax.rsqrt(jnp.mean(x * x, axis=-1, keepdims=True) + NORM_EPS) * g


def _seg_ones():
    r = lax.broadcasted_iota(I32, (LANES, LANES), 0)
    c = lax.broadcasted_iota(I32, (LANES, LANES), 1)
    return ((r < 64) == (c < 64)).astype(BF16)


def _seg64_sum(x, bd):
    n = x.shape[1] // LANES
    return jnp.concatenate([_dot_x2(x[:, LANES * m:LANES * (m + 1)], bd) for m in range(n)], axis=1)


def _attn_proj_kernel(x_ref, g_ref, w_ref, cos_ref, sin_ref,
                      qa_ref, ka_ref, va_ref, qb_ref, kb_ref, vb_ref, qi_ref, ki_ref, wi_ref):
    h = _rms(x_ref[...], g_ref[...]).astype(BF16)
    cos = cos_ref[...]
    sin = sin_ref[...]
    lane = lax.broadcasted_iota(I32, (1, LANES), 1)
    first = (lane & 32) == 0

    def proj(c0, c1):
        return _dot(h, w_ref[:, c0:c1])

    def rope(blk):
        rot = jnp.where(first, pltpu.roll(blk, LANES - 32, 1), pltpu.roll(blk, 32, 1))
        return blk * cos + rot * sin

    qa_ref[...] = proj(_C_QA, _C_KA)
    ka_ref[...] = proj(_C_KA, _C_VA)
    va_ref[...] = proj(_C_VA, _C_QB)
    for m in range(4):
        qb_ref[:, LANES * m:LANES * (m + 1)] = rope(proj(_C_QB + LANES * m, _C_QB + LANES * (m + 1)))
        qi_ref[:, LANES * m:LANES * (m + 1)] = rope(proj(_C_QI + LANES * m, _C_QI + LANES * (m + 1)))
    kb_ref[...] = rope(proj(_C_KB, _C_VB))
    vb_ref[...] = proj(_C_VB, _C_QI)
    ki_ref[...] = rope(proj(_C_KI, _C_WI))
    wi_ref[...] = proj(_C_WI, _C_END)


def _attn_proj(x, g, w, cos, sin, tm):
    M = x.shape[0]
    nt = cos.shape[0] // tm
    widths = (512, 512, 512, 512, 128, 128, 512, 128, 128)
    row = lambda i: (i, 0)
    return pl.pallas_call(
        _attn_proj_kernel,
        grid=(M // tm,),
        in_specs=[pl.BlockSpec((tm, D_MODEL), row),
                  pl.BlockSpec((1, D_MODEL), lambda i: (0, 0)),
                  pl.BlockSpec((D_MODEL, _C_END), lambda i: (0, 0)),
                  pl.BlockSpec((tm, LANES), lambda i: (i % nt, 0)),
                  pl.BlockSpec((tm, LANES), lambda i: (i % nt, 0))],
        out_specs=[pl.BlockSpec((tm, wd), row) for wd in widths],
        out_shape=[jax.ShapeDtypeStruct((M, wd), F32) for wd in widths],
        compiler_params=_cparams("parallel"),
        name="attn_proj",
    )(x, g, w, cos, sin)


def _rope_tables(pos):
    half = D_HEAD // 2
    inv = ROPE_THETA ** (-2.0 * jnp.arange(half, dtype=F32) / D_HEAD)
    ang = pos.astype(F32)[:, None] * inv[None, :]
    cos = jnp.cos(ang)
    sin = jnp.sin(ang)
    return jnp.tile(cos, (1, 4)), jnp.tile(jnp.concatenate([-sin, sin], axis=1), (1, 2))


def _pack_w_in(w_in):
    ki = w_in[:, 2816:2880]
    wi = w_in[:, 2880:2888]
    pad = jnp.zeros((w_in.shape[0], LANES - H_IDX), w_in.dtype)
    return jnp.concatenate([w_in[:, :2816], ki, ki, wi, pad], axis=1).astype(BF16)


LOG2E = math.log2(math.e)
ATT_SCALE = D_HEAD ** -0.5 * LOG2E


def _softplus2(z2):
    return jnp.log(1.0 + jnp.exp2(-jnp.abs(z2))) * LOG2E


def _sb_block(z, c, tri, vb, causal):
    t = _softplus2(z)
    lb = jnp.minimum(z, 0.0) - t
    lk = lb - z
    if causal is not None:
        lk = jnp.where(causal, lk, 0.0)
    aft = _dot(lk.astype(BF16), tri) + c
    w = jnp.exp2(lb + aft)
    if causal is not None:
        w = jnp.where(causal, w, 0.0)
    pv = _dot(w.astype(BF16), vb)
    return c + jnp.sum(lk, axis=1, keepdims=True), pv


SB_DEAD = -150.0


def _sb_prompt_kernel(q_ref, k_ref, v_ref, o_ref, *, tq):
    i = pl.program_id(2)
    lane = lax.broadcasted_iota(I32, (1, LANES), 1)
    row = lax.broadcasted_iota(I32, (tq, tq), 0)
    col = lax.broadcasted_iota(I32, (tq, tq), 1)
    tri = (row > col).astype(BF16)
    causal = col < row
    q = q_ref[...] * ATT_SCALE
    qh = [jnp.where(lane < 64, q, 0.0).astype(BF16), jnp.where(lane >= 64, q, 0.0).astype(BF16)]

    def step(js, carry, mask):
        kbs, vbs = [], []
        for j in js:
            off = pl.multiple_of(j * tq, tq)
            kbs.append(k_ref[pl.ds(off, tq), :].astype(BF16))
            vbs.append(v_ref[pl.ds(off, tq), :].astype(BF16))
        tiles = [(h2, n) for n in range(len(js)) for h2 in range(2)]
        z = [_dot_nt(qh[h2], kbs[n]) for h2, n in tiles]
        t = [_softplus2(x) for x in z]
        lb = [jnp.minimum(x, 0.0) - y for x, y in zip(z, t)]
        lk = [x - y for x, y in zip(lb, z)]
        if mask is not None:
            lk = [jnp.where(mask, x, 0.0) for x in lk]
        loc = [_dot(x.astype(BF16), tri) for x in lk]
        tot = [jnp.sum(x, axis=1, keepdims=True) for x in lk]
        c = [carry[0], carry[2]]
        w = []
        for idx, (h2, n) in enumerate(tiles):
            x = jnp.exp2(lb[idx] + loc[idx] + c[h2])
            w.append((jnp.where(mask, x, 0.0) if mask is not None else x).astype(BF16))
            c[h2] = c[h2] + tot[idx]
        pv = [_dot(w[idx], vbs[n]) for idx, (h2, n) in enumerate(tiles)]
        acc = [carry[1], carry[3]]
        for idx, (h2, n) in enumerate(tiles):
            acc[h2] = acc[h2] + pv[idx]
        return c[0], acc[0], c[1], acc[1]

    def alive(carry):
        return jnp.maximum(jnp.max(carry[0]), jnp.max(carry[2])) > SB_DEAD

    zc, za = jnp.zeros((tq, 1), F32), jnp.zeros((tq, LANES), F32)
    carry = step([i], (zc, za, zc, za), causal)
    carry = lax.fori_loop(0, jnp.minimum(i, 1), lambda jj, cr: step([i - 1], cr, None), carry)
    rem = jnp.maximum(i - 1, 0)

    def pair_cond(st):
        return (st[0] < rem // 2) & alive(st[1:])

    def pair_body(st):
        jj = st[0]
        return (jj + 1,) + step([i - 2 - 2 * jj, i - 3 - 2 * jj], st[1:], None)

    carry = lax.while_loop(pair_cond, pair_body, (jnp.int32(0),) + carry)[1:]
    last = jnp.where((rem % 2 == 1) & alive(carry), 1, 0)
    carry = lax.fori_loop(0, last, lambda jj, cr: step([0], cr, None), carry)
    o_ref[...] = jnp.where(lane < 64, carry[1], carry[3])


def _sb_prompt(q, k, v, B, T, tq):
    nq = T // tq
    return pl.pallas_call(
        functools.partial(_sb_prompt_kernel, tq=tq),
        grid=(B, 4, nq),
        in_specs=[pl.BlockSpec((tq, LANES), lambda b, p, i: (b * nq + i, p)),
                  pl.BlockSpec((T, LANES), lambda b, p, i: (b, p)),
                  pl.BlockSpec((T, LANES), lambda b, p, i: (b, p))],
        out_specs=pl.BlockSpec((tq, LANES), lambda b, p, i: (b * nq + i, p)),
        out_shape=jax.ShapeDtypeStruct(q.shape, F32),
        compiler_params=_cparams("parallel", "parallel", "arbitrary"),
        name="sb_prompt",
    )(q, k, v)


SROWS = 16


def _pages_per_step(n_pages):
    for g in (32, 16, 8, 4, 2):
        if n_pages % g == 0:
            return g
    return 1


def _sb_probe_kernel(pt_ref, q_ref, kn_ref, *rest, G, n_real):
    kc_refs, alive_ref = rest[:G], rest[G]
    qrow = lax.broadcasted_iota(I32, (SROWS, PAGE), 0)
    kcol = lax.broadcasted_iota(I32, (SROWS, PAGE), 1)
    causal = kcol < qrow
    log_keep = lambda z: jnp.minimum(z, 0.0) - _softplus2(z) - z
    cmax = None
    for h in range(H_SB):
        q = (q_ref[0, :, D_HEAD * h:D_HEAD * (h + 1)] * ATT_SCALE).astype(BF16)
        kb = kn_ref[0, :, D_HEAD * h:D_HEAD * (h + 1)].astype(BF16)
        kt = jnp.concatenate([kc_refs[s][h] for s in range(G)], axis=1).astype(BF16)
        c = (jnp.sum(jnp.where(causal, log_keep(_dot_nt(q, kb)), 0.0), axis=1, keepdims=True)
             + jnp.sum(log_keep(_dot(q, kt)), axis=1, keepdims=True))
        cmax = c if cmax is None else jnp.maximum(cmax, c)
    real = lax.broadcasted_iota(I32, (SROWS, 1), 0) < n_real
    worst = jnp.max(jnp.where(real, cmax, NEG_BIG))
    alive_ref[...] = jnp.full(alive_ref.shape, jnp.where(worst > SB_DEAD - 1.0, 1, 0), I32)


def _sb_sample_kernel(pt_ref, alive_ref, q_ref, kn_ref, vn_ref, *rest, G):
    kc_refs, vc_refs = rest[:G], rest[G:2 * G]
    o_ref, c_scr, acc_scr = rest[2 * G:]
    b = pl.program_id(0)
    j = pl.program_id(1)
    row = lax.broadcasted_iota(I32, (PAGE, PAGE), 0)
    col = lax.broadcasted_iota(I32, (PAGE, PAGE), 1)
    tri = (row > col).astype(BF16)
    qrow = lax.broadcasted_iota(I32, (SROWS, PAGE), 0)
    kcol = lax.broadcasted_iota(I32, (SROWS, PAGE), 1)
    causal = kcol < qrow
    scale = ATT_SCALE

    def qhead(h):
        return (q_ref[0, :, D_HEAD * h:D_HEAD * (h + 1)] * scale).astype(BF16)

    @pl.when(j == 0)
    def _():
        for h in range(H_SB):
            kb = kn_ref[0, :, D_HEAD * h:D_HEAD * (h + 1)].astype(BF16)
            vb = vn_ref[0, :, D_HEAD * h:D_HEAD * (h + 1)].astype(BF16)
            c, pv = _sb_block(_dot_nt(qhead(h), kb), jnp.zeros((SROWS, 1), F32), tri, vb, causal)
            c_scr[h] = jnp.broadcast_to(c, (SROWS, LANES))
            acc_scr[h] = pv

    @pl.when((j == 0) | (alive_ref[b] > 0))
    def _():
        order = range(G - 1, -1, -1)
        zs = []
        for h in range(H_SB):
            kt = jnp.concatenate([kc_refs[s][h] for s in order], axis=1).astype(BF16)
            z = _dot(qhead(h), kt)
            zs += [z[:, PAGE * p:PAGE * (p + 1)] for p in range(G)]
        z = jnp.concatenate(zs, axis=0)
        t = _softplus2(z)
        lb = jnp.minimum(z, 0.0) - t
        lk = lb - z
        loc = _dot(lk.astype(BF16), tri)
        tot = jnp.sum(lk, axis=1, keepdims=True)
        cs = []
        for h in range(H_SB):
            run = c_scr[h][:, 0:1]
            per_page = [None] * G
            for p in range(G - 1, -1, -1):
                per_page[p] = run
                run = run + tot[(h * G + p) * SROWS:(h * G + p + 1) * SROWS]
            c_scr[h] = jnp.broadcast_to(run, (SROWS, LANES))
            cs += per_page
        w = jnp.exp2(lb + loc + jnp.concatenate(cs, axis=0)).astype(BF16)
        for h in range(H_SB):
            wh = jnp.concatenate([w[(h * G + p) * SROWS:(h * G + p + 1) * SROWS] for p in range(G)], axis=1)
            vt = jnp.concatenate([vc_refs[s][h] for s in order], axis=1).astype(BF16)
            acc_scr[h] = acc_scr[h] + _dot_nt(wh, vt)

    @pl.when(j == pl.num_programs(1) - 1)
    def _():
        for h in range(H_SB):
            o_ref[0, :, D_HEAD * h:D_HEAD * (h + 1)] = acc_scr[h]


def _sb_sample(pt, q, kn, vn, kc, vc, n_real):
    B, NP = pt.shape
    G = _pages_per_step(NP)
    newest = lambda s: pl.BlockSpec((None, H_SB, D_HEAD, PAGE), lambda b, pt: (pt[b, NP - 1 - s], 0, 0, 0))
    alive = pl.pallas_call(
        functools.partial(_sb_probe_kernel, G=G, n_real=n_real),
        grid_spec=pltpu.PrefetchScalarGridSpec(
            num_scalar_prefetch=1,
            grid=(B,),
            in_specs=([pl.BlockSpec((1, SROWS, 512), lambda b, pt: (b, 0, 0)),
                       pl.BlockSpec((1, PAGE, 512), lambda b, pt: (b, 0, 0))]
                      + [newest(s) for s in range(G)]),
            out_specs=pl.BlockSpec((1, 8, LANES), lambda b, pt: (b, 0, 0))),
        out_shape=jax.ShapeDtypeStruct((B, 8, LANES), I32),
        compiler_params=_cparams("parallel"),
        name="sb_probe",
    )(pt, q, kn, *([kc] * G))[:, 0, 0]

    def page(s):
        def index(b, j, pt, alive):
            jj = jnp.where(alive[b] > 0, j, 0)
            return (pt[b, NP - 1 - (jj * G + s)], 0, 0, 0)
        return pl.BlockSpec((None, H_SB, D_HEAD, PAGE), index)

    per_b = lambda b, j, pt, alive: (b, 0, 0)
    grid_spec = pltpu.PrefetchScalarGridSpec(
        num_scalar_prefetch=2,
        grid=(B, NP // G),
        in_specs=([pl.BlockSpec((1, SROWS, 512), per_b),
                   pl.BlockSpec((1, PAGE, 512), per_b),
                   pl.BlockSpec((1, PAGE, 512), per_b)]
                  + [page(s) for s in range(G)] * 2),
        out_specs=pl.BlockSpec((1, SROWS, 512), per_b),
        scratch_shapes=[pltpu.VMEM((H_SB, SROWS, LANES), F32), pltpu.VMEM((H_SB, SROWS, D_HEAD), F32)])
    return pl.pallas_call(
        functools.partial(_sb_sample_kernel, G=G),
        grid_spec=grid_spec,
        out_shape=jax.ShapeDtypeStruct((B, SROWS, 512), F32),
        compiler_params=_cparams("parallel", "arbitrary"),
        name="sb_sample",
    )(pt, alive, q, kn, vn, *([kc] * G), *([vc] * G))


_IDX_SCALE = D_IDX ** -0.5 * H_IDX ** -0.5


def _score_key(score):
    score = jnp.where(score == 0.0, 0.0, score)
    bits = pltpu.bitcast(score, I32)
    return bits ^ ((bits >> 31) & INT_MAX)


def _topk_threshold(count_ge, count_eq_lt, parts, rows, real_rows, n_sel, idx_bits, jl_ref):
    zeros = tuple(jnp.zeros((rows, 1), I32) for _ in range(parts))

    def bit_body(it, tu):
        cand = tuple(t | jnp.left_shift(jnp.int32(1), 31 - it) for t in tu)
        cnt = count_ge(tuple(c ^ INT_MIN for c in cand))
        return tuple(jnp.where(n >= n_sel, c, t) for n, c, t in zip(cnt, cand, tu))

    tu = lax.fori_loop(0, 32, bit_body, zeros)
    thr = tuple(t ^ INT_MIN for t in tu)
    n_ge = count_ge(thr)
    tied = tuple((n > n_sel) & (t != INT_MIN) for n, t in zip(n_ge, thr))
    if real_rows < rows:
        real = lax.broadcasted_iota(I32, (rows, 1), 0) < real_rows
        tied = tuple(t & real for t in tied)
    width = jl_ref.shape[-1]
    for n in range(parts):
        jl_ref[n] = jnp.full((rows, width), INT_MAX, I32)
    any_tied = tied[0].astype(I32)
    for t in tied[1:]:
        any_tied = jnp.maximum(any_tied, t.astype(I32))

    @pl.when(jnp.max(any_tied) > 0)
    def _():
        n_gt = count_ge(tuple(t + 1 for t in thr))
        need = tuple(n_sel - jnp.where(t == INT_MAX, 0, n) for t, n in zip(thr, n_gt))

        def idx_body(it, lo):
            cand = tuple(x | jnp.left_shift(jnp.int32(1), idx_bits - 1 - it) for x in lo)
            cnt = count_eq_lt(thr, cand)
            return tuple(jnp.where(n < nd, c, x) for n, nd, c, x in zip(cnt, need, cand, lo))
        lo = lax.fori_loop(0, idx_bits, idx_body, zeros)
        for n in range(parts):
            jl_ref[n] = jnp.broadcast_to(jnp.where(tied[n], lo[n], INT_MAX), (rows, width))

    return thr


def _dsa_prompt_kernel(qi_ref, wi_ref, ki_ref, qb_ref, kb_ref, vb_ref, o_ref, key_scr, jl_scr, row_scr,
                       *, tq, n_sel, idx_bits, AG):
    i = pl.program_id(1)
    tk = LANES
    R = tq // tk
    cm = R * i
    cl = cm + R - 1
    lane = lax.broadcasted_iota(I32, (1, LANES), 1)
    lo_half = lane < 64
    row = lax.broadcasted_iota(I32, (tq, tk), 0)
    col = lax.broadcasted_iota(I32, (tq, tk), 1)
    diag_ok = [col + r * tk <= row for r in range(R)]

    qs = []
    for h in range(H_IDX):
        blk = qi_ref[:, LANES * (h // 2):LANES * (h // 2 + 1)]
        qs.append(jnp.where(lo_half if h % 2 == 0 else ~lo_half, blk, 0.0))
    qstack = jnp.concatenate(qs, axis=0).astype(BF16)
    wb = [jnp.broadcast_to(wi_ref[:, h:h + 1], (tq, tk)) for h in range(H_IDX)]

    def score_chunks(c0, n, masks):
        off = pl.multiple_of(c0 * tk, tk)
        kc = ki_ref[pl.ds(off, n * tk), :].astype(BF16)
        s = _dot_nt(qstack, kc)
        relu = [[jnp.maximum(s[h * tq:(h + 1) * tq, u * tk:(u + 1) * tk], 0.0) for h in range(H_IDX)]
                for u in range(n)]
        score = [wb[0] * r[0] for r in relu]
        for h in range(1, H_IDX):
            score = [sc + wb[h] * r[h] for sc, r in zip(score, relu)]
        for u in range(n):
            key = _score_key(score[u] * _IDX_SCALE)
            if masks is not None:
                key = jnp.where(masks[u], key, INT_MIN)
            key_scr[c0 + u] = key

    def score_group(g, carry):
        score_chunks(g * AG, AG, None)
        return carry

    def score_single(c, carry):
        score_chunks(c, 1, None)
        return carry

    lax.fori_loop(0, cm // AG, score_group, 0)
    lax.fori_loop((cm // AG) * AG, cm, score_single, 0)
    score_chunks(cm, R, diag_ok)
    for u in range(1, AG):
        @pl.when((cl % AG) + u < AG)
        def _():
            key_scr[cl + u] = jnp.full((tq, tk), INT_MIN, I32)

    NP = tq // LANES
    prow = [slice(LANES * n, LANES * (n + 1)) for n in range(NP)]
    colp = lax.broadcasted_iota(I32, (LANES, tk), 1)

    def count_ge(cands):
        accs = []
        for n in range(NP):
            def body(g, acc, n=n):
                for u in range(AG):
                    acc = acc + (key_scr[g * AG + u][prow[n]] >= cands[n]).astype(F32)
                return acc
            accs.append(lax.fori_loop(0, cl // AG + 1, body, jnp.zeros((LANES, tk), F32)))
        return tuple(jnp.sum(a, axis=1, keepdims=True).astype(I32) for a in accs)

    def count_eq_lt(thrs, jcands):
        accs = []
        for n in range(NP):
            def body(g, acc, n=n):
                for u in range(AG):
                    c = g * AG + u
                    hit = (key_scr[c][prow[n]] == thrs[n]) & (colp + c * tk < jcands[n])
                    acc = acc + hit.astype(F32)
                return acc
            accs.append(lax.fori_loop(0, cl // AG + 1, body, jnp.zeros((LANES, tk), F32)))
        return tuple(jnp.sum(a, axis=1, keepdims=True).astype(I32) for a in accs)

    thr = _topk_threshold(count_ge, count_eq_lt, NP, LANES, LANES, n_sel, idx_bits, jl_scr)
    jl = [jl_scr[n][:, 0:1] for n in range(NP)]

    qs = []
    for h in range(H_DSA):
        blk = qb_ref[:, LANES * (h // 2):LANES * (h // 2 + 1)] * ATT_SCALE
        g = h // (H_DSA // KV_DSA)
        if (h % 2) != g:
            blk = pltpu.roll(blk, 64, 1)
        qs.append(jnp.where(lo_half if g == 0 else ~lo_half, blk, 0.0))
    qstack2 = jnp.concatenate(qs, axis=0).astype(BF16)

    ta = AG * tk
    rowa = lax.broadcasted_iota(I32, (tq, ta), 0)
    cola = lax.broadcasted_iota(I32, (tq, ta), 1)
    for n in range(NP):
        row_scr[0, prow[n], :] = jnp.broadcast_to(thr[n], (LANES, LANES))
        row_scr[1, prow[n], :] = jnp.broadcast_to(jl[n], (LANES, LANES))
    thr_a = jnp.concatenate([row_scr[0]] * AG, axis=1)
    jl_a = jnp.concatenate([row_scr[1]] * AG, axis=1)

    def att_group(c, carry, last):
        m, l, acc = carry
        off = pl.multiple_of(c * ta, ta)
        kblk = jnp.concatenate([key_scr[c * AG + u] for u in range(AG)], axis=1)
        sel = (kblk > thr_a) | ((kblk == thr_a) & (cola + off <= jl_a))
        if last:
            sel = sel & (cola + off <= rowa + i * tq)
        z = _dot_nt(qstack2, kb_ref[pl.ds(off, ta), :].astype(BF16))
        z = jnp.where(sel[None], z.reshape(H_DSA, tq, ta), NEG_BIG).reshape(H_DSA * tq, ta)
        m_new = jnp.maximum(m, jnp.max(z, axis=1, keepdims=True))
        alpha = jnp.exp2(m - m_new)
        p = jnp.exp2(z - m_new)
        l = alpha * l + jnp.sum(p, axis=1, keepdims=True)
        acc = alpha * acc + _dot(p.astype(BF16), vb_ref[pl.ds(off, ta), :].astype(BF16))
        return m_new, l, acc

    carry = (jnp.full((H_DSA * tq, 1), NEG_BIG, F32), jnp.zeros((H_DSA * tq, 1), F32),
             jnp.zeros((H_DSA * tq, LANES), F32))
    full_groups = (i * tq) // ta
    carry = lax.fori_loop(0, full_groups, lambda c, cr: att_group(c, cr, False), carry)
    for r in range(max(1, tq // ta)):
        carry = att_group(full_groups + r, carry, True)
    m, l, acc = carry
    out = acc / l
    for mblk in range(4):
        parts = []
        for h in (2 * mblk, 2 * mblk + 1):
            o_h = out[h * tq:(h + 1) * tq]
            if (h % 2) != h // (H_DSA // KV_DSA):
                o_h = pltpu.roll(o_h, 64, 1)
            parts.append(o_h)
        o_ref[:, LANES * mblk:LANES * (mblk + 1)] = jnp.where(lo_half, parts[0], parts[1])


def _dsa_prompt(qi, wi, ki2, qb, kb, vb, B, T, tq, n_sel):
    nq = T // tq
    qrow = lambda b, i: (b * nq + i, 0)
    full = lambda b, i: (b, 0)
    AG = 4 if (T // LANES) % 4 == 0 else 1
    return pl.pallas_call(
        functools.partial(_dsa_prompt_kernel, tq=tq, n_sel=n_sel, idx_bits=max(1, (T - 1).bit_length()), AG=AG),
        grid=(B, nq),
        in_specs=[pl.BlockSpec((tq, 512), qrow), pl.BlockSpec((tq, LANES), qrow),
                  pl.BlockSpec((T, LANES), full), pl.BlockSpec((tq, 512), qrow),
                  pl.BlockSpec((T, LANES), full), pl.BlockSpec((T, LANES), full)],
        out_specs=pl.BlockSpec((tq, 512), qrow),
        out_shape=jax.ShapeDtypeStruct(qb.shape, F32),
        scratch_shapes=[pltpu.VMEM((T // LANES, tq, LANES), I32), pltpu.VMEM((tq // LANES, LANES, LANES), I32),
                        pltpu.VMEM((2, tq, LANES), I32)],
        compiler_params=_cparams("parallel", "arbitrary"),
        name="dsa_prompt",
    )(qi, wi, ki2, qb, kb, vb)


def _dsa_score_sample_kernel(pt_ref, q_ref, w_ref, kn_ref, *rest, G, n_pages, n_real, n_sel, idx_bits):
    kc_refs = rest[:G]
    key_ref, thr_ref, jl_ref = rest[G:]
    j = pl.program_id(1)
    qrow = lax.broadcasted_iota(I32, (SROWS, PAGE), 0)
    kcol = lax.broadcasted_iota(I32, (SROWS, PAGE), 1)
    q = q_ref[0].astype(BF16)
    w = w_ref[0]

    def score_keys(s):
        s = jnp.maximum(s, 0.0) * w
        sc = s[0:SROWS]
        for h in range(1, H_IDX):
            sc = sc + s[h * SROWS:(h + 1) * SROWS]
        return _score_key(sc * _IDX_SCALE)

    @pl.when(j < n_pages // G)
    def _():
        kt = jnp.concatenate([kc_refs[s][...] for s in range(G)], axis=1).astype(BF16)
        key = score_keys(_dot(q, kt))
        for s in range(G):
            key_ref[0, j * G + s] = key[:, PAGE * s:PAGE * (s + 1)]

    @pl.when(j == n_pages // G)
    def _():
        key = score_keys(_dot_nt(q, kn_ref[0].astype(BF16)))
        key_ref[0, n_pages] = jnp.where(kcol <= qrow, key, INT_MIN)
        pos = lax.broadcasted_iota(I32, (G, SROWS, PAGE), 0) * PAGE + kcol[None]

        def count_ge(cands):
            cand, = cands

            def body(g, acc):
                blk = key_ref[0, pl.ds(g * G, G)]
                return acc + jnp.sum((blk >= cand[None]).astype(F32), axis=0)
            acc = lax.fori_loop(0, n_pages // G, body, jnp.zeros((SROWS, PAGE), F32))
            acc = acc + (key_ref[0, n_pages] >= cand).astype(F32)
            return (jnp.sum(acc, axis=1, keepdims=True).astype(I32),)

        def count_eq_lt(thrs, jcands):
            (thr,), (jcand,) = thrs, jcands

            def body(g, acc):
                blk = key_ref[0, pl.ds(g * G, G)]
                hit = (blk == thr[None]) & (pos + g * (G * PAGE) < jcand[None])
                return acc + jnp.sum(hit.astype(F32), axis=0)
            acc = lax.fori_loop(0, n_pages // G, body, jnp.zeros((SROWS, PAGE), F32))
            last = (key_ref[0, n_pages] == thr) & (kcol + n_pages * PAGE < jcand)
            return (jnp.sum(acc + last.astype(F32), axis=1, keepdims=True).astype(I32),)

        thr_ref[0], = _topk_threshold(count_ge, count_eq_lt, 1, SROWS, n_real, n_sel, idx_bits, jl_ref)


def _dsa_score_sample(pt, q, w, kn, kc, n_real, n_sel):
    B, NP = pt.shape
    G = _pages_per_step(NP)
    L = (NP + 1) * PAGE
    page = lambda s: pl.BlockSpec((None, D_IDX, PAGE),
                                  lambda b, j, pt: (pt[b, jnp.minimum(j * G + s, NP - 1)], 0, 0))
    grid_spec = pltpu.PrefetchScalarGridSpec(
        num_scalar_prefetch=1,
        grid=(B, NP // G + 1),
        in_specs=([pl.BlockSpec((1, H_IDX * SROWS, D_IDX), lambda b, j, pt: (b, 0, 0)),
                   pl.BlockSpec((1, H_IDX * SROWS, 1), lambda b, j, pt: (b, 0, 0)),
                   pl.BlockSpec((1, PAGE, D_IDX), lambda b, j, pt: (b, 0, 0))]
                  + [page(s) for s in range(G)]),
        out_specs=[pl.BlockSpec((1, NP + 1, SROWS, PAGE), lambda b, j, pt: (b, 0, 0, 0)),
                   pl.BlockSpec((1, SROWS, 1), lambda b, j, pt: (b, 0, 0)),
                   pl.BlockSpec((1, SROWS, 1), lambda b, j, pt: (b, 0, 0))])
    return pl.pallas_call(
        functools.partial(_dsa_score_sample_kernel, G=G, n_pages=NP, n_real=n_real, n_sel=n_sel,
                          idx_bits=max(1, (L - 1).bit_length())),
        grid_spec=grid_spec,
        out_shape=[jax.ShapeDtypeStruct((B, NP + 1, SROWS, PAGE), I32),
                   jax.ShapeDtypeStruct((B, SROWS, 1), I32),
                   jax.ShapeDtypeStruct((B, SROWS, 1), I32)],
        compiler_params=_cparams("parallel", "arbitrary"),
        name="dsa_score_sample",
    )(pt, q, w, kn, *([kc] * G))


_GR = (H_DSA // KV_DSA) * SROWS


def _dsa_att_sample_kernel(pt_ref, q_ref, key_ref, thr_ref, jl_ref, kn_ref, vn_ref, *rest, G, n_pages):
    kc_refs, vc_refs = rest[:G], rest[G:2 * G]
    o_ref, m_scr, l_scr, acc_scr = rest[2 * G:]
    j = pl.program_id(1)
    qrow = lax.broadcasted_iota(I32, (SROWS, PAGE), 0)
    kcol = lax.broadcasted_iota(I32, (SROWS, PAGE), 1)
    nrep = H_DSA // KV_DSA

    @pl.when(j == 0)
    def _():
        m_scr[...] = jnp.full(m_scr.shape, NEG_BIG, F32)
        l_scr[...] = jnp.zeros(l_scr.shape, F32)
        acc_scr[...] = jnp.zeros(acc_scr.shape, F32)

    def attend(zs, vs, sel, v_transposed=True):
        groups = range(KV_DSA)
        n = zs[0].shape[1]
        zs = [jnp.where(sel[None], z.reshape(nrep, SROWS, n), NEG_BIG).reshape(_GR, n) for z in zs]
        m = [m_scr[g][:, 0:1] for g in groups]
        l = [l_scr[g][:, 0:1] for g in groups]
        m_new = [jnp.maximum(m[g], jnp.max(zs[g], axis=1, keepdims=True)) for g in groups]
        alpha = [jnp.exp2(m[g] - m_new[g]) for g in groups]
        p = [jnp.exp2(zs[g] - m_new[g]) for g in groups]
        l = [alpha[g] * l[g] + jnp.sum(p[g], axis=1, keepdims=True) for g in groups]
        pb = [x.astype(BF16) for x in p]
        vb = [x.astype(BF16) for x in vs]
        pv = [_dot_nt(pb[g], vb[g]) if v_transposed else _dot(pb[g], vb[g]) for g in groups]
        for g in groups:
            acc_scr[g] = alpha[g] * acc_scr[g] + pv[g]
            m_scr[g] = jnp.broadcast_to(m_new[g], (_GR, LANES))
            l_scr[g] = jnp.broadcast_to(l[g], (_GR, LANES))

    def qgroup(g):
        return (q_ref[0, g] * ATT_SCALE).astype(BF16)

    thr = thr_ref[0]
    jl = jl_ref[0]

    @pl.when(j < n_pages // G)
    def _():
        sels = []
        for s in range(G):
            kblk = key_ref[0, j * G + s]
            sels.append((kblk > thr) | ((kblk == thr) & (kcol + (j * G + s) * PAGE <= jl)))
        sel = jnp.concatenate(sels, axis=1)
        kts = [jnp.concatenate([kc_refs[s][g] for s in range(G)], axis=1).astype(BF16) for g in range(KV_DSA)]
        vts = [jnp.concatenate([vc_refs[s][g] for s in range(G)], axis=1) for g in range(KV_DSA)]
        attend([_dot(qgroup(g), kts[g]) for g in range(KV_DSA)], vts, sel)

    @pl.when(j == n_pages // G)
    def _():
        off = n_pages * PAGE
        kblk = key_ref[0, n_pages]
        sel = ((kblk > thr) | ((kblk == thr) & (kcol + off <= jl))) & (kcol <= qrow)
        kns = [kn_ref[0, :, D_HEAD * g:D_HEAD * (g + 1)].astype(BF16) for g in range(KV_DSA)]
        vns = [vn_ref[0, :, D_HEAD * g:D_HEAD * (g + 1)] for g in range(KV_DSA)]
        attend([_dot_nt(qgroup(g), kns[g]) for g in range(KV_DSA)], vns, sel, v_transposed=False)
        for g in range(KV_DSA):
            o_ref[0, g] = acc_scr[g] / l_scr[g][:, 0:1]


def _dsa_att_sample(pt, q, keys, thr, jl, kn, vn, kc, vc):
    B, NP = pt.shape
    G = _pages_per_step(NP)
    page = lambda s: pl.BlockSpec((None, KV_DSA, D_HEAD, PAGE),
                                  lambda b, j, pt: (pt[b, jnp.minimum(j * G + s, NP - 1)], 0, 0, 0))
    per_b3 = lambda b, j, pt: (b, 0, 0)
    grid_spec = pltpu.PrefetchScalarGridSpec(
        num_scalar_prefetch=1,
        grid=(B, NP // G + 1),
        in_specs=([pl.BlockSpec((1, KV_DSA, _GR, D_HEAD), lambda b, j, pt: (b, 0, 0, 0)),
                   pl.BlockSpec((1, NP + 1, SROWS, PAGE), lambda b, j, pt: (b, 0, 0, 0)),
                   pl.BlockSpec((1, SROWS, 1), per_b3),
                   pl.BlockSpec((1, SROWS, 1), per_b3),
                   pl.BlockSpec((1, PAGE, LANES), per_b3),
                   pl.BlockSpec((1, PAGE, LANES), per_b3)]
                  + [page(s) for s in range(G)] * 2),
        out_specs=pl.BlockSpec((1, KV_DSA, _GR, D_HEAD), lambda b, j, pt: (b, 0, 0, 0)),
        scratch_shapes=[pltpu.VMEM((KV_DSA, _GR, LANES), F32), pltpu.VMEM((KV_DSA, _GR, LANES), F32),
                        pltpu.VMEM((KV_DSA, _GR, D_HEAD), F32)])
    return pl.pallas_call(
        functools.partial(_dsa_att_sample_kernel, G=G, n_pages=NP),
        grid_spec=grid_spec,
        out_shape=jax.ShapeDtypeStruct((B, KV_DSA, _GR, D_HEAD), F32),
        compiler_params=_cparams("parallel", "arbitrary"),
        name="dsa_att_sample",
    )(pt, q, keys, thr, jl, kn, vn, *([kc] * G), *([vc] * G))


def _mm_norm_res_kernel(*refs, n):
    a_refs, w_refs = refs[:n], refs[n:2 * n]
    g_ref, x_ref, o_ref = refs[2 * n:]
    acc = _dot(a_refs[0][...].astype(BF16), w_refs[0][...])
    for a_ref, w_ref in zip(a_refs[1:], w_refs[1:]):
        acc = acc + _dot(a_ref[...].astype(BF16), w_ref[...])
    o_ref[...] = x_ref[...] + _rms(acc, g_ref[...])


def _mm_norm_res(a_list, w_list, g, x, tm, name):
    M = x.shape[0]
    n = len(a_list)
    row = lambda i: (i, 0)
    const = lambda i: (0, 0)
    return pl.pallas_call(
        functools.partial(_mm_norm_res_kernel, n=n),
        grid=(M // tm,),
        in_specs=([pl.BlockSpec((tm, a.shape[1]), row) for a in a_list]
                  + [pl.BlockSpec(w.shape, const) for w in w_list]
                  + [pl.BlockSpec((1, D_MODEL), const), pl.BlockSpec((tm, D_MODEL), row)]),
        out_specs=pl.BlockSpec((tm, D_MODEL), row),
        out_shape=jax.ShapeDtypeStruct((M, D_MODEL), F32),
        compiler_params=_cparams("parallel"),
        name=name,
    )(*a_list, *w_list, g, x)


_GELU_C = math.sqrt(2.0 / math.pi)
FFN_SUB = 768


def _gelu_tanh(x):
    return x * (0.5 * (1.0 + jnp.tanh(_GELU_C * (x + 0.044715 * (x * x * x)))))


def _ffn_up_kernel(x_ref, g_ref, wg_ref, wv_ref, cwg_ref, cwv_ref, cbg_ref, cbv_ref, pg_ref, pv_ref,
                   act_ref, og_ref, ov_ref, eg_scr, ev_scr, *, tm, T, streamed):
    i = pl.program_id(1)
    h = _rms(x_ref[...], g_ref[...]).astype(BF16)
    trow = lax.broadcasted_iota(I32, (tm, 1), 0) % T

    tn = act_ref.shape[1]
    if streamed:
        @pl.when((i * tm) % T == 0)
        def _():
            eg_scr[6:8, :] = pg_ref[...]
            ev_scr[6:8, :] = pv_ref[...]
    else:
        eg_scr[0:8, :] = jnp.zeros((8, tn), F32)
        ev_scr[0:8, :] = jnp.zeros((8, tn), F32)

    rid = lax.broadcasted_iota(I32, (8, 1), 0)

    def conv(u, cs, cw_ref, cb_ref, p_ref, o_ref, e_scr):
        if streamed:
            o_ref[:, cs] = u[tm - 2:tm, :]
            prev2, prev1 = e_scr[6:7, cs], e_scr[7:8, cs]
            r1 = pltpu.roll(u, 1, 0)
            r2 = pltpu.roll(u, 2, 0)
            head1 = jnp.where(rid == 0, prev1, r1[0:8])
            head2 = jnp.where(rid == 0, prev2, jnp.where(rid == 1, prev1, r2[0:8]))
            u1 = jnp.concatenate([head1, r1[8:]], axis=0)
            u2 = jnp.concatenate([head2, r2[8:]], axis=0)
            e_scr[6:8, cs] = u[tm - 2:tm, :]
        else:
            e_scr[8:8 + tm, cs] = u
            o_ref[:, cs] = u
            u1 = jnp.where(trow >= 1, e_scr[7:7 + tm, cs], p_ref[0, :, cs])
            u2 = jnp.where(trow >= 2, e_scr[6:6 + tm, cs], p_ref[1, :, cs])
        return cb_ref[:, cs] + cw_ref[0:1, cs] * u2 + cw_ref[1:2, cs] * u1 + cw_ref[2:3, cs] * u

    subs = [slice(c0, min(c0 + FFN_SUB, tn)) for c0 in range(0, tn, FFN_SUB)]
    dots = lambda cs: (_dot(h, wg_ref[:, cs]), _dot(h, wv_ref[:, cs]))
    cur = dots(subs[0])
    for n, cs in enumerate(subs):
        nxt = dots(subs[n + 1]) if n + 1 < len(subs) else None
        gate = conv(cur[0], cs, cwg_ref, cbg_ref, pg_ref, og_ref, eg_scr)
        val = conv(cur[1], cs, cwv_ref, cbv_ref, pv_ref, ov_ref, ev_scr)
        act_ref[:, cs] = (_gelu_tanh(gate) * val).astype(BF16)
        cur = nxt


def _ffn_up(x, g, w_up, conv_w, conv_b, prev, T, tm, tn):
    M = x.shape[0]
    nj = D_FF // tn
    streamed = T >= tm
    tpb = max(T // tm, 1)
    if streamed:
        p_spec = lambda off: pl.BlockSpec((None, 2, tn), lambda j, i: (i // tpb, 0, j + off))
        o_shape = jax.ShapeDtypeStruct((M // T, 2, D_FF), F32)
        o_spec = pl.BlockSpec((None, 2, tn), lambda j, i: (i // tpb, 0, j))
    else:
        p_spec = lambda off: pl.BlockSpec((2, tm, tn), lambda j, i: (0, i, j + off))
        o_shape = jax.ShapeDtypeStruct((M, D_FF), F32)
        o_spec = pl.BlockSpec((tm, tn), lambda j, i: (i, j))
    col = lambda rows, off: pl.BlockSpec((rows, tn), lambda j, i: (0, j + off))
    return pl.pallas_call(
        functools.partial(_ffn_up_kernel, tm=tm, T=T, streamed=streamed),
        grid=(nj, M // tm),
        in_specs=[pl.BlockSpec((tm, D_MODEL), lambda j, i: (i, 0)),
                  pl.BlockSpec((1, D_MODEL), lambda j, i: (0, 0)),
                  col(D_MODEL, 0), col(D_MODEL, nj), col(3, 0), col(3, nj), col(1, 0), col(1, nj),
                  p_spec(0), p_spec(nj)],
        out_specs=[pl.BlockSpec((tm, tn), lambda j, i: (i, j)), o_spec, o_spec],
        out_shape=[jax.ShapeDtypeStruct((M, D_FF), BF16), o_shape, o_shape],
        scratch_shapes=[pltpu.VMEM((tm + 8, tn), F32), pltpu.VMEM((tm + 8, tn), F32)],
        compiler_params=_cparams("parallel", "arbitrary"),
        name="ffn_up",
    )(x, g, w_up, w_up, conv_w, conv_w, conv_b, conv_b, prev, prev)


def _rk_pre_kernel(x_ref, g_ref, sp_ref, mix_ref, wr_ref, wk_ref, wv_ref, w1_ref, w2_ref, a1_ref, a2_ref,
                   g1_ref, g2_ref, vec_ref,
                   hn_ref, r_ref, wl_ref, kf_ref, v_ref, av_ref, bv_ref, gg_ref, e_scr, *, tm, T, streamed):
    i = pl.program_id(0)
    h = _rms(x_ref[...], g_ref[...])
    hn_ref[...] = h
    e_scr[8:8 + tm, :] = h
    if streamed:
        @pl.when((i * tm) % T == 0)
        def _():
            e_scr[7:8, :] = sp_ref[...]
        xp = e_scr[7:7 + tm, :]
        e_scr[7:8, :] = h[tm - 1:tm, :]
    else:
        e_scr[0:8, :] = jnp.zeros((8, D_MODEL), F32)
        trow = lax.broadcasted_iota(I32, (tm, 1), 0) % T
        xp = jnp.where(trow >= 1, e_scr[7:7 + tm, :], sp_ref[...])
    xx = xp - h
    mixed = lambda j: (h + xx * mix_ref[j:j + 1, :]).astype(BF16)
    w0, a0, k_k, k_a = vec_ref[0:1, :], vec_ref[1:2, :], vec_ref[2:3, :], vec_ref[3:4, :]
    r = _dot(mixed(0), wr_ref[...])
    k = _dot(mixed(2), wk_ref[...])
    v = _dot(mixed(3), wv_ref[...])
    lw = w0 + _dot(jnp.tanh(_dot(mixed(1), w1_ref[...])).astype(BF16), w2_ref[...])
    w_log = -(jnp.maximum(-lw, 0.0) + jnp.log(1.0 + jnp.exp(-jnp.abs(lw)))) - 0.5
    wl_ref[...] = -jnp.exp(w_log)
    a = jax.nn.sigmoid(a0 + _dot(_dot(mixed(4), a1_ref[...]).astype(BF16), a2_ref[...]))
    gg_ref[...] = _dot(jax.nn.sigmoid(_dot(mixed(5), g1_ref[...])).astype(BF16), g2_ref[...])
    kk = k * k_k
    bd = _seg_ones()
    kk = kk / jnp.maximum(jnp.sqrt(_seg64_sum(kk * kk, bd)), 1e-12)
    r_ref[...] = r
    v_ref[...] = v
    kf_ref[...] = k * (1.0 + (a - 1.0) * k_a)
    av_ref[...] = -kk
    bv_ref[...] = kk * a


def _rk_pre(x, g, sp, mix, wr, wk, wv, w1, w2, a1, a2, g1, g2, vecs, T, tm):
    M = x.shape[0]
    streamed = T >= tm
    tpb = max(T // tm, 1)
    row = lambda i: (i, 0)
    const = lambda i: (0, 0)
    sp_spec = (pl.BlockSpec((None, 1, D_MODEL), lambda i: (i // tpb, 0, 0)) if streamed
               else pl.BlockSpec((tm, D_MODEL), row))
    full = lambda a: pl.BlockSpec(a.shape, const)
    return pl.pallas_call(
        functools.partial(_rk_pre_kernel, tm=tm, T=T, streamed=streamed),
        grid=(M // tm,),
        in_specs=[pl.BlockSpec((tm, D_MODEL), row), pl.BlockSpec((1, D_MODEL), const), sp_spec,
                  full(mix), full(wr), full(wk), full(wv), full(w1), full(w2), full(a1), full(a2),
                  full(g1), full(g2), full(vecs)],
        out_specs=[pl.BlockSpec((tm, D_MODEL), row)] * 8,
        out_shape=[jax.ShapeDtypeStruct((M, D_MODEL), F32)] * 8,
        scratch_shapes=[pltpu.VMEM((tm + 8, D_MODEL), F32)],
        compiler_params=_cparams("arbitrary"),
        name="rk_pre",
    )(x, g, sp, mix, wr, wk, wv, w1, w2, a1, a2, g1, g2, vecs)


def _bdot(a, b):
    return _dot(a.astype(BF16), b.astype(BF16))


def _rk_scan_kernel(r_ref, wl_ref, k_ref, v_ref, a_ref, b_ref, s0_ref, y_ref, sf_ref, s_scr, *, C, npair):
    c = pl.program_id(2)

    @pl.when(c == 0)
    def _():
        s_scr[...] = s0_ref[0]

    lane = lax.broadcasted_iota(I32, (1, LANES), 1)
    m0 = lane < 64
    rr = lax.broadcasted_iota(I32, (C, C), 0)
    cc = lax.broadcasted_iota(I32, (C, C), 1)
    tri_incl = (cc <= rr).astype(BF16)
    r2 = lax.broadcasted_iota(I32, (2 * C, 2 * C), 0)
    c2 = lax.broadcasted_iota(I32, (2 * C, 2 * C), 1)
    strict = (r2 % C) > (c2 % C)
    incl = (r2 % C) >= (c2 % C)
    incl2 = jnp.concatenate([incl, incl], axis=1)
    eye = (r2 == c2).astype(F32)
    n_double = max(int(math.log2(C)) - 1, 0)

    def stack2(z):
        return jnp.concatenate([jnp.where(m0, z, 0.0), jnp.where(m0, 0.0, z)], axis=0)

    pairs = range(npair)
    sls = [slice(LANES * p, LANES * (p + 1)) for p in pairs]
    wl = [wl_ref[:, sl] for sl in sls]
    ld = []
    for p in pairs:
        wl_hi, wl_lo = _split(wl[p])
        ld.append(_dot(tri_incl, wl_hi) + _dot(tri_incl, wl_lo))
    dfull = [jnp.exp(x) for x in ld]
    dinv = [jnp.exp(-x) for x in ld]
    As = [stack2(a_ref[:, sls[p]] * jnp.exp(ld[p] - wl[p])).astype(BF16) for p in pairs]
    Bs = [stack2(b_ref[:, sls[p]] * dinv[p]) for p in pairs]
    Ks = [stack2(k_ref[:, sls[p]] * dinv[p]) for p in pairs]
    Rs = [stack2(r_ref[:, sls[p]] * dfull[p]).astype(BF16) for p in pairs]
    Vs = [stack2(v_ref[:, sls[p]]) for p in pairs]
    Vb = [x.astype(BF16) for x in Vs]
    BK = [jnp.concatenate([Bs[p], Ks[p]], axis=0).astype(BF16) for p in pairs]
    AR = [jnp.concatenate([As[p], Rs[p]], axis=0) for p in pairs]
    GG = [_dot_nt(AR[p], BK[p]) for p in pairs]
    Lab = [jnp.where(strict, g[:2 * C, :2 * C], 0.0) for g in GG]
    Lak = [jnp.where(strict, g[:2 * C, 2 * C:], 0.0).astype(BF16) for g in GG]
    Mr = [jnp.where(incl2, g[2 * C:], 0.0).astype(BF16) for g in GG]
    Tm = [eye + x for x in Lab]
    if n_double > 0:
        Pb = [x.astype(BF16) for x in Lab]
        Pb = [_dot(x, x).astype(BF16) for x in Pb]
        for rnd in range(n_double):
            Tb = [x.astype(BF16) for x in Tm]
            if rnd + 1 < n_double:
                X = [_dot(Pb[p], jnp.concatenate([Pb[p], Tb[p]], axis=1)) for p in pairs]
                Pb = [x[:, :2 * C].astype(BF16) for x in X]
                Tm = [Tm[p] + X[p][:, 2 * C:] for p in pairs]
            else:
                Tm = [Tm[p] + _dot(Pb[p], Tb[p]) for p in pairs]
    S = [s_scr[p] for p in pairs]
    Sb = [x.astype(BF16) for x in S]
    ARS = [_dot_nt(AR[p], Sb[p]) for p in pairs]
    rhs = [ARS[p][:2 * C] + _dot(Lak[p], Vb[p]) for p in pairs]
    U = [_bdot(Tm[p], rhs[p]) for p in pairs]
    UV = [jnp.concatenate([U[p], Vs[p]], axis=0) for p in pairs]
    UVb = [x.astype(BF16) for x in UV]
    Ys = [ARS[p][2 * C:] + _dot(Mr[p], UVb[p]) for p in pairs]
    upd = [_dot(UV[p].T.astype(BF16), BK[p]) for p in pairs]
    for p in pairs:
        y_ref[:, sls[p]] = Ys[p][:C] + Ys[p][C:]
        s_scr[p] = (S[p] + upd[p]) * dfull[p][C - 1:C, :]

    @pl.when(c == pl.num_programs(2) - 1)
    def _():
        sf_ref[0] = s_scr[...]


def _rk_scan(r, wl, k, v, a, b, s0, B, T, C, npair):
    nc = T // C
    ng = 8 // npair
    blk = pl.BlockSpec((C, LANES * npair), lambda bb, g, c: (bb * nc + c, g))
    s_spec = pl.BlockSpec((1, npair, LANES, LANES), lambda bb, g, c: (bb, g, 0, 0))
    return pl.pallas_call(
        functools.partial(_rk_scan_kernel, C=C, npair=npair),
        grid=(B, ng, nc),
        in_specs=[blk] * 6 + [s_spec],
        out_specs=[blk, s_spec],
        out_shape=[jax.ShapeDtypeStruct(r.shape, F32), jax.ShapeDtypeStruct(s0.shape, F32)],
        scratch_shapes=[pltpu.VMEM((npair, LANES, LANES), F32)],
        compiler_params=_cparams("parallel", "parallel", "arbitrary"),
        name="rk_scan",
    )(r, wl, k, v, a, b, s0)


def _rk_post_kernel(y_ref, r_ref, kf_ref, v_ref, gg_ref, vec_ref, o_ref):
    bd = _seg_ones()
    ln_w, ln_b, r_k = vec_ref[0:1, :], vec_ref[1:2, :], vec_ref[2:3, :]
    y = y_ref[...]
    mu = _seg64_sum(y, bd) * (1.0 / RK_N)
    d = y - mu
    var = _seg64_sum(d * d, bd) * (1.0 / RK_N)
    yn = d * lax.rsqrt(var + RK_GN_EPS) * ln_w + ln_b
    bonus = _seg64_sum(r_ref[...] * kf_ref[...] * r_k, bd) * v_ref[...]
    o_ref[...] = ((yn + bonus) * gg_ref[...]).astype(BF16)


def _rk_post(y, r, kf, v, gg, vecs, tm):
    M = y.shape[0]
    row = lambda i: (i, 0)
    return pl.pallas_call(
        _rk_post_kernel,
        grid=(M // tm,),
        in_specs=[pl.BlockSpec((tm, D_MODEL), row)] * 5 + [pl.BlockSpec(vecs.shape, lambda i: (0, 0))],
        out_specs=pl.BlockSpec((tm, D_MODEL), row),
        out_shape=jax.ShapeDtypeStruct((M, D_MODEL), BF16),
        compiler_params=_cparams("parallel"),
        name="rk_post",
    )(y, r, kf, v, gg, vecs)


def _row_tile(M):
    return min(256, M)


def _pad_rows(a, n):
    return jnp.pad(a, ((0, 0), (0, n - a.shape[1])) + ((0, 0),) * (a.ndim - 2))


def _attn_layer_prompt(x, B, T, g_pre, w_in, w_outs, g_post):
    M = B * T
    tm = _row_tile(M)
    cos, sin = _rope_tables(jnp.arange(T, dtype=I32))
    qa, ka, va, qb, kb, vb, qi, ki2, wi = _attn_proj(x, g_pre, w_in, cos, sin, tm)
    n_sel = min(DSA_TOPK, T // 4)
    oa = _sb_prompt(qa, ka, va, B, T, min(256, T))
    ob = _dsa_prompt(qi, wi, ki2, qb, kb, vb, B, T, min(256, T), n_sel)
    x = _mm_norm_res([oa, ob], w_outs, g_post, x, tm, "attn_out")
    rows = (ka.reshape(1, B, T, H_SB, D_HEAD), va.reshape(1, B, T, H_SB, D_HEAD),
            kb.reshape(1, B, T, KV_DSA, D_HEAD), vb.reshape(1, B, T, KV_DSA, D_HEAD),
            ki2[:, :D_IDX].reshape(1, B, T, D_IDX))
    return x, rows


def _attn_layer_sample(x, B, T, P, page_table, caches, g_pre, w_in, w_outs, g_post):
    M = B * T
    cos, sin = _rope_tables(P + jnp.arange(T, dtype=I32))
    cos, sin = jnp.tile(cos, (B, 1)), jnp.tile(sin, (B, 1))
    qa, ka, va, qb, kb, vb, qi, ki2, wi = _attn_proj(x, g_pre, w_in, cos, sin, M)
    c_sb_k, c_sb_v, c_dsa_k, c_dsa_v, c_idx = caches
    n_pool = c_sb_k.shape[0]
    n_sel = min(DSA_TOPK, (P + T) // 4)
    b3 = lambda a: a.reshape(B, T, a.shape[-1])

    row_minor = lambda c: jnp.moveaxis(c, 1, -1)
    oa = _sb_sample(page_table, _pad_rows(b3(qa), SROWS), _pad_rows(b3(ka), PAGE), _pad_rows(b3(va), PAGE),
                    row_minor(c_sb_k), row_minor(c_sb_v), T)
    oa = oa[:, :T].reshape(M, H_SB * D_HEAD)

    qi_s = _pad_rows(b3(qi).reshape(B, T, H_IDX, D_IDX), SROWS).transpose(0, 2, 1, 3).reshape(B, H_IDX * SROWS, D_IDX)
    wi_s = _pad_rows(b3(wi)[:, :, :H_IDX], SROWS).transpose(0, 2, 1).reshape(B, H_IDX * SROWS, 1)
    keys, thr, jl = _dsa_score_sample(page_table, qi_s, wi_s, _pad_rows(b3(ki2)[:, :, :D_IDX], PAGE),
                                      row_minor(c_idx), T, n_sel)
    qb_s = _pad_rows(b3(qb).reshape(B, T, H_DSA, D_HEAD), SROWS).transpose(0, 2, 1, 3).reshape(B, KV_DSA, _GR, D_HEAD)
    ob = _dsa_att_sample(page_table, qb_s, keys, thr, jl, _pad_rows(b3(kb), PAGE), _pad_rows(b3(vb), PAGE),
                         row_minor(c_dsa_k), row_minor(c_dsa_v))
    ob = ob.reshape(B, H_DSA, SROWS, D_HEAD)[:, :, :T].transpose(0, 2, 1, 3).reshape(M, H_DSA * D_HEAD)

    x = _mm_norm_res([oa, ob], w_outs, g_post, x, M, "attn_out")
    rows = (ka.reshape(1, B, T, H_SB, D_HEAD), va.reshape(1, B, T, H_SB, D_HEAD),
            kb.reshape(1, B, T, KV_DSA, D_HEAD), vb.reshape(1, B, T, KV_DSA, D_HEAD),
            ki2[:, :D_IDX].reshape(1, B, T, D_IDX))
    return x, rows


def _pair_state(S):
    B = S.shape[0]
    S = S.reshape(B, 8, 2, RK_N, RK_N)
    z = jnp.zeros_like(S[:, :, 0])
    top = jnp.concatenate([S[:, :, 0], z], axis=-1)
    bot = jnp.concatenate([z, S[:, :, 1]], axis=-1)
    return jnp.concatenate([top, bot], axis=-2)


def _unpair_state(Sp):
    B = Sp.shape[0]
    return jnp.stack([Sp[:, :, :RK_N, :RK_N], Sp[:, :, RK_N:, RK_N:]], axis=2).reshape(B, RK_H, RK_N, RK_N)


RK_CHUNK = 64
RK_PAIRS = 8


def _rwkv_layer(x, B, T, shift_prev, S0, g_pre, prm, g_post):
    M = B * T
    tm = _row_tile(M)
    (mix, wr, wk, wv, wo, w1, w2, a1, a2, g1, g2, vec_pre, vec_post) = prm
    if T >= tm:
        sp = shift_prev.reshape(B, 1, D_MODEL)
    else:
        sp = jnp.repeat(shift_prev, T, axis=0)
    hn, r, wl, kf, v, av, bv, gg = _rk_pre(x, g_pre, sp, mix, wr, wk, wv, w1, w2, a1, a2, g1, g2, vec_pre, T, tm)
    Tp = -(-T // RK_CHUNK) * RK_CHUNK
    if Tp != T:
        padt = lambda a: _pad_rows(a.reshape(B, T, D_MODEL), Tp).reshape(B * Tp, D_MODEL)
        y, Sf = _rk_scan(padt(r), padt(wl), padt(kf), padt(v), padt(av), padt(bv), _pair_state(S0), B, Tp, RK_CHUNK, RK_PAIRS)
        y = y.reshape(B, Tp, D_MODEL)[:, :T].reshape(M, D_MODEL)
    else:
        y, Sf = _rk_scan(r, wl, kf, v, av, bv, _pair_state(S0), B, T, RK_CHUNK, RK_PAIRS)
    z = _rk_post(y, r, kf, v, gg, vec_post, tm)
    x = _mm_norm_res([z], [wo], g_post, x, tm, "rk_out")
    shift = hn.reshape(B, T, D_MODEL)[:, -1]
    return x, shift, _unpair_state(Sf)


def _ffn_layer(x, B, T, prev, g_pre, w_up, conv_w, conv_b, w_down, g_post):
    M = B * T
    tm = _row_tile(M)
    tn = D_FF // 2
    if T >= tm:
        act, cg, cv = _ffn_up(x, g_pre, w_up, conv_w, conv_b, prev, T, tm, tn)
        conv_state = jnp.concatenate([cg, cv], axis=-1)
    else:
        zeros = jnp.zeros((B, T, 2 * D_FF), F32)
        p1 = zeros.at[:, 0].set(prev[:, 1])
        p2 = zeros.at[:, 0].set(prev[:, 0]).at[:, 1].set(prev[:, 1])
        pp = jnp.stack([p1.reshape(M, -1), p2.reshape(M, -1)])
        act, ug, uv = _ffn_up(x, g_pre, w_up, conv_w, conv_b, pp, T, tm, tn)
        u = jnp.concatenate([ug, uv], axis=-1).reshape(B, T, 2 * D_FF)
        conv_state = jnp.concatenate([prev, u], axis=1)[:, -2:]
    x = _mm_norm_res([act], [w_down], g_post, x, tm, "ffn_down")
    return x, conv_state


def _forward(x_prompt, x_sample, cache_sb_k, cache_sb_v, cache_dsa_k, cache_dsa_v, cache_idx_k, page_table,
             state_wkv, state_shift, state_ffn_conv,
             norm_mix_pre, norm_mix_post, norm_ffn_pre, norm_ffn_post,
             att_w_in, att_w_out,
             rk_mix, rk_w_r, rk_w_k, rk_w_v, rk_w_o, rk_w0, rk_w1, rk_w2, rk_a0, rk_a1, rk_a2,
             rk_g1, rk_g2, rk_k_k, rk_k_a, rk_r_k, rk_ln_w, rk_ln_b,
             ffn_w_up, ffn_conv_w, ffn_conv_b, ffn_w_down):
    Bp, Tp, D = x_prompt.shape
    Bs, Ts, _ = x_sample.shape
    P = page_table.shape[1] * PAGE
    depth = norm_mix_pre.shape[0]
    xp = x_prompt.reshape(Bp * Tp, D)
    xs = x_sample.reshape(Bs * Ts, D)
    vrow = lambda a: a.reshape(1, -1)
    bf = lambda a: a.astype(BF16)
    att_p, att_s, wkv_p, wkv_s, sh_p, sh_s, cv_p, cv_s = [], [], [], [], [], [], [], []
    for i in range(depth):
        li = i // 2
        g_pre, g_post = vrow(norm_mix_pre[i]), vrow(norm_mix_post[i])
        if i % 2 == 0:
            w_in = _pack_w_in(att_w_in[li])
            w_outs = [bf(att_w_out[li][:H_SB * D_HEAD]), bf(att_w_out[li][H_SB * D_HEAD:])]
            caches = (cache_sb_k[li], cache_sb_v[li], cache_dsa_k[li], cache_dsa_v[li], cache_idx_k[li])
            xp, rows_p = _attn_layer_prompt(xp, Bp, Tp, g_pre, w_in, w_outs, g_post)
            xs, rows_s = _attn_layer_sample(xs, Bs, Ts, P, page_table, caches, g_pre, w_in, w_outs, g_post)
            att_p.append(rows_p)
            att_s.append(rows_s)
        else:
            vec_pre = jnp.stack([rk_w0[li], rk_a0[li], rk_k_k[li], rk_k_a[li]])
            vec_post = jnp.stack([rk_ln_w[li], rk_ln_b[li], rk_r_k[li].reshape(-1)])
            prm = (rk_mix[li], bf(rk_w_r[li]), bf(rk_w_k[li]), bf(rk_w_v[li]), bf(rk_w_o[li]),
                   bf(rk_w1[li]), bf(rk_w2[li]), bf(rk_a1[li]), bf(rk_a2[li]), bf(rk_g1[li]), bf(rk_g2[li]),
                   vec_pre, vec_post)
            xp, shp, Sp = _rwkv_layer(xp, Bp, Tp, jnp.zeros((Bp, D), F32), jnp.zeros((Bp, RK_H, RK_N, RK_N), F32),
                                      g_pre, prm, g_post)
            xs, shs, Ss = _rwkv_layer(xs, Bs, Ts, state_shift[li], state_wkv[li], g_pre, prm, g_post)
            wkv_p.append(Sp)
            wkv_s.append(Ss)
            sh_p.append(shp)
            sh_s.append(shs)
        f_pre, f_post = vrow(norm_ffn_pre[i]), vrow(norm_ffn_post[i])
        w_up, w_down = bf(ffn_w_up[i]), bf(ffn_w_down[i])
        cb = vrow(ffn_conv_b[i])
        xp, cp = _ffn_layer(xp, Bp, Tp, jnp.zeros((Bp, 2, 2 * D_FF), F32), f_pre, w_up, ffn_conv_w[i], cb, w_down, f_post)
        xs, cs = _ffn_layer(xs, Bs, Ts, state_ffn_conv[i], f_pre, w_up, ffn_conv_w[i], cb, w_down, f_post)
        cv_p.append(cp)
        cv_s.append(cs)
    cat = lambda rows, j: jnp.concatenate([r[j] for r in rows], axis=0)
    return (xp.reshape(Bp, Tp, D), xs.reshape(Bs, Ts, D),
            cat(att_p, 0), cat(att_s, 0), cat(att_p, 1), cat(att_s, 1),
            cat(att_p, 2), cat(att_s, 2), cat(att_p, 3), cat(att_s, 3),
            cat(att_p, 4), cat(att_s, 4),
            jnp.stack(wkv_p), jnp.stack(wkv_s), jnp.stack(sh_p), jnp.stack(sh_s),
            jnp.stack(cv_p), jnp.stack(cv_s))


def kernel(x_prompt, x_sample, cache_sb_k, cache_sb_v, cache_dsa_k, cache_dsa_v, cache_idx_k, page_table, state_wkv, state_shift, state_ffn_conv, norm_mix_pre, norm_mix_post, norm_ffn_pre, norm_ffn_post, att_w_in, att_w_out, rk_mix, rk_w_r, rk_w_k, rk_w_v, rk_w_o, rk_w0, rk_w1, rk_w2, rk_a0, rk_a1, rk_a2, rk_g1, rk_g2, rk_k_k, rk_k_a, rk_r_k, rk_ln_w, rk_ln_b, ffn_w_up, ffn_conv_w, ffn_conv_b, ffn_w_down):
    return _forward(x_prompt, x_sample, cache_sb_k, cache_sb_v, cache_dsa_k, cache_dsa_v, cache_idx_k, page_table,
                    state_wkv, state_shift, state_ffn_conv,
                    norm_mix_pre, norm_mix_post, norm_ffn_pre, norm_ffn_post,
                    att_w_in, att_w_out,
                    rk_mix, rk_w_r, rk_w_k, rk_w_v, rk_w_o, rk_w0, rk_w1, rk_w2, rk_a0, rk_a1, rk_a2,
                    rk_g1, rk_g2, rk_k_k, rk_k_a, rk_r_k, rk_ln_w, rk_ln_b,
                    ffn_w_up, ffn_conv_w, ffn_conv_b, ffn_w_down)
```
